```python
import math
import jax
import jax.numpy as jnp
from jax import lax
import numpy as np

D_MODEL = 1024
BATCH = 4
SEQ = 8192
DEPTH = 2

PLE_DIM = 256
NORM_EPS = 1e-6
N_EVEN = (DEPTH + 1) // 2
N_ODD = DEPTH // 2
CHUNK = 64
MIX_WIDTH = D_MODEL
GLA_HEADS = 4
GLA_DV = MIX_WIDTH // (2 * GLA_HEADS)
GLA_DK = GLA_DV // 2
GLA_GATE_RANK = 16
GLA_GATE_NORM = 16.0
RET_HEADS = 4
RET_DV = MIX_WIDTH // (2 * RET_HEADS)
RET_DK = RET_DV // 2
ROPE_BASE = 10000.0
EVEN_SPLIT_SIZES = (GLA_HEADS * GLA_DK, GLA_HEADS * GLA_DK, GLA_HEADS * GLA_DV, GLA_HEADS * GLA_DV, GLA_GATE_RANK, RET_HEADS * RET_DK, RET_HEADS * RET_DK, RET_HEADS * RET_DV, RET_HEADS * RET_DV)
EVEN_PROJ = sum(EVEN_SPLIT_SIZES)
POOL_WINDOWS = (2, 4, 8, 16)
POOL_GROUPS = 4
POOL_WIDTH = MIX_WIDTH // 2
POOL_GROUP_WIDTH = POOL_WIDTH // POOL_GROUPS
S5_H = 16
S5_P = 64
S5_WIDTH = MIX_WIDTH - POOL_WIDTH
S5_GROUPS = S5_WIDTH // S5_H
DT_MIN = 1e-3
DT_MAX = 1e-1
PEER_HEADS = 8
PEER_NKEYS = 128
PEER_EXPERTS = PEER_NKEYS * PEER_NKEYS
PEER_TOPK = 16
PEER_DKEY = 256
PEER_BLOCK = 128

kernel_name = 'hybrid_gla_retnet_pool_s5_peer'


def rms_norm(x, w):
    xf = x.astype(jnp.float32)
    y = xf * lax.rsqrt(jnp.mean(xf * xf, axis=-1, keepdims=True) + NORM_EPS)
    return (y * w.astype(jnp.float32)).astype(x.dtype)


def group_norm(x):
    xf = x.astype(jnp.float32)
    mu = jnp.mean(xf, axis=-1, keepdims=True)
    var = jnp.mean(jnp.square(xf - mu), axis=-1, keepdims=True)
    return (xf - mu) * lax.rsqrt(var + NORM_EPS)


def to_heads(t, n_heads):
    b, s, _ = t.shape
    return t.reshape(b, s, n_heads, -1).transpose(0, 2, 1, 3)


def from_heads(t):
    b, h, s, d = t.shape
    return t.transpose(0, 2, 1, 3).reshape(b, s, h * d)


def rotary(x, positions):
    half = x.shape[-1] // 2
    freqs = ROPE_BASE ** (-jnp.arange(half, dtype=jnp.float32) / half)
    ang = positions.astype(jnp.float32)[:, None] * freqs[None, :]
    cos, sin = jnp.cos(ang), jnp.sin(ang)
    xf = x.astype(jnp.float32)
    x1, x2 = xf[..., :half], xf[..., half:]
    return jnp.concatenate([x1 * cos - x2 * sin, x2 * cos + x1 * sin], axis=-1).astype(x.dtype)


def to_chunks(t):
    b, h, s, d = t.shape
    return t.astype(jnp.float32).reshape(b, h, s // CHUNK, CHUNK, d)


def scan_chunk_states(kv, decay):
    decay = jnp.broadcast_to(decay, kv.shape)

    def step(state, inp):
        kv_n, d_n = inp
        return d_n * state + kv_n, state

    init = jnp.zeros(kv.shape[:2] + kv.shape[3:], kv.dtype)
    _, prev = lax.scan(step, init, (jnp.moveaxis(kv, 2, 0), jnp.moveaxis(decay, 2, 0)))
    return jnp.moveaxis(prev, 0, 2)


def gla_chunked(q, k, v, log_a):
    b, h, s, _ = q.shape
    q, k, v, log_a = to_chunks(q), to_chunks(k), to_chunks(v), to_chunks(log_a)
    g = jnp.cumsum(log_a, axis=3)
    g_last = g[:, :, :, -1:, :]
    ref = 0.5 * g_last
    causal = jnp.tril(jnp.ones((CHUNK, CHUNK), dtype=bool))
    scores = jnp.einsum('bhnid,bhnjd->bhnij', q * jnp.exp(g - ref), k * jnp.exp(ref - g))
    scores = jnp.where(causal, scores, 0.0)
    o_intra = jnp.einsum('bhnij,bhnje->bhnie', scores, v)
    kv = jnp.einsum('bhncd,bhnce->bhnde', k * jnp.exp(g_last - g), v)
    prev = scan_chunk_states(kv, jnp.exp(g_last[:, :, :, 0, :])[..., None])
    o_inter = jnp.einsum('bhncd,bhnde->bhnce', q * jnp.exp(g), prev)
    return (o_intra + o_inter).reshape(b, h, s, -1)


def retention_chunked(q, k, v, log_gamma):
    b, h, s, _ = q.shape
    q, k, v = to_chunks(q), to_chunks(k), to_chunks(v)
    idx = jnp.arange(CHUNK, dtype=jnp.float32)
    diff = idx[:, None] - idx[None, :]
    lg = log_gamma[:, None, None]
    decay_mask = jnp.where(diff >= 0, jnp.exp(lg * jnp.maximum(diff, 0.0)), 0.0)
    scores = jnp.einsum('bhnid,bhnjd->bhnij', q, k) * decay_mask[None, :, None]
    o_intra = jnp.einsum('bhnij,bhnje->bhnie', scores, v)
    q_dec = jnp.exp(log_gamma[:, None] * (idx + 1.0))
    k_dec = jnp.exp(log_gamma[:, None] * (CHUNK - 1.0 - idx))
    kv = jnp.einsum('bhncd,bhnce->bhnde', k * k_dec[None, :, None, :, None], v)
    prev = scan_chunk_states(kv, jnp.exp(log_gamma * CHUNK)[None, :, None, None, None])
    o_inter = jnp.einsum('bhncd,bhnde->bhnce', q, prev) * q_dec[None, :, None, :, None]
    return (o_intra + o_inter).reshape(b, h, s, -1)


def even_mixer(xn, positions, log_gamma, w_in, gla_w_up, gla_b_up, gla_norm_w, ret_norm_w, w_out):
    split_at = np.cumsum(EVEN_SPLIT_SIZES)[:-1].tolist()
    qa, ka, va, ga, ra, qb, kb, vb, gb = jnp.split(xn @ w_in, split_at, axis=-1)
    log_a = jax.nn.log_sigmoid((ra @ gla_w_up + gla_b_up).astype(jnp.float32)) / GLA_GATE_NORM
    o_a = gla_chunked(to_heads(qa, GLA_HEADS) * GLA_DK ** -0.5, to_heads(ka, GLA_HEADS),
                      to_heads(va, GLA_HEADS), to_heads(log_a, GLA_HEADS))
    o_a = from_heads(rms_norm(o_a, gla_norm_w)).astype(xn.dtype) * jax.nn.silu(ga)
    qb = rotary(to_heads(qb, RET_HEADS), positions)
    kb = rotary(to_heads(kb, RET_HEADS), positions) * RET_DK ** -0.5
    o_b = retention_chunked(qb, kb, to_heads(vb, RET_HEADS), log_gamma)
    o_b = (from_heads(group_norm(o_b)) * ret_norm_w).astype(xn.dtype) * jax.nn.silu(gb)
    return jnp.concatenate([o_a, o_b], axis=-1) @ w_out


def pool_mixer(u, pool_w, pool_scale):
    b, s, _ = u.shape
    ug = u.reshape(b, s, POOL_GROUPS, POOL_GROUP_WIDTH).astype(jnp.float32)
    cs = jnp.cumsum(ug, axis=1)
    t = jnp.arange(s)
    outs = []
    for g, win in enumerate(POOL_WINDOWS):
        c_g = cs[:, :, g]
        lagged = jnp.pad(c_g, ((0, 0), (win, 0), (0, 0)))[:, :s]
        count = jnp.minimum(t + 1, win).astype(jnp.float32)[None, :, None]
        outs.append((c_g - lagged) / count - ug[:, :, g])
    pooled = jnp.stack(outs, axis=2).astype(u.dtype)
    mixed = jnp.einsum('bsgc,gcd->bsgd', pooled, pool_w)
    return mixed.reshape(b, s, POOL_WIDTH) * pool_scale


def s5_mixer(u, a_re, a_im, log_dt, b_re, b_im, c_re, c_im, d_skip, w_glu, b_glu):
    b, s, _ = u.shape
    ug = u.reshape(b, s, S5_GROUPS, S5_H).astype(jnp.float32)
    dt = jnp.exp(log_dt.astype(jnp.float32))[:, None]
    a_re = a_re.astype(jnp.float32)
    a_im = a_im.astype(jnp.float32)
    mag = jnp.exp(a_re * dt)
    abar_re, abar_im = mag * jnp.cos(a_im * dt), mag * jnp.sin(a_im * dt)
    den = a_re * a_re + a_im * a_im
    nr, ni = abar_re - 1.0, abar_im
    coef_re = (nr * a_re + ni * a_im) / den
    coef_im = (ni * a_re - nr * a_im) / den
    bbar_re = coef_re[..., None] * b_re - coef_im[..., None] * b_im
    bbar_im = coef_re[..., None] * b_im + coef_im[..., None] * b_re
    bu_re = jnp.einsum('bsgh,gph->bsgp', ug, bbar_re)
    bu_im = jnp.einsum('bsgh,gph->bsgp', ug, bbar_im)
    lam_re = jnp.broadcast_to(abar_re, (1, s) + abar_re.shape)
    lam_im = jnp.broadcast_to(abar_im, (1, s) + abar_im.shape)

    def combine(e1, e2):
        a1r, a1i, b1r, b1i = e1
        a2r, a2i, b2r, b2i = e2
        return (a2r * a1r - a2i * a1i, a2r * a1i + a2i * a1r,
                a2r * b1r - a2i * b1i + b2r, a2r * b1i + a2i * b1r + b2i)

    _, _, x_re, x_im = lax.associative_scan(combine, (lam_re, lam_im, bu_re, bu_im), axis=1)
    y = (jnp.einsum('bsgp,ghp->bsgh', x_re, c_re) - jnp.einsum('bsgp,ghp->bsgh', x_im, c_im)
         + d_skip * ug)
    z = jax.nn.gelu(y.reshape(b, s, S5_WIDTH).astype(u.dtype), approximate=False)
    return z * jax.nn.sigmoid(z @ w_glu + b_glu)


def odd_mixer(xn, w_in, pool_w, pool_scale, a_re, a_im, log_dt, b_re, b_im, c_re, c_im, d_skip, w_glu, b_glu, w_out):
    u_c, u_d = jnp.split(xn @ w_in, [POOL_WIDTH], axis=-1)
    y_c = pool_mixer(u_c, pool_w, pool_scale)
    y_d = s5_mixer(u_d, a_re, a_im, log_dt, b_re, b_im, c_re, c_im, d_skip, w_glu, b_glu)
    return jnp.concatenate([y_c, y_d], axis=-1) @ w_out


def peer(xn, w_q, sub_keys, u_tab, v_tab):
    b, s, d = xn.shape
    t = b * s
    xf = xn.reshape(t, d)
    q = (xf @ w_q).reshape(t, PEER_HEADS, 2, PEER_DKEY // 2)
    scores = jnp.einsum('thpc,hpkc->thpk', q, sub_keys).astype(jnp.float32)
    s_top, i_top = lax.top_k(scores, PEER_TOPK)
    cand = (s_top[:, :, 0, :, None] + s_top[:, :, 1, None, :]).reshape(t, PEER_HEADS, PEER_TOPK * PEER_TOPK)
    s_best, flat = lax.top_k(cand, PEER_TOPK)
    e_a = jnp.take_along_axis(i_top[:, :, 0], flat // PEER_TOPK, axis=-1)
    e_b = jnp.take_along_axis(i_top[:, :, 1], flat % PEER_TOPK, axis=-1)
    n_blk = t // PEER_BLOCK
    experts = (e_a * PEER_NKEYS + e_b).reshape(n_blk, PEER_BLOCK, PEER_HEADS * PEER_TOPK)
    gates = jax.nn.softmax(s_best, axis=-1).astype(xn.dtype).reshape(n_blk, PEER_BLOCK, PEER_HEADS * PEER_TOPK)

    def expert_block(args):
        x_blk, e_blk, g_blk = args
        u_sel = jnp.take(u_tab, e_blk, axis=0)
        hid = jax.nn.gelu(jnp.einsum('tkd,td->tk', u_sel, x_blk), approximate=False)
        return jnp.einsum('tk,tkd->td', g_blk * hid, jnp.take(v_tab, e_blk, axis=0))

    out = lax.map(expert_block, (xf.reshape(n_blk, PEER_BLOCK, d), experts, gates))
    return out.reshape(b, s, d)


def setup_inputs(seed: int = 0) -> dict:
    key = jax.random.key(seed)
    ks = iter(jax.random.split(key, 48))
    f32 = jnp.float32

    def nrm(shape, scale):
        return jax.random.normal(next(ks), shape, f32) * scale

    def gain(shape):
        return 1.0 + nrm(shape, 0.02)

    x = nrm((BATCH, SEQ, D_MODEL), 1.0)
    p = nrm((DEPTH, BATCH, SEQ, PLE_DIM), 1.0)
    positions = jnp.arange(SEQ, dtype=jnp.int32)
    norm_mix_w = gain((DEPTH, D_MODEL))
    norm_ffn_w = gain((DEPTH, D_MODEL))
    norm_ple_w = gain((DEPTH, D_MODEL))
    final_norm_w = gain((D_MODEL,))
    ev_w_in = nrm((N_EVEN, D_MODEL, EVEN_PROJ), D_MODEL ** -0.5)
    ev_gla_w_up = nrm((N_EVEN, GLA_GATE_RANK, GLA_HEADS * GLA_DK), GLA_GATE_RANK ** -0.5)
    ev_gla_b_up = nrm((N_EVEN, GLA_HEADS * GLA_DK), 0.1)
    ev_gla_norm_w = gain((N_EVEN, GLA_DV))
    ev_ret_norm_w = gain((N_EVEN, RET_HEADS * RET_DV))
    ev_w_out = nrm((N_EVEN, MIX_WIDTH, D_MODEL), MIX_WIDTH ** -0.5)
    od_w_in = nrm((N_ODD, D_MODEL, MIX_WIDTH), D_MODEL ** -0.5)
    od_pool_w = nrm((N_ODD, POOL_GROUPS, POOL_GROUP_WIDTH, POOL_GROUP_WIDTH), POOL_GROUP_WIDTH ** -0.5)
    od_pool_scale = 1.0 + nrm((N_ODD, POOL_WIDTH), 0.1)
    od_s5_a_re = -0.5 + nrm((N_ODD, S5_GROUPS, S5_P), 0.01)
    od_s5_a_im = math.pi * jnp.arange(S5_P, dtype=f32) + nrm((N_ODD, S5_GROUPS, S5_P), 0.01)
    od_s5_log_dt = jax.random.uniform(next(ks), (N_ODD, S5_GROUPS), f32, math.log(DT_MIN), math.log(DT_MAX))
    od_s5_b_re = nrm((N_ODD, S5_GROUPS, S5_P, S5_H), (2.0 * S5_H) ** -0.5)
    od_s5_b_im = nrm((N_ODD, S5_GROUPS, S5_P, S5_H), (2.0 * S5_H) ** -0.5)
    od_s5_c_re = nrm((N_ODD, S5_GROUPS, S5_H, S5_P), (2.0 * S5_P) ** -0.5)
    od_s5_c_im = nrm((N_ODD, S5_GROUPS, S5_H, S5_P), (2.0 * S5_P) ** -0.5)
    od_s5_d = nrm((N_ODD, S5_GROUPS, S5_H), 1.0)
    od_s5_w_glu = nrm((N_ODD, S5_WIDTH, S5_WIDTH), S5_WIDTH ** -0.5)
    od_s5_b_glu = nrm((N_ODD, S5_WIDTH), 0.02)
    od_w_out = nrm((N_ODD, MIX_WIDTH, D_MODEL), MIX_WIDTH ** -0.5)
    peer_w_q = nrm((DEPTH, D_MODEL, PEER_HEADS * PEER_DKEY), D_MODEL ** -0.5)
    peer_sub_keys = nrm((DEPTH, PEER_HEADS, 2, PEER_NKEYS, PEER_DKEY // 2), (PEER_DKEY // 2) ** -0.5)
    peer_u = nrm((DEPTH, PEER_EXPERTS, D_MODEL), D_MODEL ** -0.5)
    peer_v = nrm((DEPTH, PEER_EXPERTS, D_MODEL), PEER_HEADS ** -0.5)
    ple_w_proj = nrm((DEPTH, PLE_DIM, D_MODEL), PLE_DIM ** -0.5)
    ple_w_gate = nrm((DEPTH, D_MODEL, D_MODEL), D_MODEL ** -0.5)
    return {'x': x, 'p': p, 'positions': positions,
            'norm_mix_w': norm_mix_w, 'norm_ffn_w': norm_ffn_w, 'norm_ple_w': norm_ple_w,
            'final_norm_w': final_norm_w,
            'ev_w_in': ev_w_in, 'ev_gla_w_up': ev_gla_w_up, 'ev_gla_b_up': ev_gla_b_up,
            'ev_gla_norm_w': ev_gla_norm_w, 'ev_ret_norm_w': ev_ret_norm_w, 'ev_w_out': ev_w_out,
            'od_w_in': od_w_in, 'od_pool_w': od_pool_w, 'od_pool_scale': od_pool_scale,
            'od_s5_a_re': od_s5_a_re, 'od_s5_a_im': od_s5_a_im, 'od_s5_log_dt': od_s5_log_dt,
            'od_s5_b_re': od_s5_b_re, 'od_s5_b_im': od_s5_b_im,
            'od_s5_c_re': od_s5_c_re, 'od_s5_c_im': od_s5_c_im, 'od_s5_d': od_s5_d,
            'od_s5_w_glu': od_s5_w_glu, 'od_s5_b_glu': od_s5_b_glu, 'od_w_out': od_w_out,
            'peer_w_q': peer_w_q, 'peer_sub_keys': peer_sub_keys, 'peer_u': peer_u, 'peer_v': peer_v,
            'ple_w_proj': ple_w_proj, 'ple_w_gate': ple_w_gate}


def reference(x, p, positions, norm_mix_w, norm_ffn_w, norm_ple_w, final_norm_w,
              ev_w_in, ev_gla_w_up, ev_gla_b_up, ev_gla_norm_w, ev_ret_norm_w, ev_w_out,
              od_w_in, od_pool_w, od_pool_scale, od_s5_a_re, od_s5_a_im, od_s5_log_dt,
              od_s5_b_re, od_s5_b_im, od_s5_c_re, od_s5_c_im, od_s5_d, od_s5_w_glu, od_s5_b_glu, od_w_out,
              peer_w_q, peer_sub_keys, peer_u, peer_v, ple_w_proj, ple_w_gate):
    log_gamma = jnp.log(1.0 - 2.0 ** (-5.0 - jnp.arange(RET_HEADS, dtype=jnp.float32)))
    h = x
    for i in range(DEPTH):
        j = i // 2
        xn = rms_norm(h, norm_mix_w[i])
        if i % 2 == 0:
            mix = even_mixer(xn, positions, log_gamma, ev_w_in[j], ev_gla_w_up[j], ev_gla_b_up[j],
                             ev_gla_norm_w[j], ev_ret_norm_w[j], ev_w_out[j])
        else:
            mix = odd_mixer(xn, od_w_in[j], od_pool_w[j], od_pool_scale[j], od_s5_a_re[j], od_s5_a_im[j],
                            od_s5_log_dt[j], od_s5_b_re[j], od_s5_b_im[j], od_s5_c_re[j], od_s5_c_im[j],
                            od_s5_d[j], od_s5_w_glu[j], od_s5_b_glu[j], od_w_out[j])
        h = h + mix
        h = h + peer(rms_norm(h, norm_ffn_w[i]), peer_w_q[i], peer_sub_keys[i], peer_u[i], peer_v[i])
        gate = jax.nn.sigmoid(rms_norm(h, norm_ple_w[i]) @ ple_w_gate[i])
        h = h + (p[i] @ ple_w_proj[i]) * gate
    return rms_norm(h, final_norm_w)
```

```python
import functools
import math

import numpy as np
import jax
import jax.numpy as jnp
from jax import lax
from jax.experimental import pallas as pl
from jax.experimental.pallas import tpu as pltpu

F32 = jnp.float32
BF16 = jnp.bfloat16

D_MODEL = 1024
NORM_EPS = 1e-6
CHUNK = 64
GLA_HEADS = 4
GLA_DK = 64
GLA_DV = 128
GLA_GATE_RANK = 16
GLA_GATE_NORM = 16.0
RET_HEADS = 4
RET_DK = 64
RET_DV = 128
ROPE_BASE = 10000.0
POOL_WINDOWS = (2, 4, 8, 16)
POOL_WIDTH = 512
POOL_GROUP_WIDTH = 128
S5_H = 16
S5_P = 64
S5_GROUPS = 32
S5_WIDTH = 512
S5_STATE = S5_GROUPS * S5_P
PEER_HEADS = 8
PEER_NKEYS = 128
PEER_TOPK = 16
PEER_HK = PEER_HEADS * PEER_TOPK
PEER_EXPERTS = PEER_NKEYS * PEER_NKEYS
EVEN_COLS = 3200

VMEM_LIMIT_BIG = 56 * 1024 * 1024
VMEM_LIMIT = 48 * 1024 * 1024
SUBLANES = 8

NT_DIMS = (((1,), (1,)), ((), ()))


def _split_bf16(x):
    hi = x.astype(BF16)
    lo = (x - hi.astype(F32)).astype(BF16)
    return hi, lo


def _rms(x, w):
    return x * lax.rsqrt(jnp.mean(x * x, axis=-1, keepdims=True) + NORM_EPS) * w


def _sigmoid(x):
    return 1.0 / (1.0 + jnp.exp(-x))


def _gelu(x):
    return 0.5 * x * (1.0 + lax.erf(x * (2.0 ** -0.5)))


def _log_sigmoid(z):
    return jnp.minimum(z, 0.0) - jnp.log1p(jnp.exp(-jnp.abs(z)))


def _norm_mm_kernel(h_ref, nw_ref, w_ref, o_ref):
    xn = _rms(h_ref[...], nw_ref[...])
    o_ref[...] = jnp.dot(xn.astype(BF16), w_ref[...], preferred_element_type=F32)


def norm_matmul(h, nw, w, tm):
    t, d = h.shape
    n = w.shape[1]
    return pl.pallas_call(
        _norm_mm_kernel,
        grid=(t // tm,),
        in_specs=[
            pl.BlockSpec((tm, d), lambda i: (i, 0)),
            pl.BlockSpec((1, d), lambda i: (0, 0)),
            pl.BlockSpec((d, n), lambda i: (0, 0)),
        ],
        out_specs=pl.BlockSpec((tm, n), lambda i: (i, 0)),
        out_shape=jax.ShapeDtypeStruct((t, n), F32),
        compiler_params=pltpu.CompilerParams(dimension_semantics=("parallel",), vmem_limit_bytes=VMEM_LIMIT),
        name="norm_matmul",
    )(h, nw, w)


def _even_mixer_kernel(h_ref, qk_ref, va_ref, ga_ref, vb_ref, gb_ref, ra_ref, pos_ref,
                       freq_ref, wup_ref, bup_ref, gnw_ref, rnw_ref, wout_ref, tril_ref,
                       dmask_ref, kdec_ref, qdec_ref, cdec_ref,
                       o_ref, gstate, rstate, y_scr):
    tm = h_ref.shape[0]

    @pl.when(pl.program_id(1) == 0)
    def _():
        gstate[...] = jnp.zeros_like(gstate)
        rstate[...] = jnp.zeros_like(rstate)

    lane = lax.broadcasted_iota(jnp.int32, (1, 256), 1)
    gla_masks = [((lane >= GLA_DK * h) & (lane < GLA_DK * (h + 1))).astype(F32) for h in range(GLA_HEADS)]
    half = RET_DK // 2
    ret_masks = [(((lane >= half * h) & (lane < half * (h + 1)))
                  | ((lane >= 128 + half * h) & (lane < 128 + half * (h + 1)))).astype(F32)
                 for h in range(RET_HEADS)]
    ri = lax.broadcasted_iota(jnp.int32, (CHUNK, CHUNK), 0)
    ci = lax.broadcasted_iota(jnp.int32, (CHUNK, CHUNK), 1)
    causal = ri >= ci
    tril = tril_ref[...]
    freqs = freq_ref[...]
    wup = wup_ref[...]
    bup = bup_ref[...]

    def chunk_body(c, carry):
        rows = pl.ds(pl.multiple_of(c * CHUNK, CHUNK), CHUNK)
        qk = qk_ref[rows, :]
        qa, ka, qb, kb = qk[:, 0:256], qk[:, 256:512], qk[:, 512:768], qk[:, 768:1024]

        z = jnp.dot(ra_ref[rows, :].astype(BF16), wup, preferred_element_type=F32) + bup
        la = _log_sigmoid(z) * (1.0 / GLA_GATE_NORM)
        lah, lal = _split_bf16(la)
        g = (jnp.dot(tril, lah, preferred_element_type=F32)
             + jnp.dot(tril, lal, preferred_element_type=F32))
        g_last = g[CHUNK - 1:CHUNK, :]
        ref = 0.5 * g_last
        qs = qa * (GLA_DK ** -0.5)
        qt = qs * jnp.exp(g - ref)
        kt = (ka * jnp.exp(ref - g)).astype(BF16)
        kd = ka * jnp.exp(g_last - g)
        qg = qs * jnp.exp(g)
        decay = jnp.exp(g_last)
        for h in range(GLA_HEADS):
            m = gla_masks[h]
            s = lax.dot_general((qt * m).astype(BF16), kt, NT_DIMS, preferred_element_type=F32)
            s = jnp.where(causal, s, 0.0)
            v = va_ref[rows, GLA_DV * h:GLA_DV * (h + 1)]
            st = gstate[h]
            o = jnp.dot(s.astype(BF16), v.astype(BF16), preferred_element_type=F32)
            o = o + lax.dot_general((qg * m).astype(BF16), st.astype(BF16), NT_DIMS, preferred_element_type=F32)
            gstate[h] = st * decay + jnp.dot(v.T.astype(BF16), (kd * m).astype(BF16), preferred_element_type=F32)
            y_scr[rows, GLA_DV * h:GLA_DV * (h + 1)] = o

        ang = pos_ref[rows, :] * freqs
        cs, sn = jnp.cos(ang), jnp.sin(ang)
        q1, q2 = qb[:, 0:128], qb[:, 128:256]
        k1, k2 = kb[:, 0:128], kb[:, 128:256]
        qr = jnp.concatenate([q1 * cs - q2 * sn, q2 * cs + q1 * sn], axis=1)
        kr = jnp.concatenate([k1 * cs - k2 * sn, k2 * cs + k1 * sn], axis=1) * (RET_DK ** -0.5)
        krb = kr.astype(BF16)
        for h in range(RET_HEADS):
            m = ret_masks[h]
            qm = (qr * m).astype(BF16)
            s = lax.dot_general(qm, krb, NT_DIMS, preferred_element_type=F32) * dmask_ref[h]
            v = vb_ref[rows, RET_DV * h:RET_DV * (h + 1)]
            st = rstate[h]
            o = jnp.dot(s.astype(BF16), v.astype(BF16), preferred_element_type=F32)
            o = o + lax.dot_general(qm, st.astype(BF16), NT_DIMS, preferred_element_type=F32) * qdec_ref[h]
            rstate[h] = st * cdec_ref[h] + jnp.dot((v * kdec_ref[h]).T.astype(BF16), (kr * m).astype(BF16),
                                                    preferred_element_type=F32)
            y_scr[rows, 512 + RET_DV * h:512 + RET_DV * (h + 1)] = o
        return carry

    lax.fori_loop(0, tm // CHUNK, chunk_body, 0)

    pieces = []
    gnw = gnw_ref[...]
    for h in range(GLA_HEADS):
        o = y_scr[:, GLA_DV * h:GLA_DV * (h + 1)]
        gt = ga_ref[:, GLA_DV * h:GLA_DV * (h + 1)]
        pieces.append(_rms(o, gnw) * (gt * _sigmoid(gt)))
    for h in range(RET_HEADS):
        o = y_scr[:, 512 + RET_DV * h:512 + RET_DV * (h + 1)]
        gt = gb_ref[:, RET_DV * h:RET_DV * (h + 1)]
        mu = jnp.mean(o, axis=-1, keepdims=True)
        oc = o - mu
        var = jnp.mean(oc * oc, axis=-1, keepdims=True)
        nrm = oc * lax.rsqrt(var + NORM_EPS) * rnw_ref[:, RET_DV * h:RET_DV * (h + 1)]
        pieces.append(nrm * (gt * _sigmoid(gt)))
    y = jnp.concatenate(pieces, axis=1).astype(BF16)
    o_ref[...] = h_ref[...] + jnp.dot(y, wout_ref[...], preferred_element_type=F32)


def even_mixer(h, proj, pos, consts, bsz, seq, tm):
    nj = seq // tm
    tok = lambda b, j: (b * nj + j, 0)
    col = lambda cb: (lambda b, j: (b * nj + j, cb))
    full2 = lambda b, j: (0, 0)
    full3 = lambda b, j: (0, 0, 0)
    (freqs, wup, bup, gnw, rnw, wout, tril, dmask, kdec, qdec, cdec) = consts
    return pl.pallas_call(
        _even_mixer_kernel,
        grid=(bsz, nj),
        in_specs=[
            pl.BlockSpec((tm, D_MODEL), tok),
            pl.BlockSpec((tm, 1024), col(0)),
            pl.BlockSpec((tm, 512), col(2)),
            pl.BlockSpec((tm, 512), col(3)),
            pl.BlockSpec((tm, 512), col(4)),
            pl.BlockSpec((tm, 512), col(5)),
            pl.BlockSpec((tm, 128), col(24)),
            pl.BlockSpec((tm, 1), lambda b, j: (j, 0)),
            pl.BlockSpec((1, 128), full2),
            pl.BlockSpec((128, 256), full2),
            pl.BlockSpec((1, 256), full2),
            pl.BlockSpec((1, 128), full2),
            pl.BlockSpec((1, 512), full2),
            pl.BlockSpec((1024, D_MODEL), full2),
            pl.BlockSpec((CHUNK, CHUNK), full2),
            pl.BlockSpec((RET_HEADS, CHUNK, CHUNK), full3),
            pl.BlockSpec((RET_HEADS, CHUNK, 128), full3),
            pl.BlockSpec((RET_HEADS, CHUNK, 128), full3),
            pl.BlockSpec((RET_HEADS, 1, 256), full3),
        ],
        out_specs=pl.BlockSpec((tm, D_MODEL), tok),
        out_shape=jax.ShapeDtypeStruct(h.shape, F32),
        scratch_shapes=[
            pltpu.VMEM((GLA_HEADS, GLA_DV, 256), F32),
            pltpu.VMEM((RET_HEADS, RET_DV, 256), F32),
            pltpu.VMEM((tm, 1024), F32),
        ],
        compiler_params=pltpu.CompilerParams(dimension_semantics=("arbitrary", "arbitrary"),
                                             vmem_limit_bytes=VMEM_LIMIT),
        name="even_mixer",
    )(h, proj, proj, proj, proj, proj, proj, pos, freqs, wup, bup, gnw, rnw, wout, tril, dmask, kdec, qdec, cdec)


def _s5_param_kernel(are_ref, aim_ref, ldt_ref, bre_ref, bim_ref,
                     pw_re_ref, pw_im_ref, bbre_ref, bbim_ref):
    a_re = are_ref[...]
    a_im = aim_ref[...]
    dt = jnp.exp(ldt_ref[...])
    for r in range(SUBLANES):
        mag = jnp.exp(a_re * dt * (r + 1.0))
        pw_re_ref[r] = mag * jnp.cos(a_im * dt * (r + 1.0))
        pw_im_ref[r] = mag * jnp.sin(a_im * dt * (r + 1.0))
    mag = jnp.exp(a_re * dt)
    abar_re, abar_im = mag * jnp.cos(a_im * dt), mag * jnp.sin(a_im * dt)
    den = a_re * a_re + a_im * a_im
    nr, ni = abar_re - 1.0, abar_im
    coef_re = (nr * a_re + ni * a_im) / den
    coef_im = (ni * a_re - nr * a_im) / den
    b_re = bre_ref[...]
    b_im = bim_ref[...]
    c_re = jnp.concatenate([coef_re] * S5_H, axis=1)
    c_im = jnp.concatenate([coef_im] * S5_H, axis=1)
    bbre_ref[...] = c_re * b_re - c_im * b_im
    bbim_ref[...] = c_re * b_im + c_im * b_re


def s5_params(a_re, a_im, log_dt, b_re_t, b_im_t):
    g, p = a_re.shape
    hh = b_re_t.shape[1]
    pw_re, pw_im, bb_re, bb_im = pl.pallas_call(
        _s5_param_kernel,
        out_shape=[jax.ShapeDtypeStruct((SUBLANES, g, p), F32), jax.ShapeDtypeStruct((SUBLANES, g, p), F32),
                   jax.ShapeDtypeStruct((g, hh * p), F32), jax.ShapeDtypeStruct((g, hh * p), F32)],
        name="s5_params",
    )(a_re, a_im, log_dt, b_re_t.reshape(g, hh * p), b_im_t.reshape(g, hh * p))
    return pw_re, pw_im, bb_re.reshape(g, hh, p), bb_im.reshape(g, hh, p)


def _odd_mixer_kernel(h_ref, u_ref, poolw_ref, pscale_ref, wbre_ref, wbim_ref, wcre_ref, wcim_ref,
                      pwre_ref, pwim_ref, dskip_ref, wglu_ref, bglu_ref, wout_ref,
                      o_ref, tail_scr, car_re, car_im, xre_scr, xim_scr):
    tm = h_ref.shape[0]
    j = pl.program_id(1)
    halo = POOL_WINDOWS[-1]

    @pl.when(j == 0)
    def _():
        tail_scr[...] = jnp.zeros_like(tail_scr)
        car_re[...] = jnp.zeros_like(car_re)
        car_im[...] = jnp.zeros_like(car_im)

    uc = u_ref[:, 0:POOL_WIDTH]
    ud = u_ref[:, POOL_WIDTH:POOL_WIDTH + S5_WIDTH]

    ext = jnp.concatenate([tail_scr[...], uc], axis=0)
    tail_scr[...] = uc[tm - halo:tm, :]
    pos = (j * tm + lax.broadcasted_iota(jnp.int32, (tm, 1), 0)).astype(F32)
    mixed = []
    for gi, win in enumerate(POOL_WINDOWS):
        a = ext[:, POOL_GROUP_WIDTH * gi:POOL_GROUP_WIDTH * (gi + 1)]
        n = tm + halo
        step = 1
        end = 0
        while step < win:
            a = a[step:n, :] + a[0:n - step, :]
            n -= step
            end += step
            step *= 2
        wsum = a[halo - end:halo - end + tm, :]
        cnt = jnp.minimum(pos + 1.0, float(win))
        pooled = wsum / cnt - uc[:, POOL_GROUP_WIDTH * gi:POOL_GROUP_WIDTH * (gi + 1)]
        mixed.append(jnp.dot(pooled.astype(BF16), poolw_ref[gi], preferred_element_type=F32))
    y_c = jnp.concatenate(mixed, axis=1) * pscale_ref[...]

    udb = ud.astype(BF16)
    xre_scr[...] = jnp.dot(udb, wbre_ref[...], preferred_element_type=F32)
    xim_scr[...] = jnp.dot(udb, wbim_ref[...], preferred_element_type=F32)
    rowi = lax.broadcasted_iota(jnp.int32, (SUBLANES, S5_STATE), 0)
    pw_re = pwre_ref[...]
    pw_im = pwim_ref[...]

    def slab(s, carry):
        cr, ci = carry
        rows = pl.ds(pl.multiple_of(s * SUBLANES, SUBLANES), SUBLANES)
        xr = xre_scr[rows, :]
        xi = xim_scr[rows, :]
        for dsh in (1, 2, 4):
            pr = pw_re[dsh - 1:dsh, :]
            pi = pw_im[dsh - 1:dsh, :]
            keep = rowi >= dsh
            sr = jnp.where(keep, pltpu.roll(xr, dsh, axis=0), 0.0)
            si = jnp.where(keep, pltpu.roll(xi, dsh, axis=0), 0.0)
            xr, xi = xr + (pr * sr - pi * si), xi + (pr * si + pi * sr)
        xr, xi = xr + (pw_re * cr - pw_im * ci), xi + (pw_re * ci + pw_im * cr)
        xre_scr[rows, :] = xr
        xim_scr[rows, :] = xi
        return xr[SUBLANES - 1:SUBLANES, :], xi[SUBLANES - 1:SUBLANES, :]

    cr, ci = lax.fori_loop(0, tm // SUBLANES, slab, (car_re[...], car_im[...]))
    car_re[...] = cr
    car_im[...] = ci

    y = (jnp.dot(xre_scr[...].astype(BF16), wcre_ref[...], preferred_element_type=F32)
         - jnp.dot(xim_scr[...].astype(BF16), wcim_ref[...], preferred_element_type=F32)
         + dskip_ref[...] * ud)
    z = _gelu(y)
    y_d = z * _sigmoid(jnp.dot(z.astype(BF16), wglu_ref[...], preferred_element_type=F32) + bglu_ref[...])

    ycat = jnp.concatenate([y_c, y_d], axis=1).astype(BF16)
    o_ref[...] = h_ref[...] + jnp.dot(ycat, wout_ref[...], preferred_element_type=F32)


def odd_mixer(h, u, consts, bsz, seq, tm):
    nj = seq // tm
    tok = lambda b, j: (b * nj + j, 0)
    full2 = lambda b, j: (0, 0)
    full3 = lambda b, j: (0, 0, 0)
    (poolw, pscale, wbre, wbim, wcre, wcim, pwre, pwim, dskip, wglu, bglu, wout) = consts
    return pl.pallas_call(
        _odd_mixer_kernel,
        grid=(bsz, nj),
        in_specs=[
            pl.BlockSpec((tm, D_MODEL), tok),
            pl.BlockSpec((tm, 1024), tok),
            pl.BlockSpec((4, 128, 128), full3),
            pl.BlockSpec((1, POOL_WIDTH), full2),
            pl.BlockSpec((S5_WIDTH, S5_STATE), full2),
            pl.BlockSpec((S5_WIDTH, S5_STATE), full2),
            pl.BlockSpec((S5_STATE, S5_WIDTH), full2),
            pl.BlockSpec((S5_STATE, S5_WIDTH), full2),
            pl.BlockSpec((SUBLANES, S5_STATE), full2),
            pl.BlockSpec((SUBLANES, S5_STATE), full2),
            pl.BlockSpec((1, S5_WIDTH), full2),
            pl.BlockSpec((S5_WIDTH, S5_WIDTH), full2),
            pl.BlockSpec((1, S5_WIDTH), full2),
            pl.BlockSpec((1024, D_MODEL), full2),
        ],
        out_specs=pl.BlockSpec((tm, D_MODEL), tok),
        out_shape=jax.ShapeDtypeStruct(h.shape, F32),
        scratch_shapes=[
            pltpu.VMEM((POOL_WINDOWS[-1], POOL_WIDTH), F32),
            pltpu.VMEM((1, S5_STATE), F32),
            pltpu.VMEM((1, S5_STATE), F32),
            pltpu.VMEM((tm, S5_STATE), F32),
            pltpu.VMEM((tm, S5_STATE), F32),
        ],
        compiler_params=pltpu.CompilerParams(dimension_semantics=("arbitrary", "arbitrary"),
                                             vmem_limit_bytes=VMEM_LIMIT),
        name="odd_mixer",
    )(h, u, poolw, pscale, wbre, wbim, wcre, wcim, pwre, pwim, dskip, wglu, bglu, wout)


def _top16_rows(s, n_rows):
    iota = lax.broadcasted_iota(jnp.int32, s.shape, 0).astype(F32)
    vals, idxs = [], []
    for _ in range(PEER_TOPK):
        m = jnp.max(s, axis=0, keepdims=True)
        am = jnp.min(jnp.where(s == m, iota, float(n_rows)), axis=0, keepdims=True)
        vals.append(m)
        idxs.append(am)
        s = jnp.where(iota == am, -jnp.inf, s)
    return jnp.concatenate(vals, axis=0), jnp.concatenate(idxs, axis=0)


def _take16(table, sel):
    out = jnp.zeros(sel.shape, table.dtype)
    for i in range(PEER_TOPK):
        out = jnp.where(sel == i, table[i:i + 1, :], out)
    return out


def _dot3(ah, al, bh, bl, dims):
    return (lax.dot_general(ah, bh, dims, preferred_element_type=F32)
            + lax.dot_general(al, bh, dims, preferred_element_type=F32)
            + lax.dot_general(ah, bl, dims, preferred_element_type=F32))


def _peer_route_kernel(h_ref, nw_ref, wqh_ref, wql_ref, kh_ref, kl_ref, xn_ref, exp_ref, gate_ref,
                       qt_scr, et_scr, gt_scr):
    xn = _rms(h_ref[...], nw_ref[...])
    for r in range(SUBLANES):
        xn_ref[:, r, :] = xn[:, 128 * r:128 * (r + 1)]
    xh, xl = _split_bf16(xn)
    qt_scr[...] = _dot3(wqh_ref[...], wql_ref[...], xh, xl, NT_DIMS)
    mm = (((1,), (0,)), ((), ()))

    def head(hd, carry):
        ra = pl.ds(pl.multiple_of(hd * 256, 256), 128)
        rb = pl.ds(pl.multiple_of(hd * 256 + 128, 128), 128)
        qah, qal = _split_bf16(qt_scr[ra, :])
        qbh, qbl = _split_bf16(qt_scr[rb, :])
        sa = _dot3(kh_ref[2 * hd], kl_ref[2 * hd], qah, qal, mm)
        sb = _dot3(kh_ref[2 * hd + 1], kl_ref[2 * hd + 1], qbh, qbl, mm)
        av, ai = _top16_rows(sa, PEER_NKEYS)
        bv, bi = _top16_rows(sb, PEER_NKEYS)
        cand = jnp.concatenate([av[i:i + 1, :] + bv for i in range(PEER_TOPK)], axis=0)
        cv, flat = _top16_rows(cand, PEER_TOPK * PEER_TOPK)
        flat = flat.astype(jnp.int32)
        e_a = _take16(ai, flat >> 4)
        e_b = _take16(bi, flat & (PEER_TOPK - 1))
        ex = jnp.exp(cv - cv[0:1, :])
        rows = pl.ds(pl.multiple_of(hd * PEER_TOPK, PEER_TOPK), PEER_TOPK)
        et_scr[rows, :] = e_a * float(PEER_NKEYS) + e_b
        gt_scr[rows, :] = ex / jnp.sum(ex, axis=0, keepdims=True)
        return carry

    lax.fori_loop(0, PEER_HEADS, head, 0)
    exp_ref[...] = et_scr[...].T.astype(jnp.int32)
    gate_ref[...] = gt_scr[...].T


def peer_route(h, nw, wqt, keys, tm):
    t, d = h.shape
    nq = wqt.shape[0]
    wqh, wql = _split_bf16(wqt)
    kh, kl = _split_bf16(keys)
    return pl.pallas_call(
        _peer_route_kernel,
        grid=(t // tm,),
        in_specs=[
            pl.BlockSpec((tm, d), lambda i: (i, 0)),
            pl.BlockSpec((1, d), lambda i: (0, 0)),
            pl.BlockSpec((nq, d), lambda i: (0, 0)),
            pl.BlockSpec((nq, d), lambda i: (0, 0)),
            pl.BlockSpec((2 * PEER_HEADS, PEER_NKEYS, 128), lambda i: (0, 0, 0)),
            pl.BlockSpec((2 * PEER_HEADS, PEER_NKEYS, 128), lambda i: (0, 0, 0)),
        ],
        out_specs=[
            pl.BlockSpec((tm, SUBLANES, d // SUBLANES), lambda i: (i, 0, 0)),
            pl.BlockSpec((tm, PEER_HK), lambda i: (i, 0)),
            pl.BlockSpec((tm, PEER_HK), lambda i: (i, 0)),
        ],
        out_shape=[jax.ShapeDtypeStruct((t, SUBLANES, d // SUBLANES), F32),
                   jax.ShapeDtypeStruct((t, PEER_HK), jnp.int32), jax.ShapeDtypeStruct((t, PEER_HK), F32)],
        scratch_shapes=[pltpu.VMEM((nq, tm), F32), pltpu.VMEM((PEER_HK, tm), F32), pltpu.VMEM((PEER_HK, tm), F32)],
        compiler_params=pltpu.CompilerParams(dimension_semantics=("parallel",), vmem_limit_bytes=VMEM_LIMIT),
        name="peer_route",
    )(h, nw, wqh, wql, kh, kl)


PEER_TOK_UNROLL = 4


def _gather_rows(idx_ref, t, tab_ref, stage_ref):
    for k in range(PEER_HK):
        stage_ref[pl.ds(4 * k, 4), :] = tab_ref[idx_ref[t, k]]


def _diag_mask():
    row = lax.broadcasted_iota(jnp.int32, (SUBLANES, SUBLANES * PEER_HK), 0)
    lane = lax.broadcasted_iota(jnp.int32, (SUBLANES, SUBLANES * PEER_HK), 1)
    return (lane & (SUBLANES - 1)) == row


def _peer_hidden_kernel(idx_ref, x_ref, g_ref, sel_ref, tab_ref, o_ref, *scr):
    stages, part_scr = scr[:PEER_TOK_UNROLL], scr[PEER_TOK_UNROLL]
    tb = x_ref.shape[0]
    diag = _diag_mask()

    def one_token(t, stage):
        _gather_rows(idx_ref, t, tab_ref, stage)
        u = pltpu.bitcast(stage[...], BF16)
        xh, xl = _split_bf16(x_ref[t])
        x16 = jnp.concatenate([xh, xl], axis=0)
        out = lax.dot_general(x16, u, NT_DIMS, preferred_element_type=F32)
        o8 = out[0:SUBLANES] + out[SUBLANES:2 * SUBLANES]
        part_scr[pl.ds(t, 1), :] = jnp.sum(jnp.where(diag, o8, 0.0), axis=0, keepdims=True)

    def body(i, carry):
        for j in range(PEER_TOK_UNROLL):
            one_token(PEER_TOK_UNROLL * i + j, stages[j])
        return carry

    lax.fori_loop(0, tb // PEER_TOK_UNROLL, body, 0)
    ph, plo = _split_bf16(part_scr[...])
    sel = sel_ref[...]
    hid = jnp.dot(ph, sel, preferred_element_type=F32) + jnp.dot(plo, sel, preferred_element_type=F32)
    o_ref[...] = g_ref[...] * _gelu(hid)


def peer_hidden(idx, x8, gates, sel, tab, tb):
    t = idx.shape[0]
    return pl.pallas_call(
        _peer_hidden_kernel,
        grid=(t // tb,),
        in_specs=[
            pl.BlockSpec((tb, PEER_HK), lambda i: (i, 0), memory_space=pltpu.SMEM),
            pl.BlockSpec((tb, SUBLANES, 128), lambda i: (i, 0, 0)),
            pl.BlockSpec((tb, PEER_HK), lambda i: (i, 0)),
            pl.BlockSpec((SUBLANES * PEER_HK, PEER_HK), lambda i: (0, 0)),
            pl.BlockSpec((PEER_EXPERTS, 4, 128), lambda i: (0, 0, 0), pipeline_mode=pl.Buffered(1)),
        ],
        out_specs=pl.BlockSpec((tb, PEER_HK), lambda i: (i, 0)),
        out_shape=jax.ShapeDtypeStruct((t, PEER_HK), F32),
        scratch_shapes=[pltpu.VMEM((4 * PEER_HK, 128), jnp.int32) for _ in range(PEER_TOK_UNROLL)]
        + [pltpu.VMEM((tb, SUBLANES * PEER_HK), F32)],
        compiler_params=pltpu.CompilerParams(dimension_semantics=("arbitrary",), vmem_limit_bytes=VMEM_LIMIT_BIG),
        name="peer_hidden",
    )(idx, x8, gates, sel, tab)


def _peer_out_kernel(idx_ref, w_ref, exp_ref, tab_ref, o_ref, *scr):
    stages, wexp_scr = scr[:PEER_TOK_UNROLL], scr[PEER_TOK_UNROLL]
    tb = w_ref.shape[0]
    diag = _diag_mask()
    wh, wl = _split_bf16(w_ref[...])
    ex = exp_ref[...]
    wexp_scr[...] = jnp.dot(wh, ex, preferred_element_type=F32) + jnp.dot(wl, ex, preferred_element_type=F32)

    def one_token(t, stage):
        _gather_rows(idx_ref, t, tab_ref, stage)
        v = pltpu.bitcast(stage[...], BF16)
        w8 = jnp.where(diag, jnp.broadcast_to(wexp_scr[pl.ds(t, 1), :], (SUBLANES, SUBLANES * PEER_HK)), 0.0)
        w8h, w8l = _split_bf16(w8)
        w16 = jnp.concatenate([w8h, w8l], axis=0)
        out = jnp.dot(w16, v, preferred_element_type=F32)
        o_ref[t] = out[0:SUBLANES] + out[SUBLANES:2 * SUBLANES]

    def body(i, carry):
        for j in range(PEER_TOK_UNROLL):
            one_token(PEER_TOK_UNROLL * i + j, stages[j])
        return carry

    lax.fori_loop(0, tb // PEER_TOK_UNROLL, body, 0)


def peer_out(idx, w, expand, tab, tb):
    t = idx.shape[0]
    return pl.pallas_call(
        _peer_out_kernel,
        grid=(t // tb,),
        in_specs=[
            pl.BlockSpec((tb, PEER_HK), lambda i: (i, 0), memory_space=pltpu.SMEM),
            pl.BlockSpec((tb, PEER_HK), lambda i: (i, 0)),
            pl.BlockSpec((PEER_HK, SUBLANES * PEER_HK), lambda i: (0, 0)),
            pl.BlockSpec((PEER_EXPERTS, 4, 128), lambda i: (0, 0, 0), pipeline_mode=pl.Buffered(1)),
        ],
        out_specs=pl.BlockSpec((tb, SUBLANES, 128), lambda i: (i, 0, 0)),
        out_shape=jax.ShapeDtypeStruct((t, SUBLANES, 128), F32),
        scratch_shapes=[pltpu.VMEM((4 * PEER_HK, 128), jnp.int32) for _ in range(PEER_TOK_UNROLL)]
        + [pltpu.VMEM((tb, SUBLANES * PEER_HK), F32)],
        compiler_params=pltpu.CompilerParams(dimension_semantics=("arbitrary",), vmem_limit_bytes=VMEM_LIMIT_BIG),
        name="peer_out",
    )(idx, w, expand, tab)


def _ple_kernel(h_ref, e8_ref, p_ref, nw_ref, wg_ref, wp_ref, fw_ref, o_ref, *, final_norm):
    h = h_ref[...] + jnp.concatenate([e8_ref[:, r, :] for r in range(SUBLANES)], axis=1)
    gate = _sigmoid(jnp.dot(_rms(h, nw_ref[...]).astype(BF16), wg_ref[...], preferred_element_type=F32))
    out = h + jnp.dot(p_ref[...].astype(BF16), wp_ref[...], preferred_element_type=F32) * gate
    if final_norm:
        out = _rms(out, fw_ref[...])
    o_ref[...] = out


def ple(h, e8, p, nw, wg, wp, fw, tm, final_norm):
    t, d = h.shape
    pd = p.shape[1]
    return pl.pallas_call(
        functools.partial(_ple_kernel, final_norm=final_norm),
        grid=(t // tm,),
        in_specs=[
            pl.BlockSpec((tm, d), lambda i: (i, 0)),
            pl.BlockSpec((tm, SUBLANES, d // SUBLANES), lambda i: (i, 0, 0)),
            pl.BlockSpec((tm, pd), lambda i: (i, 0)),
            pl.BlockSpec((1, d), lambda i: (0, 0)),
            pl.BlockSpec((d, d), lambda i: (0, 0)),
            pl.BlockSpec((pd, d), lambda i: (0, 0)),
            pl.BlockSpec((1, d), lambda i: (0, 0)),
        ],
        out_specs=pl.BlockSpec((tm, d), lambda i: (i, 0)),
        out_shape=jax.ShapeDtypeStruct((t, d), F32),
        compiler_params=pltpu.CompilerParams(dimension_semantics=("parallel",), vmem_limit_bytes=VMEM_LIMIT),
        name="ple",
    )(h, e8, p, nw, wg, wp, fw)


def _pack_table(tab):
    e = tab.shape[0]
    b = lax.bitcast_convert_type(tab.astype(BF16).reshape(e, 4, 2, 128), jnp.uint16).astype(jnp.uint32)
    return lax.bitcast_convert_type(b[:, :, 0, :] | (b[:, :, 1, :] << 16), jnp.int32)


def _even_in_perm():
    off = np.cumsum([0, 256, 256, 512, 512, 16, 256, 256, 512, 512])
    qa, ka, va, ga, ra, qb, kb, vb, gb = [np.arange(off[i], off[i + 1]) for i in range(9)]
    half = RET_DK // 2
    rot = np.concatenate([np.concatenate([np.arange(h * RET_DK, h * RET_DK + half) for h in range(RET_HEADS)]),
                          np.concatenate([np.arange(h * RET_DK + half, (h + 1) * RET_DK) for h in range(RET_HEADS)])])
    return np.concatenate([qa, ka, qb[rot], kb[rot], va, ga, vb, gb, ra])


def _retention_tables():
    lg = np.log(1.0 - 2.0 ** (-5.0 - np.arange(RET_HEADS, dtype=np.float64)))
    idx = np.arange(CHUNK, dtype=np.float64)
    diff = idx[:, None] - idx[None, :]
    dmask = np.where(diff >= 0, np.exp(lg[:, None, None] * np.maximum(diff, 0.0)), 0.0)
    qdec = np.exp(lg[:, None] * (idx + 1.0))
    kdec = np.exp(lg[:, None] * (CHUNK - 1.0 - idx))
    cdec = np.exp(lg * CHUNK)
    return (jnp.asarray(dmask, F32),
            jnp.asarray(np.broadcast_to(kdec[:, :, None], (RET_HEADS, CHUNK, 128)), F32),
            jnp.asarray(np.broadcast_to(qdec[:, :, None], (RET_HEADS, CHUNK, 128)), F32),
            jnp.asarray(np.broadcast_to(cdec[:, None, None], (RET_HEADS, 1, 256)), F32))


def _peer_layer(h, nw, w_q, sub_keys, u_tab, v_tab, tm_route, tb):
    keys = sub_keys.reshape(2 * PEER_HEADS, PEER_NKEYS, -1)
    xn8, experts, gates = peer_route(h, nw, w_q.T, keys, tm_route)
    kk = np.arange(SUBLANES * PEER_HK) // SUBLANES
    sel = jnp.asarray(kk[:, None] == np.arange(PEER_HK)[None, :], BF16)
    w = peer_hidden(experts, xn8, gates, sel, _pack_table(u_tab), tb)
    return peer_out(experts, w, sel.T, _pack_table(v_tab), tb)


def _row(v):
    return v.reshape(1, -1).astype(F32)


def _layer0_mixer(h, w, bsz, seq):
    t = h.shape[0]
    w_in = w["ev_w_in"][0]
    w_in0 = jnp.pad(w_in[:, _even_in_perm()], ((0, 0), (0, EVEN_COLS - w_in.shape[1]))).astype(BF16)
    proj = norm_matmul(h, _row(w["norm_mix_w"][0]), w_in0, min(256, t))
    half = RET_DK // 2
    freqs = ROPE_BASE ** (-np.arange(half, dtype=np.float32) / half)
    dmask, kdec, qdec, cdec = _retention_tables()
    consts = (
        jnp.asarray(np.tile(freqs, RET_HEADS)[None, :], F32),
        jnp.pad(w["ev_gla_w_up"][0], ((0, 128 - GLA_GATE_RANK), (0, 0))).astype(BF16),
        _row(w["ev_gla_b_up"][0]), _row(w["ev_gla_norm_w"][0]), _row(w["ev_ret_norm_w"][0]),
        w["ev_w_out"][0].astype(BF16),
        jnp.asarray(np.tril(np.ones((CHUNK, CHUNK))), BF16),
        dmask, kdec, qdec, cdec,
    )
    pos = w["positions"].astype(F32).reshape(seq, 1)
    return even_mixer(h, proj, pos, consts, bsz, seq, min(256, seq))


def _layer1_mixer(h, w, bsz, seq):
    t = h.shape[0]
    u = norm_matmul(h, _row(w["norm_mix_w"][1]), w["od_w_in"][0].astype(BF16), min(512, t))
    pw_re, pw_im, bb_re, bb_im = s5_params(w["od_s5_a_re"][0], w["od_s5_a_im"][0], w["od_s5_log_dt"][0].reshape(-1, 1),
                                           w["od_s5_b_re"][0].transpose(0, 2, 1), w["od_s5_b_im"][0].transpose(0, 2, 1))
    eye = jnp.eye(S5_GROUPS, dtype=F32)
    blockdiag = lambda m: (m[:, :, None, :] * eye[:, None, :, None]).reshape(m.shape[0] * m.shape[1], -1)
    consts = (
        w["od_pool_w"][0].astype(BF16), _row(w["od_pool_scale"][0]),
        blockdiag(bb_re).astype(BF16), blockdiag(bb_im).astype(BF16),
        blockdiag(w["od_s5_c_re"][0]).T.astype(BF16), blockdiag(w["od_s5_c_im"][0]).T.astype(BF16),
        pw_re.reshape(SUBLANES, S5_STATE), pw_im.reshape(SUBLANES, S5_STATE),
        _row(w["od_s5_d"][0]), w["od_s5_w_glu"][0].astype(BF16), _row(w["od_s5_b_glu"][0]), w["od_w_out"][0].astype(BF16),
    )
    return odd_mixer(h, u, consts, bsz, seq, min(256, seq))


def kernel(x, p, positions, norm_mix_w, norm_ffn_w, norm_ple_w, final_norm_w, ev_w_in, ev_gla_w_up, ev_gla_b_up, ev_gla_norm_w, ev_ret_norm_w, ev_w_out, od_w_in, od_pool_w, od_pool_scale, od_s5_a_re, od_s5_a_im, od_s5_log_dt, od_s5_b_re, od_s5_b_im, od_s5_c_re, od_s5_c_im, od_s5_d, od_s5_w_glu, od_s5_b_glu, od_w_out, peer_w_q, peer_sub_keys, peer_u, peer_v, ple_w_proj, ple_w_gate):
    w = dict(locals())
    bsz, seq, d = x.shape
    t = bsz * seq
    tm_tok = min(512, t)
    tm_route = min(256, t)
    tb = min(128, t)
    h = x.reshape(t, d)
    for i, mixer in enumerate((_layer0_mixer, _layer1_mixer)):
        h = mixer(h, w, bsz, seq)
        e8 = _peer_layer(h, _row(norm_ffn_w[i]), peer_w_q[i], peer_sub_keys[i], peer_u[i], peer_v[i], tm_route, tb)
        h = ple(h, e8, p[i].reshape(t, -1), _row(norm_ple_w[i]), ple_w_gate[i].astype(BF16), ple_w_proj[i].astype(BF16),
                _row(final_norm_w), tm_tok, i == 1)
    return h.reshape(bsz, seq, d)
```

```python
import functools
import math

import numpy as np
import jax
import jax.numpy as jnp
from jax import lax
from jax.experimental import pallas as pl
from jax.experimental.pallas import tpu as pltpu

F32 = jnp.float32
BF16 = jnp.bfloat16

D_MODEL = 1024
NORM_EPS = 1e-6
CHUNK = 64
GLA_HEADS = 4
GLA_DK = 64
GLA_DV = 128
GLA_GATE_RANK = 16
GLA_GATE_NORM = 16.0
RET_HEADS = 4
RET_DK = 64
RET_DV = 128
ROPE_BASE = 10000.0
POOL_WINDOWS = (2, 4, 8, 16)
POOL_WIDTH = 512
POOL_GROUP_WIDTH = 128
S5_H = 16
S5_P = 64
S5_GROUPS = 32
S5_WIDTH = 512
S5_STATE = S5_GROUPS * S5_P
PEER_HEADS = 8
PEER_NKEYS = 128
PEER_TOPK = 16
PEER_HK = PEER_HEADS * PEER_TOPK
PEER_EXPERTS = PEER_NKEYS * PEER_NKEYS
EVEN_COLS = 3200

VMEM_LIMIT_BIG = 56 * 1024 * 1024
VMEM_LIMIT = 48 * 1024 * 1024
SUBLANES = 8

NT_DIMS = (((1,), (1,)), ((), ()))


def _split_bf16(x):
    hi = x.astype(BF16)
    lo = (x - hi.astype(F32)).astype(BF16)
    return hi, lo


def _rms(x, w):
    return x * lax.rsqrt(jnp.mean(x * x, axis=-1, keepdims=True) + NORM_EPS) * w


def _sigmoid(x):
    return 1.0 / (1.0 + jnp.exp(-x))


def _gelu(x):
    return 0.5 * x * (1.0 + lax.erf(x * (2.0 ** -0.5)))


def _log_sigmoid(z):
    return jnp.minimum(z, 0.0) - jnp.log1p(jnp.exp(-jnp.abs(z)))


def _norm_mm_kernel(h_ref, nw_ref, w_ref, o_ref):
    xn = _rms(h_ref[...], nw_ref[...])
    o_ref[...] = jnp.dot(xn.astype(BF16), w_ref[...], preferred_element_type=F32)


def norm_matmul(h, nw, w, tm):
    t, d = h.shape
    n = w.shape[1]
    return pl.pallas_call(
        _norm_mm_kernel,
        grid=(t // tm,),
        in_specs=[
            pl.BlockSpec((tm, d), lambda i: (i, 0)),
            pl.BlockSpec((1, d), lambda i: (0, 0)),
            pl.BlockSpec((d, n), lambda i: (0, 0)),
        ],
        out_specs=pl.BlockSpec((tm, n), lambda i: (i, 0)),
        out_shape=jax.ShapeDtypeStruct((t, n), F32),
        compiler_params=pltpu.CompilerParams(dimension_semantics=("parallel",), vmem_limit_bytes=VMEM_LIMIT),
        name="norm_matmul",
    )(h, nw, w)


def _even_mixer_kernel(h_ref, qk_ref, va_ref, ga_ref, vb_ref, gb_ref, ra_ref, pos_ref,
                       freq_ref, wup_ref, bup_ref, gnw_ref, rnw_ref, wout_ref, tril_ref,
                       dmask_ref, kdec_ref, qdec_ref, cdec_ref,
                       o_ref, gstate, rstate, y_scr):
    tm = h_ref.shape[0]

    @pl.when(pl.program_id(1) == 0)
    def _():
        gstate[...] = jnp.zeros_like(gstate)
        rstate[...] = jnp.zeros_like(rstate)

    lane = lax.broadcasted_iota(jnp.int32, (1, 256), 1)
    gla_masks = [((lane >= GLA_DK * h) & (lane < GLA_DK * (h + 1))).astype(F32) for h in range(GLA_HEADS)]
    half = RET_DK // 2
    ret_masks = [(((lane >= half * h) & (lane < half * (h + 1)))
                  | ((lane >= 128 + half * h) & (lane < 128 + half * (h + 1)))).astype(F32)
                 for h in range(RET_HEADS)]
    ri = lax.broadcasted_iota(jnp.int32, (CHUNK, CHUNK), 0)
    ci = lax.broadcasted_iota(jnp.int32, (CHUNK, CHUNK), 1)
    causal = ri >= ci
    tril = tril_ref[...]
    freqs = freq_ref[...]
    wup = wup_ref[...]
    bup = bup_ref[...]

    def chunk_body(c, carry):
        rows = pl.ds(pl.multiple_of(c * CHUNK, CHUNK), CHUNK)
        qk = qk_ref[rows, :]
        qa, ka, qb, kb = qk[:, 0:256], qk[:, 256:512], qk[:, 512:768], qk[:, 768:1024]

        z = jnp.dot(ra_ref[rows, :].astype(BF16), wup, preferred_element_type=F32) + bup
        la = _log_sigmoid(z) * (1.0 / GLA_GATE_NORM)
        lah, lal = _split_bf16(la)
        g = (jnp.dot(tril, lah, preferred_element_type=F32)
             + jnp.dot(tril, lal, preferred_element_type=F32))
        g_last = g[CHUNK - 1:CHUNK, :]
        ref = 0.5 * g_last
        qs = qa * (GLA_DK ** -0.5)
        qt = qs * jnp.exp(g - ref)
        kt = (ka * jnp.exp(ref - g)).astype(BF16)
        kd = ka * jnp.exp(g_last - g)
        qg = qs * jnp.exp(g)
        decay = jnp.exp(g_last)
        for h in range(GLA_HEADS):
            m = gla_masks[h]
            s = lax.dot_general((qt * m).astype(BF16), kt, NT_DIMS, preferred_element_type=F32)
            s = jnp.where(causal, s, 0.0)
            v = va_ref[rows, GLA_DV * h:GLA_DV * (h + 1)]
            st = gstate[h]
            o = jnp.dot(s.astype(BF16), v.astype(BF16), preferred_element_type=F32)
            o = o + lax.dot_general((qg * m).astype(BF16), st.astype(BF16), NT_DIMS, preferred_element_type=F32)
            gstate[h] = st * decay + jnp.dot(v.T.astype(BF16), (kd * m).astype(BF16), preferred_element_type=F32)
            y_scr[rows, GLA_DV * h:GLA_DV * (h + 1)] = o

        ang = pos_ref[rows, :] * freqs
        cs, sn = jnp.cos(ang), jnp.sin(ang)
        q1, q2 = qb[:, 0:128], qb[:, 128:256]
        k1, k2 = kb[:, 0:128], kb[:, 128:256]
        qr = jnp.concatenate([q1 * cs - q2 * sn, q2 * cs + q1 * sn], axis=1)
        kr = jnp.concatenate([k1 * cs - k2 * sn, k2 * cs + k1 * sn], axis=1) * (RET_DK ** -0.5)
        krb = kr.astype(BF16)
        for h in range(RET_HEADS):
            m = ret_masks[h]
            qm = (qr * m).astype(BF16)
            s = lax.dot_general(qm, krb, NT_DIMS, preferred_element_type=F32) * dmask_ref[h]
            v = vb_ref[rows, RET_DV * h:RET_DV * (h + 1)]
            st = rstate[h]
            o = jnp.dot(s.astype(BF16), v.astype(BF16), preferred_element_type=F32)
            o = o + lax.dot_general(qm, st.astype(BF16), NT_DIMS, preferred_element_type=F32) * qdec_ref[h]
            rstate[h] = st * cdec_ref[h] + jnp.dot((v * kdec_ref[h]).T.astype(BF16), (kr * m).astype(BF16),
                                                    preferred_element_type=F32)
            y_scr[rows, 512 + RET_DV * h:512 + RET_DV * (h + 1)] = o
        return carry

    lax.fori_loop(0, tm // CHUNK, chunk_body, 0)

    pieces = []
    gnw = gnw_ref[...]
    for h in range(GLA_HEADS):
        o = y_scr[:, GLA_DV * h:GLA_DV * (h + 1)]
        gt = ga_ref[:, GLA_DV * h:GLA_DV * (h + 1)]
        pieces.append(_rms(o, gnw) * (gt * _sigmoid(gt)))
    for h in range(RET_HEADS):
        o = y_scr[:, 512 + RET_DV * h:512 + RET_DV * (h + 1)]
        gt = gb_ref[:, RET_DV * h:RET_DV * (h + 1)]
        mu = jnp.mean(o, axis=-1, keepdims=True)
        oc = o - mu
        var = jnp.mean(oc * oc, axis=-1, keepdims=True)
        nrm = oc * lax.rsqrt(var + NORM_EPS) * rnw_ref[:, RET_DV * h:RET_DV * (h + 1)]
        pieces.append(nrm * (gt * _sigmoid(gt)))
    y = jnp.concatenate(pieces, axis=1).astype(BF16)
    o_ref[...] = h_ref[...] + jnp.dot(y, wout_ref[...], preferred_element_type=F32)


def even_mixer(h, proj, pos, consts, bsz, seq, tm):
    nj = seq // tm
    tok = lambda b, j: (b * nj + j, 0)
    col = lambda cb: (lambda b, j: (b * nj + j, cb))
    full2 = lambda b, j: (0, 0)
    full3 = lambda b, j: (0, 0, 0)
    (freqs, wup, bup, gnw, rnw, wout, tril, dmask, kdec, qdec, cdec) = consts
    return pl.pallas_call(
        _even_mixer_kernel,
        grid=(bsz, nj),
        in_specs=[
            pl.BlockSpec((tm, D_MODEL), tok),
            pl.BlockSpec((tm, 1024), col(0)),
            pl.BlockSpec((tm, 512), col(2)),
            pl.BlockSpec((tm, 512), col(3)),
            pl.BlockSpec((tm, 512), col(4)),
            pl.BlockSpec((tm, 512), col(5)),
            pl.BlockSpec((tm, 128), col(24)),
            pl.BlockSpec((tm, 1), lambda b, j: (j, 0)),
            pl.BlockSpec((1, 128), full2),
            pl.BlockSpec((128, 256), full2),
            pl.BlockSpec((1, 256), full2),
            pl.BlockSpec((1, 128), full2),
            pl.BlockSpec((1, 512), full2),
            pl.BlockSpec((1024, D_MODEL), full2),
            pl.BlockSpec((CHUNK, CHUNK), full2),
            pl.BlockSpec((RET_HEADS, CHUNK, CHUNK), full3),
            pl.BlockSpec((RET_HEADS, CHUNK, 128), full3),
            pl.BlockSpec((RET_HEADS, CHUNK, 128), full3),
            pl.BlockSpec((RET_HEADS, 1, 256), full3),
        ],
        out_specs=pl.BlockSpec((tm, D_MODEL), tok),
        out_shape=jax.ShapeDtypeStruct(h.shape, F32),
        scratch_shapes=[
            pltpu.VMEM((GLA_HEADS, GLA_DV, 256), F32),
            pltpu.VMEM((RET_HEADS, RET_DV, 256), F32),
            pltpu.VMEM((tm, 1024), F32),
        ],
        compiler_params=pltpu.CompilerParams(dimension_semantics=("arbitrary", "arbitrary"),
                                             vmem_limit_bytes=VMEM_LIMIT),
        name="even_mixer",
    )(h, proj, proj, proj, proj, proj, proj, pos, freqs, wup, bup, gnw, rnw, wout, tril, dmask, kdec, qdec, cdec)


def _s5_param_kernel(are_ref, aim_ref, ldt_ref, bre_ref, bim_ref,
                     pw_re_ref, pw_im_ref, bbre_ref, bbim_ref):
    a_re = are_ref[...]
    a_im = aim_ref[...]
    dt = jnp.exp(ldt_ref[...])
    for r in range(SUBLANES):
        mag = jnp.exp(a_re * dt * (r + 1.0))
        pw_re_ref[r] = mag * jnp.cos(a_im * dt * (r + 1.0))
        pw_im_ref[r] = mag * jnp.sin(a_im * dt * (r + 1.0))
    mag = jnp.exp(a_re * dt)
    abar_re, abar_im = mag * jnp.cos(a_im * dt), mag * jnp.sin(a_im * dt)
    den = a_re * a_re + a_im * a_im
    nr, ni = abar_re - 1.0, abar_im
    coef_re = (nr * a_re + ni * a_im) / den
    coef_im = (ni * a_re - nr * a_im) / den
    b_re = bre_ref[...]
    b_im = bim_ref[...]
    c_re = jnp.concatenate([coef_re] * S5_H, axis=1)
    c_im = jnp.concatenate([coef_im] * S5_H, axis=1)
    bbre_ref[...] = c_re * b_re - c_im * b_im
    bbim_ref[...] = c_re * b_im + c_im * b_re


def s5_params(a_re, a_im, log_dt, b_re_t, b_im_t):
    g, p = a_re.shape
    hh = b_re_t.shape[1]
    pw_re, pw_im, bb_re, bb_im = pl.pallas_call(
        _s5_param_kernel,
        out_shape=[jax.ShapeDtypeStruct((SUBLANES, g, p), F32), jax.ShapeDtypeStruct((SUBLANES, g, p), F32),
                   jax.ShapeDtypeStruct((g, hh * p), F32), jax.ShapeDtypeStruct((g, hh * p), F32)],
        name="s5_params",
    )(a_re, a_im, log_dt, b_re_t.reshape(g, hh * p), b_im_t.reshape(g, hh * p))
    return pw_re, pw_im, bb_re.reshape(g, hh, p), bb_im.reshape(g, hh, p)


def _odd_mixer_kernel(h_ref, u_ref, poolw_ref, pscale_ref, wbre_ref, wbim_ref, wcre_ref, wcim_ref,
                      pwre_ref, pwim_ref, dskip_ref, wglu_ref, bglu_ref, wout_ref,
                      o_ref, tail_scr, car_re, car_im, xre_scr, xim_scr):
    tm = h_ref.shape[0]
    j = pl.program_id(1)
    halo = POOL_WINDOWS[-1]

    @pl.when(j == 0)
    def _():
        tail_scr[...] = jnp.zeros_like(tail_scr)
        car_re[...] = jnp.zeros_like(car_re)
        car_im[...] = jnp.zeros_like(car_im)

    uc = u_ref[:, 0:POOL_WIDTH]
    ud = u_ref[:, POOL_WIDTH:POOL_WIDTH + S5_WIDTH]

    ext = jnp.concatenate([tail_scr[...], uc], axis=0)
    tail_scr[...] = uc[tm - halo:tm, :]
    pos = (j * tm + lax.broadcasted_iota(jnp.int32, (tm, 1), 0)).astype(F32)
    mixed = []
    for gi, win in enumerate(POOL_WINDOWS):
        a = ext[:, POOL_GROUP_WIDTH * gi:POOL_GROUP_WIDTH * (gi + 1)]
        n = tm + halo
        step = 1
        end = 0
        while step < win:
            a = a[step:n, :] + a[0:n - step, :]
            n -= step
            end += step
            step *= 2
        wsum = a[halo - end:halo - end + tm, :]
        cnt = jnp.minimum(pos + 1.0, float(win))
        pooled = wsum / cnt - uc[:, POOL_GROUP_WIDTH * gi:POOL_GROUP_WIDTH * (gi + 1)]
        mixed.append(jnp.dot(pooled.astype(BF16), poolw_ref[gi], preferred_element_type=F32))
    y_c = jnp.concatenate(mixed, axis=1) * pscale_ref[...]

    udb = ud.astype(BF16)
    xre_scr[...] = jnp.dot(udb, wbre_ref[...], preferred_element_type=F32)
    xim_scr[...] = jnp.dot(udb, wbim_ref[...], preferred_element_type=F32)
    rowi = lax.broadcasted_iota(jnp.int32, (SUBLANES, S5_STATE), 0)
    pw_re = pwre_ref[...]
    pw_im = pwim_ref[...]

    def slab(s, carry):
        cr, ci = carry
        rows = pl.ds(pl.multiple_of(s * SUBLANES, SUBLANES), SUBLANES)
        xr = xre_scr[rows, :]
        xi = xim_scr[rows, :]
        for dsh in (1, 2, 4):
            pr = pw_re[dsh - 1:dsh, :]
            pi = pw_im[dsh - 1:dsh, :]
            keep = rowi >= dsh
            sr = jnp.where(keep, pltpu.roll(xr, dsh, axis=0), 0.0)
            si = jnp.where(keep, pltpu.roll(xi, dsh, axis=0), 0.0)
            xr, xi = xr + (pr * sr - pi * si), xi + (pr * si + pi * sr)
        xr, xi = xr + (pw_re * cr - pw_im * ci), xi + (pw_re * ci + pw_im * cr)
        xre_scr[rows, :] = xr
        xim_scr[rows, :] = xi
        return xr[SUBLANES - 1:SUBLANES, :], xi[SUBLANES - 1:SUBLANES, :]

    cr, ci = lax.fori_loop(0, tm // SUBLANES, slab, (car_re[...], car_im[...]))
    car_re[...] = cr
    car_im[...] = ci

    y = (jnp.dot(xre_scr[...].astype(BF16), wcre_ref[...], preferred_element_type=F32)
         - jnp.dot(xim_scr[...].astype(BF16), wcim_ref[...], preferred_element_type=F32)
         + dskip_ref[...] * ud)
    z = _gelu(y)
    y_d = z * _sigmoid(jnp.dot(z.astype(BF16), wglu_ref[...], preferred_element_type=F32) + bglu_ref[...])

    ycat = jnp.concatenate([y_c, y_d], axis=1).astype(BF16)
    o_ref[...] = h_ref[...] + jnp.dot(ycat, wout_ref[...], preferred_element_type=F32)


def odd_mixer(h, u, consts, bsz, seq, tm):
    nj = seq // tm
    tok = lambda b, j: (b * nj + j, 0)
    full2 = lambda b, j: (0, 0)
    full3 = lambda b, j: (0, 0, 0)
    (poolw, pscale, wbre, wbim, wcre, wcim, pwre, pwim, dskip, wglu, bglu, wout) = consts
    return pl.pallas_call(
        _odd_mixer_kernel,
        grid=(bsz, nj),
        in_specs=[
            pl.BlockSpec((tm, D_MODEL), tok),
            pl.BlockSpec((tm, 1024), tok),
            pl.BlockSpec((4, 128, 128), full3),
            pl.BlockSpec((1, POOL_WIDTH), full2),
            pl.BlockSpec((S5_WIDTH, S5_STATE), full2),
            pl.BlockSpec((S5_WIDTH, S5_STATE), full2),
            pl.BlockSpec((S5_STATE, S5_WIDTH), full2),
            pl.BlockSpec((S5_STATE, S5_WIDTH), full2),
            pl.BlockSpec((SUBLANES, S5_STATE), full2),
            pl.BlockSpec((SUBLANES, S5_STATE), full2),
            pl.BlockSpec((1, S5_WIDTH), full2),
            pl.BlockSpec((S5_WIDTH, S5_WIDTH), full2),
            pl.BlockSpec((1, S5_WIDTH), full2),
            pl.BlockSpec((1024, D_MODEL), full2),
        ],
        out_specs=pl.BlockSpec((tm, D_MODEL), tok),
        out_shape=jax.ShapeDtypeStruct(h.shape, F32),
        scratch_shapes=[
            pltpu.VMEM((POOL_WINDOWS[-1], POOL_WIDTH), F32),
            pltpu.VMEM((1, S5_STATE), F32),
            pltpu.VMEM((1, S5_STATE), F32),
            pltpu.VMEM((tm, S5_STATE), F32),
            pltpu.VMEM((tm, S5_STATE), F32),
        ],
        compiler_params=pltpu.CompilerParams(dimension_semantics=("arbitrary", "arbitrary"),
                                             vmem_limit_bytes=VMEM_LIMIT),
        name="odd_mixer",
    )(h, u, poolw, pscale, wbre, wbim, wcre, wcim, pwre, pwim, dskip, wglu, bglu, wout)


def _top16_rows(s, n_rows):
    iota = lax.broadcasted_iota(jnp.int32, s.shape, 0).astype(F32)
    vals, idxs = [], []
    for _ in range(PEER_TOPK):
        m = jnp.max(s, axis=0, keepdims=True)
        am = jnp.min(jnp.where(s == m, iota, float(n_rows)), axis=0, keepdims=True)
        vals.append(m)
        idxs.append(am)
        s = jnp.where(iota == am, -jnp.inf, s)
    return jnp.concatenate(vals, axis=0), jnp.concatenate(idxs, axis=0)


def _take16(table, sel):
    out = jnp.zeros(sel.shape, table.dtype)
    for i in range(PEER_TOPK):
        out = jnp.where(sel == i, table[i:i + 1, :], out)
    return out


def _dot3(ah, al, bh, bl, dims):
    return (lax.dot_general(ah, bh, dims, preferred_element_type=F32)
            + lax.dot_general(al, bh, dims, preferred_element_type=F32)
            + lax.dot_general(ah, bl, dims, preferred_element_type=F32))


def _peer_route_kernel(h_ref, nw_ref, wqh_ref, wql_ref, kh_ref, kl_ref, xn_ref, exp_ref, gate_ref,
                       qt_scr, et_scr, gt_scr):
    xn = _rms(h_ref[...], nw_ref[...])
    for r in range(SUBLANES):
        xn_ref[:, r, :] = xn[:, 128 * r:128 * (r + 1)]
    xh, xl = _split_bf16(xn)
    qt_scr[...] = _dot3(wqh_ref[...], wql_ref[...], xh, xl, NT_DIMS)
    mm = (((1,), (0,)), ((), ()))

    def head(hd, carry):
        ra = pl.ds(pl.multiple_of(hd * 256, 256), 128)
        rb = pl.ds(pl.multiple_of(hd * 256 + 128, 128), 128)
        qah, qal = _split_bf16(qt_scr[ra, :])
        qbh, qbl = _split_bf16(qt_scr[rb, :])
        sa = _dot3(kh_ref[2 * hd], kl_ref[2 * hd], qah, qal, mm)
        sb = _dot3(kh_ref[2 * hd + 1], kl_ref[2 * hd + 1], qbh, qbl, mm)
        av, ai = _top16_rows(sa, PEER_NKEYS)
        bv, bi = _top16_rows(sb, PEER_NKEYS)
        cand = jnp.concatenate([av[i:i + 1, :] + bv for i in range(PEER_TOPK)], axis=0)
        cv, flat = _top16_rows(cand, PEER_TOPK * PEER_TOPK)
        flat = flat.astype(jnp.int32)
        e_a = _take16(ai, flat >> 4)
        e_b = _take16(bi, flat & (PEER_TOPK - 1))
        ex = jnp.exp(cv - cv[0:1, :])
        rows = pl.ds(pl.multiple_of(hd * PEER_TOPK, PEER_TOPK), PEER_TOPK)
        et_scr[rows, :] = (e_a * float(PEER_NKEYS) + e_b) * float(PEER_ROW_WORDS)
        gt_scr[rows, :] = ex / jnp.sum(ex, axis=0, keepdims=True)
        return carry

    lax.fori_loop(0, PEER_HEADS, head, 0)
    exp_ref[...] = et_scr[...].T.astype(jnp.int32)
    gate_ref[...] = gt_scr[...].T


def peer_route(h, nw, wqt, keys, tm):
    t, d = h.shape
    nq = wqt.shape[0]
    wqh, wql = _split_bf16(wqt)
    kh, kl = _split_bf16(keys)
    return pl.pallas_call(
        _peer_route_kernel,
        grid=(t // tm,),
        in_specs=[
            pl.BlockSpec((tm, d), lambda i: (i, 0)),
            pl.BlockSpec((1, d), lambda i: (0, 0)),
            pl.BlockSpec((nq, d), lambda i: (0, 0)),
            pl.BlockSpec((nq, d), lambda i: (0, 0)),
            pl.BlockSpec((2 * PEER_HEADS, PEER_NKEYS, 128), lambda i: (0, 0, 0)),
            pl.BlockSpec((2 * PEER_HEADS, PEER_NKEYS, 128), lambda i: (0, 0, 0)),
        ],
        out_specs=[
            pl.BlockSpec((tm, SUBLANES, d // SUBLANES), lambda i: (i, 0, 0)),
            pl.BlockSpec((tm, PEER_HK), lambda i: (i, 0)),
            pl.BlockSpec((tm, PEER_HK), lambda i: (i, 0)),
        ],
        out_shape=[jax.ShapeDtypeStruct((t, SUBLANES, d // SUBLANES), F32),
                   jax.ShapeDtypeStruct((t, PEER_HK), jnp.int32), jax.ShapeDtypeStruct((t, PEER_HK), F32)],
        scratch_shapes=[pltpu.VMEM((nq, tm), F32), pltpu.VMEM((PEER_HK, tm), F32), pltpu.VMEM((PEER_HK, tm), F32)],
        compiler_params=pltpu.CompilerParams(dimension_semantics=("parallel",), vmem_limit_bytes=VMEM_LIMIT),
        name="peer_route",
    )(h, nw, wqh, wql, kh, kl)


PEER_GROUP = 8
PEER_ROW_WORDS = 4


def _gather_group(idx_ref, tab_ref, g, stage_ref):
    rows = [idx_ref.at[g * PEER_GROUP + j] for j in range(PEER_GROUP)]
    for k in range(PEER_HK):
        for j in range(PEER_GROUP):
            off = pl.multiple_of(rows[j][k], PEER_ROW_WORDS)
            stage_ref[j, pl.ds(PEER_ROW_WORDS * k, PEER_ROW_WORDS), :] = tab_ref[pl.ds(off, PEER_ROW_WORDS), :]


def _gather_compute_pipeline(n_groups, idx_ref, tab_ref, consume, stage_a, stage_b):
    def compute(g, stage_ref):
        for j in range(PEER_GROUP):
            consume(g * PEER_GROUP + j, stage_ref.at[j])

    _gather_group(idx_ref, tab_ref,0, stage_a)

    def body(i, carry):
        _gather_group(idx_ref, tab_ref,2 * i + 1, stage_b)
        compute(2 * i, stage_a)
        _gather_group(idx_ref, tab_ref,2 * i + 2, stage_a)
        compute(2 * i + 1, stage_b)
        return carry

    lax.fori_loop(0, n_groups // 2 - 1, body, 0)
    _gather_group(idx_ref, tab_ref,n_groups - 1, stage_b)
    compute(n_groups - 2, stage_a)
    compute(n_groups - 1, stage_b)


def _diag_mask():
    row = lax.broadcasted_iota(jnp.int32, (SUBLANES, SUBLANES * PEER_HK), 0)
    lane = lax.broadcasted_iota(jnp.int32, (SUBLANES, SUBLANES * PEER_HK), 1)
    return (lane & (SUBLANES - 1)) == row


def _peer_hidden_kernel(idx_ref, x_ref, g_ref, sel_ref, tab_ref, o_ref, stage_a, stage_b, part_scr):
    tb = x_ref.shape[0]
    diag = _diag_mask()

    def consume(t, rows_ref):
        u = pltpu.bitcast(rows_ref[...], BF16)
        xh, xl = _split_bf16(x_ref[t])
        x16 = jnp.concatenate([xh, xl], axis=0)
        out = lax.dot_general(x16, u, NT_DIMS, preferred_element_type=F32)
        o8 = out[0:SUBLANES] + out[SUBLANES:2 * SUBLANES]
        part_scr[pl.ds(t, 1), :] = jnp.sum(jnp.where(diag, o8, 0.0), axis=0, keepdims=True)

    _gather_compute_pipeline(tb // PEER_GROUP, idx_ref, tab_ref, consume, stage_a, stage_b)
    ph, plo = _split_bf16(part_scr[...])
    sel = sel_ref[...]
    hid = jnp.dot(ph, sel, preferred_element_type=F32) + jnp.dot(plo, sel, preferred_element_type=F32)
    o_ref[...] = g_ref[...] * _gelu(hid)


def peer_hidden(idx, x8, gates, sel, tab, tb):
    t = idx.shape[0]
    return pl.pallas_call(
        _peer_hidden_kernel,
        grid=(t // tb,),
        in_specs=[
            pl.BlockSpec((tb, PEER_HK), lambda i: (i, 0), memory_space=pltpu.SMEM),
            pl.BlockSpec((tb, SUBLANES, 128), lambda i: (i, 0, 0)),
            pl.BlockSpec((tb, PEER_HK), lambda i: (i, 0)),
            pl.BlockSpec((SUBLANES * PEER_HK, PEER_HK), lambda i: (0, 0)),
            pl.BlockSpec((PEER_ROW_WORDS * PEER_EXPERTS, 128), lambda i: (0, 0), pipeline_mode=pl.Buffered(1)),
        ],
        out_specs=pl.BlockSpec((tb, PEER_HK), lambda i: (i, 0)),
        out_shape=jax.ShapeDtypeStruct((t, PEER_HK), F32),
        scratch_shapes=[pltpu.VMEM((PEER_GROUP, PEER_ROW_WORDS * PEER_HK, 128), jnp.int32),
                        pltpu.VMEM((PEER_GROUP, PEER_ROW_WORDS * PEER_HK, 128), jnp.int32),
                        pltpu.VMEM((tb, SUBLANES * PEER_HK), F32)],
        compiler_params=pltpu.CompilerParams(dimension_semantics=("arbitrary",), vmem_limit_bytes=VMEM_LIMIT_BIG),
        name="peer_hidden",
    )(idx,x8, gates, sel, tab)


def _peer_out_kernel(idx_ref, w_ref, exp_ref, tab_ref, o_ref, stage_a, stage_b, wexp_scr):
    tb = w_ref.shape[0]
    diag = _diag_mask()
    wh, wl = _split_bf16(w_ref[...])
    ex = exp_ref[...]
    wexp_scr[...] = jnp.dot(wh, ex, preferred_element_type=F32) + jnp.dot(wl, ex, preferred_element_type=F32)

    def consume(t, rows_ref):
        v = pltpu.bitcast(rows_ref[...], BF16)
        w8 = jnp.where(diag, jnp.broadcast_to(wexp_scr[pl.ds(t, 1), :], (SUBLANES, SUBLANES * PEER_HK)), 0.0)
        w8h, w8l = _split_bf16(w8)
        w16 = jnp.concatenate([w8h, w8l], axis=0)
        out = jnp.dot(w16, v, preferred_element_type=F32)
        o_ref[t] = out[0:SUBLANES] + out[SUBLANES:2 * SUBLANES]

    _gather_compute_pipeline(tb // PEER_GROUP, idx_ref, tab_ref, consume, stage_a, stage_b)


def peer_out(idx, w, expand, tab, tb):
    t = idx.shape[0]
    return pl.pallas_call(
        _peer_out_kernel,
        grid=(t // tb,),
        in_specs=[
            pl.BlockSpec((tb, PEER_HK), lambda i: (i, 0), memory_space=pltpu.SMEM),
            pl.BlockSpec((tb, PEER_HK), lambda i: (i, 0)),
            pl.BlockSpec((PEER_HK, SUBLANES * PEER_HK), lambda i: (0, 0)),
            pl.BlockSpec((PEER_ROW_WORDS * PEER_EXPERTS, 128), lambda i: (0, 0), pipeline_mode=pl.Buffered(1)),
        ],
        out_specs=pl.BlockSpec((tb, SUBLANES, 128), lambda i: (i, 0, 0)),
        out_shape=jax.ShapeDtypeStruct((t, SUBLANES, 128), F32),
        scratch_shapes=[pltpu.VMEM((PEER_GROUP, PEER_ROW_WORDS * PEER_HK, 128), jnp.int32),
                        pltpu.VMEM((PEER_GROUP, PEER_ROW_WORDS * PEER_HK, 128), jnp.int32),
                        pltpu.VMEM((tb, SUBLANES * PEER_HK), F32)],
        compiler_params=pltpu.CompilerParams(dimension_semantics=("arbitrary",), vmem_limit_bytes=VMEM_LIMIT_BIG),
        name="peer_out",
    )(idx,w, expand, tab)


def _ple_kernel(h_ref, e8_ref, p_ref, nw_ref, wg_ref, wp_ref, fw_ref, o_ref, *, final_norm):
    h = h_ref[...] + jnp.concatenate([e8_ref[:, r, :] for r in range(SUBLANES)], axis=1)
    gate = _sigmoid(jnp.dot(_rms(h, nw_ref[...]).astype(BF16), wg_ref[...], preferred_element_type=F32))
    out = h + jnp.dot(p_ref[...].astype(BF16), wp_ref[...], preferred_element_type=F32) * gate
    if final_norm:
        out = _rms(out, fw_ref[...])
    o_ref[...] = out


def ple(h, e8, p, nw, wg, wp, fw, tm, final_norm):
    t, d = h.shape
    pd = p.shape[1]
    return pl.pallas_call(
        functools.partial(_ple_kernel, final_norm=final_norm),
        grid=(t // tm,),
        in_specs=[
            pl.BlockSpec((tm, d), lambda i: (i, 0)),
            pl.BlockSpec((tm, SUBLANES, d // SUBLANES), lambda i: (i, 0, 0)),
            pl.BlockSpec((tm, pd), lambda i: (i, 0)),
            pl.BlockSpec((1, d), lambda i: (0, 0)),
            pl.BlockSpec((d, d), lambda i: (0, 0)),
            pl.BlockSpec((pd, d), lambda i: (0, 0)),
            pl.BlockSpec((1, d), lambda i: (0, 0)),
        ],
        out_specs=pl.BlockSpec((tm, d), lambda i: (i, 0)),
        out_shape=jax.ShapeDtypeStruct((t, d), F32),
        compiler_params=pltpu.CompilerParams(dimension_semantics=("parallel",), vmem_limit_bytes=VMEM_LIMIT),
        name="ple",
    )(h, e8, p, nw, wg, wp, fw)


def _pack_table(tab):
    e = tab.shape[0]
    b = lax.bitcast_convert_type(tab.astype(BF16).reshape(e, PEER_ROW_WORDS, 2, 128), jnp.uint16).astype(jnp.uint32)
    return lax.bitcast_convert_type(b[:, :, 0, :] | (b[:, :, 1, :] << 16), jnp.int32).reshape(PEER_ROW_WORDS * e, 128)


def _even_in_perm():
    off = np.cumsum([0, 256, 256, 512, 512, 16, 256, 256, 512, 512])
    qa, ka, va, ga, ra, qb, kb, vb, gb = [np.arange(off[i], off[i + 1]) for i in range(9)]
    half = RET_DK // 2
    rot = np.concatenate([np.concatenate([np.arange(h * RET_DK, h * RET_DK + half) for h in range(RET_HEADS)]),
                          np.concatenate([np.arange(h * RET_DK + half, (h + 1) * RET_DK) for h in range(RET_HEADS)])])
    return np.concatenate([qa, ka, qb[rot], kb[rot], va, ga, vb, gb, ra])


def _retention_tables():
    lg = np.log(1.0 - 2.0 ** (-5.0 - np.arange(RET_HEADS, dtype=np.float64)))
    idx = np.arange(CHUNK, dtype=np.float64)
    diff = idx[:, None] - idx[None, :]
    dmask = np.where(diff >= 0, np.exp(lg[:, None, None] * np.maximum(diff, 0.0)), 0.0)
    qdec = np.exp(lg[:, None] * (idx + 1.0))
    kdec = np.exp(lg[:, None] * (CHUNK - 1.0 - idx))
    cdec = np.exp(lg * CHUNK)
    return (jnp.asarray(dmask, F32),
            jnp.asarray(np.broadcast_to(kdec[:, :, None], (RET_HEADS, CHUNK, 128)), F32),
            jnp.asarray(np.broadcast_to(qdec[:, :, None], (RET_HEADS, CHUNK, 128)), F32),
            jnp.asarray(np.broadcast_to(cdec[:, None, None], (RET_HEADS, 1, 256)), F32))


def _peer_layer(h, nw, w_q, sub_keys, u_tab, v_tab, tm_route, tb):
    keys = sub_keys.reshape(2 * PEER_HEADS, PEER_NKEYS, -1)
    xn8, experts, gates = peer_route(h, nw, w_q.T, keys, tm_route)
    kk = np.arange(SUBLANES * PEER_HK) // SUBLANES
    sel = jnp.asarray(kk[:, None] == np.arange(PEER_HK)[None, :], BF16)
    w = peer_hidden(experts, xn8, gates, sel, _pack_table(u_tab), tb)
    return peer_out(experts, w, sel.T, _pack_table(v_tab), tb)


def _row(v):
    return v.reshape(1, -1).astype(F32)


def _layer0_mixer(h, w, bsz, seq):
    t = h.shape[0]
    w_in = w["ev_w_in"][0]
    w_in0 = jnp.pad(w_in[:, _even_in_perm()], ((0, 0), (0, EVEN_COLS - w_in.shape[1]))).astype(BF16)
    proj = norm_matmul(h, _row(w["norm_mix_w"][0]), w_in0, min(256, t))
    half = RET_DK // 2
    freqs = ROPE_BASE ** (-np.arange(half, dtype=np.float32) / half)
    dmask, kdec, qdec, cdec = _retention_tables()
    consts = (
        jnp.asarray(np.tile(freqs, RET_HEADS)[None, :], F32),
        jnp.pad(w["ev_gla_w_up"][0], ((0, 128 - GLA_GATE_RANK), (0, 0))).astype(BF16),
        _row(w["ev_gla_b_up"][0]), _row(w["ev_gla_norm_w"][0]), _row(w["ev_ret_norm_w"][0]),
        w["ev_w_out"][0].astype(BF16),
        jnp.asarray(np.tril(np.ones((CHUNK, CHUNK))), BF16),
        dmask, kdec, qdec, cdec,
    )
    pos = w["positions"].astype(F32).reshape(seq, 1)
    return even_mixer(h, proj, pos, consts, bsz, seq, min(256, seq))


def _layer1_mixer(h, w, bsz, seq):
    t = h.shape[0]
    u = norm_matmul(h, _row(w["norm_mix_w"][1]), w["od_w_in"][0].astype(BF16), min(512, t))
    pw_re, pw_im, bb_re, bb_im = s5_params(w["od_s5_a_re"][0], w["od_s5_a_im"][0], w["od_s5_log_dt"][0].reshape(-1, 1),
                                           w["od_s5_b_re"][0].transpose(0, 2, 1), w["od_s5_b_im"][0].transpose(0, 2, 1))
    eye = jnp.eye(S5_GROUPS, dtype=F32)
    blockdiag = lambda m: (m[:, :, None, :] * eye[:, None, :, None]).reshape(m.shape[0] * m.shape[1], -1)
    consts = (
        w["od_pool_w"][0].astype(BF16), _row(w["od_pool_scale"][0]),
        blockdiag(bb_re).astype(BF16), blockdiag(bb_im).astype(BF16),
        blockdiag(w["od_s5_c_re"][0]).T.astype(BF16), blockdiag(w["od_s5_c_im"][0]).T.astype(BF16),
        pw_re.reshape(SUBLANES, S5_STATE), pw_im.reshape(SUBLANES, S5_STATE),
        _row(w["od_s5_d"][0]), w["od_s5_w_glu"][0].astype(BF16), _row(w["od_s5_b_glu"][0]), w["od_w_out"][0].astype(BF16),
    )
    return odd_mixer(h, u, consts, bsz, seq, min(256, seq))


def kernel(x, p, positions, norm_mix_w, norm_ffn_w, norm_ple_w, final_norm_w, ev_w_in, ev_gla_w_up, ev_gla_b_up, ev_gla_norm_w, ev_ret_norm_w, ev_w_out, od_w_in, od_pool_w, od_pool_scale, od_s5_a_re, od_s5_a_im, od_s5_log_dt, od_s5_b_re, od_s5_b_im, od_s5_c_re, od_s5_c_im, od_s5_d, od_s5_w_glu, od_s5_b_glu, od_w_out, peer_w_q, peer_sub_keys, peer_u, peer_v, ple_w_proj, ple_w_gate):
    w = dict(locals())
    bsz, seq, d = x.shape
    t = bsz * seq
    tm_tok = min(512, t)
    tm_route = min(256, t)
    tb = min(128, t)
    h = x.reshape(t, d)
    for i, mixer in enumerate((_layer0_mixer, _layer1_mixer)):
        h = mixer(h, w, bsz, seq)
        e8 = _peer_layer(h, _row(norm_ffn_w[i]), peer_w_q[i], peer_sub_keys[i], peer_u[i], peer_v[i], tm_route, tb)
        h = ple(h, e8, p[i].reshape(t, -1), _row(norm_ple_w[i]), ple_w_gate[i].astype(BF16), ple_w_proj[i].astype(BF16),
                _row(final_norm_w), tm_tok, i == 1)
    return h.reshape(bsz, seq, d)
```

```python
import functools
import math

import numpy as np
import jax
import jax.numpy as jnp
from jax import lax
from jax.experimental import pallas as pl
from jax.experimental.pallas import tpu as pltpu

F32 = jnp.float32
BF16 = jnp.bfloat16

D_MODEL = 1024
NORM_EPS = 1e-6
CHUNK = 64
GLA_HEADS = 4
GLA_DK = 64
GLA_DV = 128
GLA_GATE_RANK = 16
GLA_GATE_NORM = 16.0
RET_HEADS = 4
RET_DK = 64
RET_DV = 128
ROPE_BASE = 10000.0
POOL_WINDOWS = (2, 4, 8, 16)
POOL_WIDTH = 512
POOL_GROUP_WIDTH = 128
S5_H = 16
S5_P = 64
S5_GROUPS = 32
S5_WIDTH = 512
S5_STATE = S5_GROUPS * S5_P
PEER_HEADS = 8
PEER_NKEYS = 128
PEER_TOPK = 16
PEER_HK = PEER_HEADS * PEER_TOPK
PEER_EXPERTS = PEER_NKEYS * PEER_NKEYS
EVEN_COLS = 3200

VMEM_LIMIT_BIG = 56 * 1024 * 1024
VMEM_LIMIT = 48 * 1024 * 1024
SUBLANES = 8

NT_DIMS = (((1,), (1,)), ((), ()))


def _split_bf16(x):
    hi = x.astype(BF16)
    lo = (x - hi.astype(F32)).astype(BF16)
    return hi, lo


def _rms(x, w):
    return x * lax.rsqrt(jnp.mean(x * x, axis=-1, keepdims=True) + NORM_EPS) * w


def _sigmoid(x):
    return 1.0 / (1.0 + jnp.exp(-x))


def _gelu(x):
    return 0.5 * x * (1.0 + lax.erf(x * (2.0 ** -0.5)))


def _log_sigmoid(z):
    return jnp.minimum(z, 0.0) - jnp.log1p(jnp.exp(-jnp.abs(z)))


def _norm_mm_kernel(h_ref, nw_ref, w_ref, o_ref):
    xn = _rms(h_ref[...], nw_ref[...])
    o_ref[...] = jnp.dot(xn.astype(BF16), w_ref[...], preferred_element_type=F32)


def norm_matmul(h, nw, w, tm):
    t, d = h.shape
    n = w.shape[1]
    return pl.pallas_call(
        _norm_mm_kernel,
        grid=(t // tm,),
        in_specs=[
            pl.BlockSpec((tm, d), lambda i: (i, 0)),
            pl.BlockSpec((1, d), lambda i: (0, 0)),
            pl.BlockSpec((d, n), lambda i: (0, 0)),
        ],
        out_specs=pl.BlockSpec((tm, n), lambda i: (i, 0)),
        out_shape=jax.ShapeDtypeStruct((t, n), F32),
        compiler_params=pltpu.CompilerParams(dimension_semantics=("parallel",), vmem_limit_bytes=VMEM_LIMIT),
        name="norm_matmul",
    )(h, nw, w)


def _even_mixer_kernel(h_ref, qk_ref, va_ref, ga_ref, vb_ref, gb_ref, ra_ref, pos_ref,
                       freq_ref, wup_ref, bup_ref, gnw_ref, rnw_ref, wout_ref, tril_ref,
                       dmask_ref, kdec_ref, qdec_ref, cdec_ref,
                       o_ref, gstate, rstate, y_scr):
    tm = h_ref.shape[0]

    @pl.when(pl.program_id(1) == 0)
    def _():
        gstate[...] = jnp.zeros_like(gstate)
        rstate[...] = jnp.zeros_like(rstate)

    lane = lax.broadcasted_iota(jnp.int32, (1, 256), 1)
    gla_masks = [((lane >= GLA_DK * h) & (lane < GLA_DK * (h + 1))).astype(F32) for h in range(GLA_HEADS)]
    half = RET_DK // 2
    ret_masks = [(((lane >= half * h) & (lane < half * (h + 1)))
                  | ((lane >= 128 + half * h) & (lane < 128 + half * (h + 1)))).astype(F32)
                 for h in range(RET_HEADS)]
    ri = lax.broadcasted_iota(jnp.int32, (CHUNK, CHUNK), 0)
    ci = lax.broadcasted_iota(jnp.int32, (CHUNK, CHUNK), 1)
    causal = ri >= ci
    tril = tril_ref[...]
    freqs = freq_ref[...]
    wup = wup_ref[...]
    bup = bup_ref[...]

    def chunk_body(c, carry):
        rows = pl.ds(pl.multiple_of(c * CHUNK, CHUNK), CHUNK)
        qk = qk_ref[rows, :]
        qa, ka, qb, kb = qk[:, 0:256], qk[:, 256:512], qk[:, 512:768], qk[:, 768:1024]

        z = jnp.dot(ra_ref[rows, :].astype(BF16), wup, preferred_element_type=F32) + bup
        la = _log_sigmoid(z) * (1.0 / GLA_GATE_NORM)
        lah, lal = _split_bf16(la)
        g = (jnp.dot(tril, lah, preferred_element_type=F32)
             + jnp.dot(tril, lal, preferred_element_type=F32))
        g_last = g[CHUNK - 1:CHUNK, :]
        ref = 0.5 * g_last
        qs = qa * (GLA_DK ** -0.5)
        qt = qs * jnp.exp(g - ref)
        kt = (ka * jnp.exp(ref - g)).astype(BF16)
        kd = ka * jnp.exp(g_last - g)
        qg = qs * jnp.exp(g)
        decay = jnp.exp(g_last)
        for h in range(GLA_HEADS):
            m = gla_masks[h]
            s = lax.dot_general((qt * m).astype(BF16), kt, NT_DIMS, preferred_element_type=F32)
            s = jnp.where(causal, s, 0.0)
            v = va_ref[rows, GLA_DV * h:GLA_DV * (h + 1)]
            st = gstate[h]
            o = jnp.dot(s.astype(BF16), v.astype(BF16), preferred_element_type=F32)
            o = o + lax.dot_general((qg * m).astype(BF16), st.astype(BF16), NT_DIMS, preferred_element_type=F32)
            gstate[h] = st * decay + jnp.dot(v.T.astype(BF16), (kd * m).astype(BF16), preferred_element_type=F32)
            y_scr[rows, GLA_DV * h:GLA_DV * (h + 1)] = o

        ang = pos_ref[rows, :] * freqs
        cs, sn = jnp.cos(ang), jnp.sin(ang)
        q1, q2 = qb[:, 0:128], qb[:, 128:256]
        k1, k2 = kb[:, 0:128], kb[:, 128:256]
        qr = jnp.concatenate([q1 * cs - q2 * sn, q2 * cs + q1 * sn], axis=1)
        kr = jnp.concatenate([k1 * cs - k2 * sn, k2 * cs + k1 * sn], axis=1) * (RET_DK ** -0.5)
        krb = kr.astype(BF16)
        for h in range(RET_HEADS):
            m = ret_masks[h]
            qm = (qr * m).astype(BF16)
            s = lax.dot_general(qm, krb, NT_DIMS, preferred_element_type=F32) * dmask_ref[h]
            v = vb_ref[rows, RET_DV * h:RET_DV * (h + 1)]
            st = rstate[h]
            o = jnp.dot(s.astype(BF16), v.astype(BF16), preferred_element_type=F32)
            o = o + lax.dot_general(qm, st.astype(BF16), NT_DIMS, preferred_element_type=F32) * qdec_ref[h]
            rstate[h] = st * cdec_ref[h] + jnp.dot((v * kdec_ref[h]).T.astype(BF16), (kr * m).astype(BF16),
                                                    preferred_element_type=F32)
            y_scr[rows, 512 + RET_DV * h:512 + RET_DV * (h + 1)] = o
        return carry

    lax.fori_loop(0, tm // CHUNK, chunk_body, 0)

    pieces = []
    gnw = gnw_ref[...]
    for h in range(GLA_HEADS):
        o = y_scr[:, GLA_DV * h:GLA_DV * (h + 1)]
        gt = ga_ref[:, GLA_DV * h:GLA_DV * (h + 1)]
        pieces.append(_rms(o, gnw) * (gt * _sigmoid(gt)))
    for h in range(RET_HEADS):
        o = y_scr[:, 512 + RET_DV * h:512 + RET_DV * (h + 1)]
        gt = gb_ref[:, RET_DV * h:RET_DV * (h + 1)]
        mu = jnp.mean(o, axis=-1, keepdims=True)
        oc = o - mu
        var = jnp.mean(oc * oc, axis=-1, keepdims=True)
        nrm = oc * lax.rsqrt(var + NORM_EPS) * rnw_ref[:, RET_DV * h:RET_DV * (h + 1)]
        pieces.append(nrm * (gt * _sigmoid(gt)))
    y = jnp.concatenate(pieces, axis=1).astype(BF16)
    o_ref[...] = h_ref[...] + jnp.dot(y, wout_ref[...], preferred_element_type=F32)


def even_mixer(h, proj, pos, consts, bsz, seq, tm):
    nj = seq // tm
    tok = lambda b, j: (b * nj + j, 0)
    col = lambda cb: (lambda b, j: (b * nj + j, cb))
    full2 = lambda b, j: (0, 0)
    full3 = lambda b, j: (0, 0, 0)
    (freqs, wup, bup, gnw, rnw, wout, tril, dmask, kdec, qdec, cdec) = consts
    return pl.pallas_call(
        _even_mixer_kernel,
        grid=(bsz, nj),
        in_specs=[
            pl.BlockSpec((tm, D_MODEL), tok),
            pl.BlockSpec((tm, 1024), col(0)),
            pl.BlockSpec((tm, 512), col(2)),
            pl.BlockSpec((tm, 512), col(3)),
            pl.BlockSpec((tm, 512), col(4)),
            pl.BlockSpec((tm, 512), col(5)),
            pl.BlockSpec((tm, 128), col(24)),
            pl.BlockSpec((tm, 1), lambda b, j: (j, 0)),
            pl.BlockSpec((1, 128), full2),
            pl.BlockSpec((128, 256), full2),
            pl.BlockSpec((1, 256), full2),
            pl.BlockSpec((1, 128), full2),
            pl.BlockSpec((1, 512), full2),
            pl.BlockSpec((1024, D_MODEL), full2),
            pl.BlockSpec((CHUNK, CHUNK), full2),
            pl.BlockSpec((RET_HEADS, CHUNK, CHUNK), full3),
            pl.BlockSpec((RET_HEADS, CHUNK, 128), full3),
            pl.BlockSpec((RET_HEADS, CHUNK, 128), full3),
            pl.BlockSpec((RET_HEADS, 1, 256), full3),
        ],
        out_specs=pl.BlockSpec((tm, D_MODEL), tok),
        out_shape=jax.ShapeDtypeStruct(h.shape, F32),
        scratch_shapes=[
            pltpu.VMEM((GLA_HEADS, GLA_DV, 256), F32),
            pltpu.VMEM((RET_HEADS, RET_DV, 256), F32),
            pltpu.VMEM((tm, 1024), F32),
        ],
        compiler_params=pltpu.CompilerParams(dimension_semantics=("arbitrary", "arbitrary"),
                                             vmem_limit_bytes=VMEM_LIMIT),
        name="even_mixer",
    )(h, proj, proj, proj, proj, proj, proj, pos, freqs, wup, bup, gnw, rnw, wout, tril, dmask, kdec, qdec, cdec)


def _s5_param_kernel(are_ref, aim_ref, ldt_ref, bre_ref, bim_ref,
                     pw_re_ref, pw_im_ref, bbre_ref, bbim_ref):
    a_re = are_ref[...]
    a_im = aim_ref[...]
    dt = jnp.exp(ldt_ref[...])
    for r in range(SUBLANES):
        mag = jnp.exp(a_re * dt * (r + 1.0))
        pw_re_ref[r] = mag * jnp.cos(a_im * dt * (r + 1.0))
        pw_im_ref[r] = mag * jnp.sin(a_im * dt * (r + 1.0))
    mag = jnp.exp(a_re * dt)
    abar_re, abar_im = mag * jnp.cos(a_im * dt), mag * jnp.sin(a_im * dt)
    den = a_re * a_re + a_im * a_im
    nr, ni = abar_re - 1.0, abar_im
    coef_re = (nr * a_re + ni * a_im) / den
    coef_im = (ni * a_re - nr * a_im) / den
    b_re = bre_ref[...]
    b_im = bim_ref[...]
    c_re = jnp.concatenate([coef_re] * S5_H, axis=1)
    c_im = jnp.concatenate([coef_im] * S5_H, axis=1)
    bbre_ref[...] = c_re * b_re - c_im * b_im
    bbim_ref[...] = c_re * b_im + c_im * b_re


def s5_params(a_re, a_im, log_dt, b_re_t, b_im_t):
    g, p = a_re.shape
    hh = b_re_t.shape[1]
    pw_re, pw_im, bb_re, bb_im = pl.pallas_call(
        _s5_param_kernel,
        out_shape=[jax.ShapeDtypeStruct((SUBLANES, g, p), F32), jax.ShapeDtypeStruct((SUBLANES, g, p), F32),
                   jax.ShapeDtypeStruct((g, hh * p), F32), jax.ShapeDtypeStruct((g, hh * p), F32)],
        name="s5_params",
    )(a_re, a_im, log_dt, b_re_t.reshape(g, hh * p), b_im_t.reshape(g, hh * p))
    return pw_re, pw_im, bb_re.reshape(g, hh, p), bb_im.reshape(g, hh, p)


def _odd_mixer_kernel(h_ref, u_ref, poolw_ref, pscale_ref, wbre_ref, wbim_ref, wcre_ref, wcim_ref,
                      pwre_ref, pwim_ref, dskip_ref, wglu_ref, bglu_ref, wout_ref,
                      o_ref, tail_scr, car_re, car_im, xre_scr, xim_scr):
    tm = h_ref.shape[0]
    j = pl.program_id(1)
    halo = POOL_WINDOWS[-1]

    @pl.when(j == 0)
    def _():
        tail_scr[...] = jnp.zeros_like(tail_scr)
        car_re[...] = jnp.zeros_like(car_re)
        car_im[...] = jnp.zeros_like(car_im)

    uc = u_ref[:, 0:POOL_WIDTH]
    ud = u_ref[:, POOL_WIDTH:POOL_WIDTH + S5_WIDTH]

    ext = jnp.concatenate([tail_scr[...], uc], axis=0)
    tail_scr[...] = uc[tm - halo:tm, :]
    pos = (j * tm + lax.broadcasted_iota(jnp.int32, (tm, 1), 0)).astype(F32)
    mixed = []
    for gi, win in enumerate(POOL_WINDOWS):
        a = ext[:, POOL_GROUP_WIDTH * gi:POOL_GROUP_WIDTH * (gi + 1)]
        n = tm + halo
        step = 1
        end = 0
        while step < win:
            a = a[step:n, :] + a[0:n - step, :]
            n -= step
            end += step
            step *= 2
        wsum = a[halo - end:halo - end + tm, :]
        cnt = jnp.minimum(pos + 1.0, float(win))
        pooled = wsum / cnt - uc[:, POOL_GROUP_WIDTH * gi:POOL_GROUP_WIDTH * (gi + 1)]
        mixed.append(jnp.dot(pooled.astype(BF16), poolw_ref[gi], preferred_element_type=F32))
    y_c = jnp.concatenate(mixed, axis=1) * pscale_ref[...]

    udb = ud.astype(BF16)
    xre_scr[...] = jnp.dot(udb, wbre_ref[...], preferred_element_type=F32)
    xim_scr[...] = jnp.dot(udb, wbim_ref[...], preferred_element_type=F32)
    rowi = lax.broadcasted_iota(jnp.int32, (SUBLANES, S5_STATE), 0)
    pw_re = pwre_ref[...]
    pw_im = pwim_ref[...]

    def slab(s, carry):
        cr, ci = carry
        rows = pl.ds(pl.multiple_of(s * SUBLANES, SUBLANES), SUBLANES)
        xr = xre_scr[rows, :]
        xi = xim_scr[rows, :]
        for dsh in (1, 2, 4):
            pr = pw_re[dsh - 1:dsh, :]
            pi = pw_im[dsh - 1:dsh, :]
            keep = rowi >= dsh
            sr = jnp.where(keep, pltpu.roll(xr, dsh, axis=0), 0.0)
            si = jnp.where(keep, pltpu.roll(xi, dsh, axis=0), 0.0)
            xr, xi = xr + (pr * sr - pi * si), xi + (pr * si + pi * sr)
        xr, xi = xr + (pw_re * cr - pw_im * ci), xi + (pw_re * ci + pw_im * cr)
        xre_scr[rows, :] = xr
        xim_scr[rows, :] = xi
        return xr[SUBLANES - 1:SUBLANES, :], xi[SUBLANES - 1:SUBLANES, :]

    cr, ci = lax.fori_loop(0, tm // SUBLANES, slab, (car_re[...], car_im[...]))
    car_re[...] = cr
    car_im[...] = ci

    y = (jnp.dot(xre_scr[...].astype(BF16), wcre_ref[...], preferred_element_type=F32)
         - jnp.dot(xim_scr[...].astype(BF16), wcim_ref[...], preferred_element_type=F32)
         + dskip_ref[...] * ud)
    z = _gelu(y)
    y_d = z * _sigmoid(jnp.dot(z.astype(BF16), wglu_ref[...], preferred_element_type=F32) + bglu_ref[...])

    ycat = jnp.concatenate([y_c, y_d], axis=1).astype(BF16)
    o_ref[...] = h_ref[...] + jnp.dot(ycat, wout_ref[...], preferred_element_type=F32)


def odd_mixer(h, u, consts, bsz, seq, tm):
    nj = seq // tm
    tok = lambda b, j: (b * nj + j, 0)
    full2 = lambda b, j: (0, 0)
    full3 = lambda b, j: (0, 0, 0)
    (poolw, pscale, wbre, wbim, wcre, wcim, pwre, pwim, dskip, wglu, bglu, wout) = consts
    return pl.pallas_call(
        _odd_mixer_kernel,
        grid=(bsz, nj),
        in_specs=[
            pl.BlockSpec((tm, D_MODEL), tok),
            pl.BlockSpec((tm, 1024), tok),
            pl.BlockSpec((4, 128, 128), full3),
            pl.BlockSpec((1, POOL_WIDTH), full2),
            pl.BlockSpec((S5_WIDTH, S5_STATE), full2),
            pl.BlockSpec((S5_WIDTH, S5_STATE), full2),
            pl.BlockSpec((S5_STATE, S5_WIDTH), full2),
            pl.BlockSpec((S5_STATE, S5_WIDTH), full2),
            pl.BlockSpec((SUBLANES, S5_STATE), full2),
            pl.BlockSpec((SUBLANES, S5_STATE), full2),
            pl.BlockSpec((1, S5_WIDTH), full2),
            pl.BlockSpec((S5_WIDTH, S5_WIDTH), full2),
            pl.BlockSpec((1, S5_WIDTH), full2),
            pl.BlockSpec((1024, D_MODEL), full2),
        ],
        out_specs=pl.BlockSpec((tm, D_MODEL), tok),
        out_shape=jax.ShapeDtypeStruct(h.shape, F32),
        scratch_shapes=[
            pltpu.VMEM((POOL_WINDOWS[-1], POOL_WIDTH), F32),
            pltpu.VMEM((1, S5_STATE), F32),
            pltpu.VMEM((1, S5_STATE), F32),
            pltpu.VMEM((tm, S5_STATE), F32),
            pltpu.VMEM((tm, S5_STATE), F32),
        ],
        compiler_params=pltpu.CompilerParams(dimension_semantics=("arbitrary", "arbitrary"),
                                             vmem_limit_bytes=VMEM_LIMIT),
        name="odd_mixer",
    )(h, u, poolw, pscale, wbre, wbim, wcre, wcim, pwre, pwim, dskip, wglu, bglu, wout)


def _top16_rows(s, ids, id_bound):
    vals, idxs = [], []
    for _ in range(PEER_TOPK):
        m = jnp.max(s, axis=0, keepdims=True)
        am = jnp.min(jnp.where(s == m, ids, float(id_bound)), axis=0, keepdims=True)
        vals.append(m)
        idxs.append(am)
        s = jnp.where(ids == am, -jnp.inf, s)
    return jnp.concatenate(vals, axis=0), jnp.concatenate(idxs, axis=0)


_PAIR_BLOCKS = (("b", 0, 0), ("b", 1, 0), ("b", 2, 0), ("b", 3, 0), ("b", 4, 0),
                ("a", 0, 8), ("a", 0, 0), ("a", 1, 0), ("b", 0, 8))
_PAIR_ID_BOUND = 4 * PEER_TOPK * PEER_TOPK


def _pair_block_ids(tm):
    r = lax.broadcasted_iota(jnp.int32, (SUBLANES, tm), 0).astype(F32)
    seen = set()
    out = []
    for side, fixed, start in _PAIR_BLOCKS:
        ids = jnp.zeros((SUBLANES, tm), F32)
        for q in range(SUBLANES):
            i, j = (fixed, start + q) if side == "a" else (start + q, fixed)
            ok = (i + 1) * (j + 1) <= PEER_TOPK and (i, j) not in seen
            seen.add((i, j))
            ids = jnp.where(r == q, float(i * PEER_TOPK + j if ok else _PAIR_ID_BOUND + len(seen)), ids)
        out.append(ids)
    assert len({p for p in seen if (p[0] + 1) * (p[1] + 1) <= PEER_TOPK}) == 50
    return jnp.concatenate(out, axis=0)


def _pair_block_sums(av, bv):
    out = []
    for side, fixed, start in _PAIR_BLOCKS:
        if side == "a":
            out.append(av[fixed:fixed + 1, :] + bv[start:start + SUBLANES, :])
        else:
            out.append(av[start:start + SUBLANES, :] + bv[fixed:fixed + 1, :])
    return jnp.concatenate(out, axis=0)


def _take16(table, sel):
    out = jnp.zeros(sel.shape, table.dtype)
    for i in range(PEER_TOPK):
        out = jnp.where(sel == i, table[i:i + 1, :], out)
    return out


def _dot3(ah, al, bh, bl, dims):
    return (lax.dot_general(ah, bh, dims, preferred_element_type=F32)
            + lax.dot_general(al, bh, dims, preferred_element_type=F32)
            + lax.dot_general(ah, bl, dims, preferred_element_type=F32))


def _peer_route_kernel(h_ref, nw_ref, wqh_ref, wql_ref, kh_ref, kl_ref, xn_ref, exp_ref, gate_ref,
                       qt_scr, et_scr, gt_scr):
    xn = _rms(h_ref[...], nw_ref[...])
    for r in range(SUBLANES):
        xn_ref[:, r, :] = xn[:, 128 * r:128 * (r + 1)]
    xh, xl = _split_bf16(xn)
    qt_scr[...] = _dot3(wqh_ref[...], wql_ref[...], xh, xl, NT_DIMS)
    mm = (((1,), (0,)), ((), ()))
    tm = h_ref.shape[0]
    key_ids = lax.broadcasted_iota(jnp.int32, (PEER_NKEYS, tm), 0).astype(F32)
    pair_ids = _pair_block_ids(tm)
    pair_ok = pair_ids < float(_PAIR_ID_BOUND)

    def head(hd, carry):
        ra = pl.ds(pl.multiple_of(hd * 256, 256), 128)
        rb = pl.ds(pl.multiple_of(hd * 256 + 128, 128), 128)
        qah, qal = _split_bf16(qt_scr[ra, :])
        qbh, qbl = _split_bf16(qt_scr[rb, :])
        sa = _dot3(kh_ref[2 * hd], kl_ref[2 * hd], qah, qal, mm)
        sb = _dot3(kh_ref[2 * hd + 1], kl_ref[2 * hd + 1], qbh, qbl, mm)
        av, ai = _top16_rows(sa, key_ids, PEER_NKEYS)
        bv, bi = _top16_rows(sb, key_ids, PEER_NKEYS)
        cand = jnp.where(pair_ok, _pair_block_sums(av, bv), -jnp.inf)
        cv, flat = _top16_rows(cand, pair_ids, _PAIR_ID_BOUND)
        flat = flat.astype(jnp.int32)
        e_a = _take16(ai, flat >> 4)
        e_b = _take16(bi, flat & (PEER_TOPK - 1))
        ex = jnp.exp(cv - cv[0:1, :])
        rows = pl.ds(pl.multiple_of(hd * PEER_TOPK, PEER_TOPK), PEER_TOPK)
        et_scr[rows, :] = (e_a * float(PEER_NKEYS) + e_b) * float(PEER_ROW_WORDS)
        gt_scr[rows, :] = ex / jnp.sum(ex, axis=0, keepdims=True)
        return carry

    def head_pair(i, carry):
        head(2 * i, carry)
        return head(2 * i + 1, carry)

    lax.fori_loop(0, PEER_HEADS // 2, head_pair, 0)
    exp_ref[...] = et_scr[...].T.astype(jnp.int32)
    gate_ref[...] = gt_scr[...].T


def peer_route(h, nw, wqt, keys, tm):
    t, d = h.shape
    nq = wqt.shape[0]
    wqh, wql = _split_bf16(wqt)
    kh, kl = _split_bf16(keys)
    return pl.pallas_call(
        _peer_route_kernel,
        grid=(t // tm,),
        in_specs=[
            pl.BlockSpec((tm, d), lambda i: (i, 0)),
            pl.BlockSpec((1, d), lambda i: (0, 0)),
            pl.BlockSpec((nq, d), lambda i: (0, 0)),
            pl.BlockSpec((nq, d), lambda i: (0, 0)),
            pl.BlockSpec((2 * PEER_HEADS, PEER_NKEYS, 128), lambda i: (0, 0, 0)),
            pl.BlockSpec((2 * PEER_HEADS, PEER_NKEYS, 128), lambda i: (0, 0, 0)),
        ],
        out_specs=[
            pl.BlockSpec((tm, SUBLANES, d // SUBLANES), lambda i: (i, 0, 0)),
            pl.BlockSpec((tm, PEER_HK), lambda i: (i, 0)),
            pl.BlockSpec((tm, PEER_HK), lambda i: (i, 0)),
        ],
        out_shape=[jax.ShapeDtypeStruct((t, SUBLANES, d // SUBLANES), F32),
                   jax.ShapeDtypeStruct((t, PEER_HK), jnp.int32), jax.ShapeDtypeStruct((t, PEER_HK), F32)],
        scratch_shapes=[pltpu.VMEM((nq, tm), F32), pltpu.VMEM((PEER_HK, tm), F32), pltpu.VMEM((PEER_HK, tm), F32)],
        compiler_params=pltpu.CompilerParams(dimension_semantics=("parallel",), vmem_limit_bytes=VMEM_LIMIT),
        name="peer_route",
    )(h, nw, wqh, wql, kh, kl)


PEER_GROUP = 8
PEER_ROW_WORDS = 4


def _gather_group(idx_ref, tab_ref, g, stage_ref):
    rows = [idx_ref.at[g * PEER_GROUP + j] for j in range(PEER_GROUP)]
    for k in range(PEER_HK):
        for j in range(PEER_GROUP):
            off = pl.multiple_of(rows[j][k], PEER_ROW_WORDS)
            stage_ref[j, pl.ds(PEER_ROW_WORDS * k, PEER_ROW_WORDS), :] = tab_ref[pl.ds(off, PEER_ROW_WORDS), :]


def _gather_compute_pipeline(n_groups, idx_ref, tab_ref, consume, stage_a, stage_b):
    def compute(g, stage_ref):
        for j in range(PEER_GROUP):
            consume(g * PEER_GROUP + j, stage_ref.at[j])

    _gather_group(idx_ref, tab_ref,0, stage_a)

    def body(i, carry):
        compute(2 * i, stage_a)
        _gather_group(idx_ref, tab_ref, 2 * i + 1, stage_b)
        compute(2 * i + 1, stage_b)
        _gather_group(idx_ref, tab_ref, 2 * i + 2, stage_a)
        return carry

    lax.fori_loop(0, n_groups // 2 - 1, body, 0)
    _gather_group(idx_ref, tab_ref,n_groups - 1, stage_b)
    compute(n_groups - 2, stage_a)
    compute(n_groups - 1, stage_b)


def _diag_mask():
    row = lax.broadcasted_iota(jnp.int32, (SUBLANES, SUBLANES * PEER_HK), 0)
    lane = lax.broadcasted_iota(jnp.int32, (SUBLANES, SUBLANES * PEER_HK), 1)
    return (lane & (SUBLANES - 1)) == row


def _peer_hidden_kernel(idx_ref, x_ref, g_ref, sel_ref, tab_ref, o_ref, stage_a, stage_b, part_scr):
    tb = x_ref.shape[0]
    diag = _diag_mask()

    def consume(t, rows_ref):
        u = pltpu.bitcast(rows_ref[...], BF16)
        xh, xl = _split_bf16(x_ref[t])
        x16 = jnp.concatenate([xh, xl], axis=0)
        out = lax.dot_general(x16, u, NT_DIMS, preferred_element_type=F32)
        o8 = out[0:SUBLANES] + out[SUBLANES:2 * SUBLANES]
        part_scr[pl.ds(t, 1), :] = jnp.sum(jnp.where(diag, o8, 0.0), axis=0, keepdims=True)

    _gather_compute_pipeline(tb // PEER_GROUP, idx_ref, tab_ref, consume, stage_a, stage_b)
    ph, plo = _split_bf16(part_scr[...])
    sel = sel_ref[...]
    hid = jnp.dot(ph, sel, preferred_element_type=F32) + jnp.dot(plo, sel, preferred_element_type=F32)
    o_ref[...] = g_ref[...] * _gelu(hid)


def peer_hidden(idx, x8, gates, sel, tab, tb):
    t = idx.shape[0]
    return pl.pallas_call(
        _peer_hidden_kernel,
        grid=(t // tb,),
        in_specs=[
            pl.BlockSpec((tb, PEER_HK), lambda i: (i, 0), memory_space=pltpu.SMEM),
            pl.BlockSpec((tb, SUBLANES, 128), lambda i: (i, 0, 0)),
            pl.BlockSpec((tb, PEER_HK), lambda i: (i, 0)),
            pl.BlockSpec((SUBLANES * PEER_HK, PEER_HK), lambda i: (0, 0)),
            pl.BlockSpec((PEER_ROW_WORDS * PEER_EXPERTS, 128), lambda i: (0, 0), pipeline_mode=pl.Buffered(1)),
        ],
        out_specs=pl.BlockSpec((tb, PEER_HK), lambda i: (i, 0)),
        out_shape=jax.ShapeDtypeStruct((t, PEER_HK), F32),
        scratch_shapes=[pltpu.VMEM((PEER_GROUP, PEER_ROW_WORDS * PEER_HK, 128), jnp.int32),
                        pltpu.VMEM((PEER_GROUP, PEER_ROW_WORDS * PEER_HK, 128), jnp.int32),
                        pltpu.VMEM((tb, SUBLANES * PEER_HK), F32)],
        compiler_params=pltpu.CompilerParams(dimension_semantics=("arbitrary",), vmem_limit_bytes=VMEM_LIMIT_BIG),
        name="peer_hidden",
    )(idx,x8, gates, sel, tab)


def _peer_out_kernel(idx_ref, w_ref, exp_ref, tab_ref, o_ref, stage_a, stage_b, wexp_scr):
    tb = w_ref.shape[0]
    diag = _diag_mask()
    wh, wl = _split_bf16(w_ref[...])
    ex = exp_ref[...]
    wexp_scr[...] = jnp.dot(wh, ex, preferred_element_type=F32) + jnp.dot(wl, ex, preferred_element_type=F32)

    def consume(t, rows_ref):
        v = pltpu.bitcast(rows_ref[...], BF16)
        w8 = jnp.where(diag, jnp.broadcast_to(wexp_scr[pl.ds(t, 1), :], (SUBLANES, SUBLANES * PEER_HK)), 0.0)
        w8h, w8l = _split_bf16(w8)
        w16 = jnp.concatenate([w8h, w8l], axis=0)
        out = jnp.dot(w16, v, preferred_element_type=F32)
        o_ref[t] = out[0:SUBLANES] + out[SUBLANES:2 * SUBLANES]

    _gather_compute_pipeline(tb // PEER_GROUP, idx_ref, tab_ref, consume, stage_a, stage_b)


def peer_out(idx, w, expand, tab, tb):
    t = idx.shape[0]
    return pl.pallas_call(
        _peer_out_kernel,
        grid=(t // tb,),
        in_specs=[
            pl.BlockSpec((tb, PEER_HK), lambda i: (i, 0), memory_space=pltpu.SMEM),
            pl.BlockSpec((tb, PEER_HK), lambda i: (i, 0)),
            pl.BlockSpec((PEER_HK, SUBLANES * PEER_HK), lambda i: (0, 0)),
            pl.BlockSpec((PEER_ROW_WORDS * PEER_EXPERTS, 128), lambda i: (0, 0), pipeline_mode=pl.Buffered(1)),
        ],
        out_specs=pl.BlockSpec((tb, SUBLANES, 128), lambda i: (i, 0, 0)),
        out_shape=jax.ShapeDtypeStruct((t, SUBLANES, 128), F32),
        scratch_shapes=[pltpu.VMEM((PEER_GROUP, PEER_ROW_WORDS * PEER_HK, 128), jnp.int32),
                        pltpu.VMEM((PEER_GROUP, PEER_ROW_WORDS * PEER_HK, 128), jnp.int32),
                        pltpu.VMEM((tb, SUBLANES * PEER_HK), F32)],
        compiler_params=pltpu.CompilerParams(dimension_semantics=("arbitrary",), vmem_limit_bytes=VMEM_LIMIT_BIG),
        name="peer_out",
    )(idx,w, expand, tab)


def _ple_kernel(h_ref, e8_ref, p_ref, nw_ref, wg_ref, wp_ref, fw_ref, o_ref, *, final_norm):
    h = h_ref[...] + jnp.concatenate([e8_ref[:, r, :] for r in range(SUBLANES)], axis=1)
    gate = _sigmoid(jnp.dot(_rms(h, nw_ref[...]).astype(BF16), wg_ref[...], preferred_element_type=F32))
    out = h + jnp.dot(p_ref[...].astype(BF16), wp_ref[...], preferred_element_type=F32) * gate
    if final_norm:
        out = _rms(out, fw_ref[...])
    o_ref[...] = out


def ple(h, e8, p, nw, wg, wp, fw, tm, final_norm):
    t, d = h.shape
    pd = p.shape[1]
    return pl.pallas_call(
        functools.partial(_ple_kernel, final_norm=final_norm),
        grid=(t // tm,),
        in_specs=[
            pl.BlockSpec((tm, d), lambda i: (i, 0)),
            pl.BlockSpec((tm, SUBLANES, d // SUBLANES), lambda i: (i, 0, 0)),
            pl.BlockSpec((tm, pd), lambda i: (i, 0)),
            pl.BlockSpec((1, d), lambda i: (0, 0)),
            pl.BlockSpec((d, d), lambda i: (0, 0)),
            pl.BlockSpec((pd, d), lambda i: (0, 0)),
            pl.BlockSpec((1, d), lambda i: (0, 0)),
        ],
        out_specs=pl.BlockSpec((tm, d), lambda i: (i, 0)),
        out_shape=jax.ShapeDtypeStruct((t, d), F32),
        compiler_params=pltpu.CompilerParams(dimension_semantics=("parallel",), vmem_limit_bytes=VMEM_LIMIT),
        name="ple",
    )(h, e8, p, nw, wg, wp, fw)


def _pack_table(tab):
    e = tab.shape[0]
    b = lax.bitcast_convert_type(tab.astype(BF16).reshape(e, PEER_ROW_WORDS, 2, 128), jnp.uint16).astype(jnp.uint32)
    return lax.bitcast_convert_type(b[:, :, 0, :] | (b[:, :, 1, :] << 16), jnp.int32).reshape(PEER_ROW_WORDS * e, 128)


def _even_in_perm():
    off = np.cumsum([0, 256, 256, 512, 512, 16, 256, 256, 512, 512])
    qa, ka, va, ga, ra, qb, kb, vb, gb = [np.arange(off[i], off[i + 1]) for i in range(9)]
    half = RET_DK // 2
    rot = np.concatenate([np.concatenate([np.arange(h * RET_DK, h * RET_DK + half) for h in range(RET_HEADS)]),
                          np.concatenate([np.arange(h * RET_DK + half, (h + 1) * RET_DK) for h in range(RET_HEADS)])])
    return np.concatenate([qa, ka, qb[rot], kb[rot], va, ga, vb, gb, ra])


def _retention_tables():
    lg = np.log(1.0 - 2.0 ** (-5.0 - np.arange(RET_HEADS, dtype=np.float64)))
    idx = np.arange(CHUNK, dtype=np.float64)
    diff = idx[:, None] - idx[None, :]
    dmask = np.where(diff >= 0, np.exp(lg[:, None, None] * np.maximum(diff, 0.0)), 0.0)
    qdec = np.exp(lg[:, None] * (idx + 1.0))
    kdec = np.exp(lg[:, None] * (CHUNK - 1.0 - idx))
    cdec = np.exp(lg * CHUNK)
    return (jnp.asarray(dmask, F32),
            jnp.asarray(np.broadcast_to(kdec[:, :, None], (RET_HEADS, CHUNK, 128)), F32),
            jnp.asarray(np.broadcast_to(qdec[:, :, None], (RET_HEADS, CHUNK, 128)), F32),
            jnp.asarray(np.broadcast_to(cdec[:, None, None], (RET_HEADS, 1, 256)), F32))


def _peer_layer(h, nw, w_q, sub_keys, u_tab, v_tab, tm_route, tb):
    keys = sub_keys.reshape(2 * PEER_HEADS, PEER_NKEYS, -1)
    xn8, experts, gates = peer_route(h, nw, w_q.T, keys, tm_route)
    kk = np.arange(SUBLANES * PEER_HK) // SUBLANES
    sel = jnp.asarray(kk[:, None] == np.arange(PEER_HK)[None, :], BF16)
    w = peer_hidden(experts, xn8, gates, sel, _pack_table(u_tab), tb)
    return peer_out(experts, w, sel.T, _pack_table(v_tab), tb)


def _row(v):
    return v.reshape(1, -1).astype(F32)


def _layer0_mixer(h, w, bsz, seq):
    t = h.shape[0]
    w_in = w["ev_w_in"][0]
    w_in0 = jnp.pad(w_in[:, _even_in_perm()], ((0, 0), (0, EVEN_COLS - w_in.shape[1]))).astype(BF16)
    proj = norm_matmul(h, _row(w["norm_mix_w"][0]), w_in0, min(256, t))
    half = RET_DK // 2
    freqs = ROPE_BASE ** (-np.arange(half, dtype=np.float32) / half)
    dmask, kdec, qdec, cdec = _retention_tables()
    consts = (
        jnp.asarray(np.tile(freqs, RET_HEADS)[None, :], F32),
        jnp.pad(w["ev_gla_w_up"][0], ((0, 128 - GLA_GATE_RANK), (0, 0))).astype(BF16),
        _row(w["ev_gla_b_up"][0]), _row(w["ev_gla_norm_w"][0]), _row(w["ev_ret_norm_w"][0]),
        w["ev_w_out"][0].astype(BF16),
        jnp.asarray(np.tril(np.ones((CHUNK, CHUNK))), BF16),
        dmask, kdec, qdec, cdec,
    )
    pos = w["positions"].astype(F32).reshape(seq, 1)
    return even_mixer(h, proj, pos, consts, bsz, seq, min(256, seq))


def _layer1_mixer(h, w, bsz, seq):
    t = h.shape[0]
    u = norm_matmul(h, _row(w["norm_mix_w"][1]), w["od_w_in"][0].astype(BF16), min(512, t))
    pw_re, pw_im, bb_re, bb_im = s5_params(w["od_s5_a_re"][0], w["od_s5_a_im"][0], w["od_s5_log_dt"][0].reshape(-1, 1),
                                           w["od_s5_b_re"][0].transpose(0, 2, 1), w["od_s5_b_im"][0].transpose(0, 2, 1))
    eye = jnp.eye(S5_GROUPS, dtype=F32)
    blockdiag = lambda m: (m[:, :, None, :] * eye[:, None, :, None]).reshape(m.shape[0] * m.shape[1], -1)
    consts = (
        w["od_pool_w"][0].astype(BF16), _row(w["od_pool_scale"][0]),
        blockdiag(bb_re).astype(BF16), blockdiag(bb_im).astype(BF16),
        blockdiag(w["od_s5_c_re"][0]).T.astype(BF16), blockdiag(w["od_s5_c_im"][0]).T.astype(BF16),
        pw_re.reshape(SUBLANES, S5_STATE), pw_im.reshape(SUBLANES, S5_STATE),
        _row(w["od_s5_d"][0]), w["od_s5_w_glu"][0].astype(BF16), _row(w["od_s5_b_glu"][0]), w["od_w_out"][0].astype(BF16),
    )
    return odd_mixer(h, u, consts, bsz, seq, min(256, seq))


def kernel(x, p, positions, norm_mix_w, norm_ffn_w, norm_ple_w, final_norm_w, ev_w_in, ev_gla_w_up, ev_gla_b_up, ev_gla_norm_w, ev_ret_norm_w, ev_w_out, od_w_in, od_pool_w, od_pool_scale, od_s5_a_re, od_s5_a_im, od_s5_log_dt, od_s5_b_re, od_s5_b_im, od_s5_c_re, od_s5_c_im, od_s5_d, od_s5_w_glu, od_s5_b_glu, od_w_out, peer_w_q, peer_sub_keys, peer_u, peer_v, ple_w_proj, ple_w_gate):
    w = dict(locals())
    bsz, seq, d = x.shape
    t = bsz * seq
    tm_tok = min(512, t)
    tm_route = min(256, t)
    tb = min(128, t)
    h = x.reshape(t, d)
    for i, mixer in enumerate((_layer0_mixer, _layer1_mixer)):
        h = mixer(h, w, bsz, seq)
        e8 = _peer_layer(h, _row(norm_ffn_w[i]), peer_w_q[i], peer_sub_keys[i], peer_u[i], peer_v[i], tm_route, tb)
        h = ple(h, e8, p[i].reshape(t, -1), _row(norm_ple_w[i]), ple_w_gate[i].astype(BF16), ple_w_proj[i].astype(BF16),
                _row(final_norm_w), tm_tok, i == 1)
    return h.reshape(bsz, seq, d)
```

```python
import functools
import math

import numpy as np
import jax
import jax.numpy as jnp
from jax import lax
from jax.experimental import pallas as pl
from jax.experimental.pallas import tpu as pltpu

F32 = jnp.float32
BF16 = jnp.bfloat16

D_MODEL = 1024
NORM_EPS = 1e-6
CHUNK = 64
GLA_HEADS = 4
GLA_DK = 64
GLA_DV = 128
GLA_GATE_RANK = 16
GLA_GATE_NORM = 16.0
RET_HEADS = 4
RET_DK = 64
RET_DV = 128
ROPE_BASE = 10000.0
POOL_WINDOWS = (2, 4, 8, 16)
POOL_WIDTH = 512
POOL_GROUP_WIDTH = 128
S5_H = 16
S5_P = 64
S5_GROUPS = 32
S5_WIDTH = 512
S5_STATE = S5_GROUPS * S5_P
PEER_HEADS = 8
PEER_NKEYS = 128
PEER_TOPK = 16
PEER_HK = PEER_HEADS * PEER_TOPK
PEER_EXPERTS = PEER_NKEYS * PEER_NKEYS
EVEN_COLS = 3200

VMEM_LIMIT_BIG = 56 * 1024 * 1024
VMEM_LIMIT = 48 * 1024 * 1024
SUBLANES = 8

NT_DIMS = (((1,), (1,)), ((), ()))


def _split_bf16(x):
    hi = x.astype(BF16)
    lo = (x - hi.astype(F32)).astype(BF16)
    return hi, lo


def _rms(x, w):
    return x * lax.rsqrt(jnp.mean(x * x, axis=-1, keepdims=True) + NORM_EPS) * w


def _sigmoid(x):
    return 1.0 / (1.0 + jnp.exp(-x))


def _gelu(x):
    return 0.5 * x * (1.0 + lax.erf(x * (2.0 ** -0.5)))


def _log_sigmoid(z):
    return jnp.minimum(z, 0.0) - jnp.log1p(jnp.exp(-jnp.abs(z)))


def _norm_mm_kernel(h_ref, nw_ref, w_ref, o_ref):
    xn = _rms(h_ref[...], nw_ref[...])
    o_ref[...] = jnp.dot(xn.astype(BF16), w_ref[...], preferred_element_type=F32)


def norm_matmul(h, nw, w, tm):
    t, d = h.shape
    n = w.shape[1]
    return pl.pallas_call(
        _norm_mm_kernel,
        grid=(t // tm,),
        in_specs=[
            pl.BlockSpec((tm, d), lambda i: (i, 0)),
            pl.BlockSpec((1, d), lambda i: (0, 0)),
            pl.BlockSpec((d, n), lambda i: (0, 0)),
        ],
        out_specs=pl.BlockSpec((tm, n), lambda i: (i, 0)),
        out_shape=jax.ShapeDtypeStruct((t, n), F32),
        compiler_params=pltpu.CompilerParams(dimension_semantics=("parallel",), vmem_limit_bytes=VMEM_LIMIT),
        name="norm_matmul",
    )(h, nw, w)


def _even_mixer_kernel(h_ref, qk_ref, va_ref, ga_ref, vb_ref, gb_ref, ra_ref, pos_ref,
                       freq_ref, wup_ref, bup_ref, gnw_ref, rnw_ref, wout_ref, tril_ref,
                       dmask_ref, kdec_ref, qdec_ref, cdec_ref,
                       o_ref, gstate, rstate, y_scr):
    tm = h_ref.shape[0]

    @pl.when(pl.program_id(1) == 0)
    def _():
        gstate[...] = jnp.zeros_like(gstate)
        rstate[...] = jnp.zeros_like(rstate)

    lane = lax.broadcasted_iota(jnp.int32, (1, 256), 1)
    gla_masks = [((lane >= GLA_DK * h) & (lane < GLA_DK * (h + 1))).astype(F32) for h in range(GLA_HEADS)]
    half = RET_DK // 2
    ret_masks = [(((lane >= half * h) & (lane < half * (h + 1)))
                  | ((lane >= 128 + half * h) & (lane < 128 + half * (h + 1)))).astype(F32)
                 for h in range(RET_HEADS)]
    ri = lax.broadcasted_iota(jnp.int32, (CHUNK, CHUNK), 0)
    ci = lax.broadcasted_iota(jnp.int32, (CHUNK, CHUNK), 1)
    causal = ri >= ci
    tril = tril_ref[...]
    freqs = freq_ref[...]
    wup = wup_ref[...]
    bup = bup_ref[...]

    def chunk_body(c, carry):
        rows = pl.ds(pl.multiple_of(c * CHUNK, CHUNK), CHUNK)
        qk = qk_ref[rows, :]
        qa, ka, qb, kb = qk[:, 0:256], qk[:, 256:512], qk[:, 512:768], qk[:, 768:1024]

        z = jnp.dot(ra_ref[rows, :].astype(BF16), wup, preferred_element_type=F32) + bup
        la = _log_sigmoid(z) * (1.0 / GLA_GATE_NORM)
        lah, lal = _split_bf16(la)
        g = (jnp.dot(tril, lah, preferred_element_type=F32)
             + jnp.dot(tril, lal, preferred_element_type=F32))
        g_last = g[CHUNK - 1:CHUNK, :]
        ref = 0.5 * g_last
        qs = qa * (GLA_DK ** -0.5)
        qt = qs * jnp.exp(g - ref)
        kt = (ka * jnp.exp(ref - g)).astype(BF16)
        kd = ka * jnp.exp(g_last - g)
        qg = qs * jnp.exp(g)
        decay = jnp.exp(g_last)
        for h in range(GLA_HEADS):
            m = gla_masks[h]
            s = lax.dot_general((qt * m).astype(BF16), kt, NT_DIMS, preferred_element_type=F32)
            s = jnp.where(causal, s, 0.0)
            v = va_ref[rows, GLA_DV * h:GLA_DV * (h + 1)]
            st = gstate[h]
            o = jnp.dot(s.astype(BF16), v.astype(BF16), preferred_element_type=F32)
            o = o + lax.dot_general((qg * m).astype(BF16), st.astype(BF16), NT_DIMS, preferred_element_type=F32)
            gstate[h] = st * decay + jnp.dot(v.T.astype(BF16), (kd * m).astype(BF16), preferred_element_type=F32)
            y_scr[rows, GLA_DV * h:GLA_DV * (h + 1)] = o

        ang = pos_ref[rows, :] * freqs
        cs, sn = jnp.cos(ang), jnp.sin(ang)
        q1, q2 = qb[:, 0:128], qb[:, 128:256]
        k1, k2 = kb[:, 0:128], kb[:, 128:256]
        qr = jnp.concatenate([q1 * cs - q2 * sn, q2 * cs + q1 * sn], axis=1)
        kr = jnp.concatenate([k1 * cs - k2 * sn, k2 * cs + k1 * sn], axis=1) * (RET_DK ** -0.5)
        krb = kr.astype(BF16)
        for h in range(RET_HEADS):
            m = ret_masks[h]
            qm = (qr * m).astype(BF16)
            s = lax.dot_general(qm, krb, NT_DIMS, preferred_element_type=F32) * dmask_ref[h]
            v = vb_ref[rows, RET_DV * h:RET_DV * (h + 1)]
            st = rstate[h]
            o = jnp.dot(s.astype(BF16), v.astype(BF16), preferred_element_type=F32)
            o = o + lax.dot_general(qm, st.astype(BF16), NT_DIMS, preferred_element_type=F32) * qdec_ref[h]
            rstate[h] = st * cdec_ref[h] + jnp.dot((v * kdec_ref[h]).T.astype(BF16), (kr * m).astype(BF16),
                                                    preferred_element_type=F32)
            y_scr[rows, 512 + RET_DV * h:512 + RET_DV * (h + 1)] = o
        return carry

    lax.fori_loop(0, tm // CHUNK, chunk_body, 0)

    pieces = []
    gnw = gnw_ref[...]
    for h in range(GLA_HEADS):
        o = y_scr[:, GLA_DV * h:GLA_DV * (h + 1)]
        gt = ga_ref[:, GLA_DV * h:GLA_DV * (h + 1)]
        pieces.append(_rms(o, gnw) * (gt * _sigmoid(gt)))
    for h in range(RET_HEADS):
        o = y_scr[:, 512 + RET_DV * h:512 + RET_DV * (h + 1)]
        gt = gb_ref[:, RET_DV * h:RET_DV * (h + 1)]
        mu = jnp.mean(o, axis=-1, keepdims=True)
        oc = o - mu
        var = jnp.mean(oc * oc, axis=-1, keepdims=True)
        nrm = oc * lax.rsqrt(var + NORM_EPS) * rnw_ref[:, RET_DV * h:RET_DV * (h + 1)]
        pieces.append(nrm * (gt * _sigmoid(gt)))
    y = jnp.concatenate(pieces, axis=1).astype(BF16)
    o_ref[...] = h_ref[...] + jnp.dot(y, wout_ref[...], preferred_element_type=F32)


def even_mixer(h, proj, pos, consts, bsz, seq, tm):
    nj = seq // tm
    tok = lambda b, j: (b * nj + j, 0)
    col = lambda cb: (lambda b, j: (b * nj + j, cb))
    full2 = lambda b, j: (0, 0)
    full3 = lambda b, j: (0, 0, 0)
    (freqs, wup, bup, gnw, rnw, wout, tril, dmask, kdec, qdec, cdec) = consts
    return pl.pallas_call(
        _even_mixer_kernel,
        grid=(bsz, nj),
        in_specs=[
            pl.BlockSpec((tm, D_MODEL), tok),
            pl.BlockSpec((tm, 1024), col(0)),
            pl.BlockSpec((tm, 512), col(2)),
            pl.BlockSpec((tm, 512), col(3)),
            pl.BlockSpec((tm, 512), col(4)),
            pl.BlockSpec((tm, 512), col(5)),
            pl.BlockSpec((tm, 128), col(24)),
            pl.BlockSpec((tm, 1), lambda b, j: (j, 0)),
            pl.BlockSpec((1, 128), full2),
            pl.BlockSpec((128, 256), full2),
            pl.BlockSpec((1, 256), full2),
            pl.BlockSpec((1, 128), full2),
            pl.BlockSpec((1, 512), full2),
            pl.BlockSpec((1024, D_MODEL), full2),
            pl.BlockSpec((CHUNK, CHUNK), full2),
            pl.BlockSpec((RET_HEADS, CHUNK, CHUNK), full3),
            pl.BlockSpec((RET_HEADS, CHUNK, 128), full3),
            pl.BlockSpec((RET_HEADS, CHUNK, 128), full3),
            pl.BlockSpec((RET_HEADS, 1, 256), full3),
        ],
        out_specs=pl.BlockSpec((tm, D_MODEL), tok),
        out_shape=jax.ShapeDtypeStruct(h.shape, F32),
        scratch_shapes=[
            pltpu.VMEM((GLA_HEADS, GLA_DV, 256), F32),
            pltpu.VMEM((RET_HEADS, RET_DV, 256), F32),
            pltpu.VMEM((tm, 1024), F32),
        ],
        compiler_params=pltpu.CompilerParams(dimension_semantics=("arbitrary", "arbitrary"),
                                             vmem_limit_bytes=VMEM_LIMIT),
        name="even_mixer",
    )(h, proj, proj, proj, proj, proj, proj, pos, freqs, wup, bup, gnw, rnw, wout, tril, dmask, kdec, qdec, cdec)


def _s5_param_kernel(are_ref, aim_ref, ldt_ref, bre_ref, bim_ref,
                     pw_re_ref, pw_im_ref, bbre_ref, bbim_ref):
    a_re = are_ref[...]
    a_im = aim_ref[...]
    dt = jnp.exp(ldt_ref[...])
    for r in range(SUBLANES):
        mag = jnp.exp(a_re * dt * (r + 1.0))
        pw_re_ref[r] = mag * jnp.cos(a_im * dt * (r + 1.0))
        pw_im_ref[r] = mag * jnp.sin(a_im * dt * (r + 1.0))
    mag = jnp.exp(a_re * dt)
    abar_re, abar_im = mag * jnp.cos(a_im * dt), mag * jnp.sin(a_im * dt)
    den = a_re * a_re + a_im * a_im
    nr, ni = abar_re - 1.0, abar_im
    coef_re = (nr * a_re + ni * a_im) / den
    coef_im = (ni * a_re - nr * a_im) / den
    b_re = bre_ref[...]
    b_im = bim_ref[...]
    c_re = jnp.concatenate([coef_re] * S5_H, axis=1)
    c_im = jnp.concatenate([coef_im] * S5_H, axis=1)
    bbre_ref[...] = c_re * b_re - c_im * b_im
    bbim_ref[...] = c_re * b_im + c_im * b_re


def s5_params(a_re, a_im, log_dt, b_re_t, b_im_t):
    g, p = a_re.shape
    hh = b_re_t.shape[1]
    pw_re, pw_im, bb_re, bb_im = pl.pallas_call(
        _s5_param_kernel,
        out_shape=[jax.ShapeDtypeStruct((SUBLANES, g, p), F32), jax.ShapeDtypeStruct((SUBLANES, g, p), F32),
                   jax.ShapeDtypeStruct((g, hh * p), F32), jax.ShapeDtypeStruct((g, hh * p), F32)],
        name="s5_params",
    )(a_re, a_im, log_dt, b_re_t.reshape(g, hh * p), b_im_t.reshape(g, hh * p))
    return pw_re, pw_im, bb_re.reshape(g, hh, p), bb_im.reshape(g, hh, p)


def _odd_mixer_kernel(h_ref, u_ref, poolw_ref, pscale_ref, wbre_ref, wbim_ref, wcre_ref, wcim_ref,
                      pwre_ref, pwim_ref, dskip_ref, wglu_ref, bglu_ref, wout_ref,
                      o_ref, tail_scr, car_re, car_im, xre_scr, xim_scr):
    tm = h_ref.shape[0]
    j = pl.program_id(1)
    halo = POOL_WINDOWS[-1]

    @pl.when(j == 0)
    def _():
        tail_scr[...] = jnp.zeros_like(tail_scr)
        car_re[...] = jnp.zeros_like(car_re)
        car_im[...] = jnp.zeros_like(car_im)

    uc = u_ref[:, 0:POOL_WIDTH]
    ud = u_ref[:, POOL_WIDTH:POOL_WIDTH + S5_WIDTH]

    ext = jnp.concatenate([tail_scr[...], uc], axis=0)
    tail_scr[...] = uc[tm - halo:tm, :]
    pos = (j * tm + lax.broadcasted_iota(jnp.int32, (tm, 1), 0)).astype(F32)
    mixed = []
    for gi, win in enumerate(POOL_WINDOWS):
        a = ext[:, POOL_GROUP_WIDTH * gi:POOL_GROUP_WIDTH * (gi + 1)]
        n = tm + halo
        step = 1
        end = 0
        while step < win:
            a = a[step:n, :] + a[0:n - step, :]
            n -= step
            end += step
            step *= 2
        wsum = a[halo - end:halo - end + tm, :]
        cnt = jnp.minimum(pos + 1.0, float(win))
        pooled = wsum / cnt - uc[:, POOL_GROUP_WIDTH * gi:POOL_GROUP_WIDTH * (gi + 1)]
        mixed.append(jnp.dot(pooled.astype(BF16), poolw_ref[gi], preferred_element_type=F32))
    y_c = jnp.concatenate(mixed, axis=1) * pscale_ref[...]

    udb = ud.astype(BF16)
    hw, hs = S5_WIDTH // 2, S5_STATE // 2
    for hf in range(2):
        cols = slice(hs * hf, hs * (hf + 1))
        uh = udb[:, hw * hf:hw * (hf + 1)]
        xre_scr[:, cols] = jnp.dot(uh, wbre_ref[hw * hf:hw * (hf + 1), cols], preferred_element_type=F32)
        xim_scr[:, cols] = jnp.dot(uh, wbim_ref[hw * hf:hw * (hf + 1), cols], preferred_element_type=F32)
    rowi = lax.broadcasted_iota(jnp.int32, (SUBLANES, S5_STATE), 0)
    pw_re = pwre_ref[...]
    pw_im = pwim_ref[...]
    step_pw = [(jnp.where(rowi >= d, pw_re[d - 1:d, :], 0.0), jnp.where(rowi >= d, pw_im[d - 1:d, :], 0.0))
               for d in (1, 2, 4)]

    def slab(s, carry):
        cr, ci = carry
        rows = pl.ds(pl.multiple_of(s * SUBLANES, SUBLANES), SUBLANES)
        xr = xre_scr[rows, :]
        xi = xim_scr[rows, :]
        for dsh, (pr, pi) in zip((1, 2, 4), step_pw):
            sr = pltpu.roll(xr, dsh, axis=0)
            si = pltpu.roll(xi, dsh, axis=0)
            xr, xi = xr + (pr * sr - pi * si), xi + (pr * si + pi * sr)
        xr, xi = xr + (pw_re * cr - pw_im * ci), xi + (pw_re * ci + pw_im * cr)
        xre_scr[rows, :] = xr
        xim_scr[rows, :] = xi
        return xr[SUBLANES - 1:SUBLANES, :], xi[SUBLANES - 1:SUBLANES, :]

    cr, ci = lax.fori_loop(0, tm // SUBLANES, slab, (car_re[...], car_im[...]))
    car_re[...] = cr
    car_im[...] = ci

    yh = []
    for hf in range(2):
        rws = slice(hs * hf, hs * (hf + 1))
        cls = slice(hw * hf, hw * (hf + 1))
        yh.append(jnp.dot(xre_scr[:, rws].astype(BF16), wcre_ref[rws, cls], preferred_element_type=F32)
                  - jnp.dot(xim_scr[:, rws].astype(BF16), wcim_ref[rws, cls], preferred_element_type=F32))
    y = jnp.concatenate(yh, axis=1) + dskip_ref[...] * ud
    z = _gelu(y)
    y_d = z * _sigmoid(jnp.dot(z.astype(BF16), wglu_ref[...], preferred_element_type=F32) + bglu_ref[...])

    ycat = jnp.concatenate([y_c, y_d], axis=1).astype(BF16)
    o_ref[...] = h_ref[...] + jnp.dot(ycat, wout_ref[...], preferred_element_type=F32)


def odd_mixer(h, u, consts, bsz, seq, tm):
    nj = seq // tm
    tok = lambda b, j: (b * nj + j, 0)
    full2 = lambda b, j: (0, 0)
    full3 = lambda b, j: (0, 0, 0)
    (poolw, pscale, wbre, wbim, wcre, wcim, pwre, pwim, dskip, wglu, bglu, wout) = consts
    return pl.pallas_call(
        _odd_mixer_kernel,
        grid=(bsz, nj),
        in_specs=[
            pl.BlockSpec((tm, D_MODEL), tok),
            pl.BlockSpec((tm, 1024), tok),
            pl.BlockSpec((4, 128, 128), full3),
            pl.BlockSpec((1, POOL_WIDTH), full2),
            pl.BlockSpec((S5_WIDTH, S5_STATE), full2),
            pl.BlockSpec((S5_WIDTH, S5_STATE), full2),
            pl.BlockSpec((S5_STATE, S5_WIDTH), full2),
            pl.BlockSpec((S5_STATE, S5_WIDTH), full2),
            pl.BlockSpec((SUBLANES, S5_STATE), full2),
            pl.BlockSpec((SUBLANES, S5_STATE), full2),
            pl.BlockSpec((1, S5_WIDTH), full2),
            pl.BlockSpec((S5_WIDTH, S5_WIDTH), full2),
            pl.BlockSpec((1, S5_WIDTH), full2),
            pl.BlockSpec((1024, D_MODEL), full2),
        ],
        out_specs=pl.BlockSpec((tm, D_MODEL), tok),
        out_shape=jax.ShapeDtypeStruct(h.shape, F32),
        scratch_shapes=[
            pltpu.VMEM((POOL_WINDOWS[-1], POOL_WIDTH), F32),
            pltpu.VMEM((1, S5_STATE), F32),
            pltpu.VMEM((1, S5_STATE), F32),
            pltpu.VMEM((tm, S5_STATE), F32),
            pltpu.VMEM((tm, S5_STATE), F32),
        ],
        compiler_params=pltpu.CompilerParams(dimension_semantics=("arbitrary", "arbitrary"),
                                             vmem_limit_bytes=VMEM_LIMIT),
        name="odd_mixer",
    )(h, u, poolw, pscale, wbre, wbim, wcre, wcim, pwre, pwim, dskip, wglu, bglu, wout)


def _top16_rows(s, ids, id_bound):
    vals, idxs = [], []
    for _ in range(PEER_TOPK):
        m = jnp.max(s, axis=0, keepdims=True)
        am = jnp.min(jnp.where(s == m, ids, float(id_bound)), axis=0, keepdims=True)
        vals.append(m)
        idxs.append(am)
        s = jnp.where(ids == am, -jnp.inf, s)
    return jnp.concatenate(vals, axis=0), jnp.concatenate(idxs, axis=0)


_PAIR_BLOCKS = (("b", 0, 0), ("b", 1, 0), ("b", 2, 0), ("b", 3, 0), ("b", 4, 0),
                ("a", 0, 8), ("a", 0, 0), ("a", 1, 0), ("b", 0, 8))
_PAIR_ID_BOUND = 4 * PEER_TOPK * PEER_TOPK


def _pair_block_ids(tm):
    r = lax.broadcasted_iota(jnp.int32, (SUBLANES, tm), 0).astype(F32)
    seen = set()
    out = []
    for side, fixed, start in _PAIR_BLOCKS:
        ids = jnp.zeros((SUBLANES, tm), F32)
        for q in range(SUBLANES):
            i, j = (fixed, start + q) if side == "a" else (start + q, fixed)
            ok = (i + 1) * (j + 1) <= PEER_TOPK and (i, j) not in seen
            seen.add((i, j))
            ids = jnp.where(r == q, float(i * PEER_TOPK + j if ok else _PAIR_ID_BOUND + len(seen)), ids)
        out.append(ids)
    assert len({p for p in seen if (p[0] + 1) * (p[1] + 1) <= PEER_TOPK}) == 50
    return jnp.concatenate(out, axis=0)


def _pair_block_sums(av, bv):
    out = []
    for side, fixed, start in _PAIR_BLOCKS:
        if side == "a":
            out.append(av[fixed:fixed + 1, :] + bv[start:start + SUBLANES, :])
        else:
            out.append(av[start:start + SUBLANES, :] + bv[fixed:fixed + 1, :])
    return jnp.concatenate(out, axis=0)


def _take16(table, sel):
    out = jnp.zeros(sel.shape, table.dtype)
    for i in range(PEER_TOPK):
        out = jnp.where(sel == i, table[i:i + 1, :], out)
    return out


def _dot3(ah, al, bh, bl, dims):
    return (lax.dot_general(ah, bh, dims, preferred_element_type=F32)
            + lax.dot_general(al, bh, dims, preferred_element_type=F32)
            + lax.dot_general(ah, bl, dims, preferred_element_type=F32))


def _peer_route_kernel(h_ref, nw_ref, wqh_ref, wql_ref, kh_ref, kl_ref, xn_ref, exp_ref, gate_ref,
                       qt_scr, et_scr, gt_scr):
    xn = _rms(h_ref[...], nw_ref[...])
    for r in range(SUBLANES):
        xn_ref[:, r, :] = xn[:, 128 * r:128 * (r + 1)]
    xh, xl = _split_bf16(xn)
    qt_scr[...] = _dot3(wqh_ref[...], wql_ref[...], xh, xl, NT_DIMS)
    mm = (((1,), (0,)), ((), ()))
    tm = h_ref.shape[0]
    key_ids = lax.broadcasted_iota(jnp.int32, (PEER_NKEYS, tm), 0).astype(F32)
    pair_ids = _pair_block_ids(tm)
    pair_ok = pair_ids < float(_PAIR_ID_BOUND)

    def head(hd, carry):
        ra = pl.ds(pl.multiple_of(hd * 256, 256), 128)
        rb = pl.ds(pl.multiple_of(hd * 256 + 128, 128), 128)
        qah, qal = _split_bf16(qt_scr[ra, :])
        qbh, qbl = _split_bf16(qt_scr[rb, :])
        sa = _dot3(kh_ref[2 * hd], kl_ref[2 * hd], qah, qal, mm)
        sb = _dot3(kh_ref[2 * hd + 1], kl_ref[2 * hd + 1], qbh, qbl, mm)
        av, ai = _top16_rows(sa, key_ids, PEER_NKEYS)
        bv, bi = _top16_rows(sb, key_ids, PEER_NKEYS)
        cand = jnp.where(pair_ok, _pair_block_sums(av, bv), -jnp.inf)
        cv, flat = _top16_rows(cand, pair_ids, _PAIR_ID_BOUND)
        flat = flat.astype(jnp.int32)
        e_a = _take16(ai, flat >> 4)
        e_b = _take16(bi, flat & (PEER_TOPK - 1))
        ex = jnp.exp(cv - cv[0:1, :])
        rows = pl.ds(pl.multiple_of(hd * PEER_TOPK, PEER_TOPK), PEER_TOPK)
        et_scr[rows, :] = (e_a * float(PEER_NKEYS) + e_b) * float(PEER_ROW_WORDS)
        gt_scr[rows, :] = ex / jnp.sum(ex, axis=0, keepdims=True)
        return carry

    def head_pair(i, carry):
        head(2 * i, carry)
        return head(2 * i + 1, carry)

    lax.fori_loop(0, PEER_HEADS // 2, head_pair, 0)
    exp_ref[...] = et_scr[...].T.astype(jnp.int32)
    gate_ref[...] = gt_scr[...].T


def peer_route(h, nw, wqt, keys, tm):
    t, d = h.shape
    nq = wqt.shape[0]
    wqh, wql = _split_bf16(wqt)
    kh, kl = _split_bf16(keys)
    return pl.pallas_call(
        _peer_route_kernel,
        grid=(t // tm,),
        in_specs=[
            pl.BlockSpec((tm, d), lambda i: (i, 0)),
            pl.BlockSpec((1, d), lambda i: (0, 0)),
            pl.BlockSpec((nq, d), lambda i: (0, 0)),
            pl.BlockSpec((nq, d), lambda i: (0, 0)),
            pl.BlockSpec((2 * PEER_HEADS, PEER_NKEYS, 128), lambda i: (0, 0, 0)),
            pl.BlockSpec((2 * PEER_HEADS, PEER_NKEYS, 128), lambda i: (0, 0, 0)),
        ],
        out_specs=[
            pl.BlockSpec((tm, SUBLANES, d // SUBLANES), lambda i: (i, 0, 0)),
            pl.BlockSpec((tm, PEER_HK), lambda i: (i, 0)),
            pl.BlockSpec((tm, PEER_HK), lambda i: (i, 0)),
        ],
        out_shape=[jax.ShapeDtypeStruct((t, SUBLANES, d // SUBLANES), F32),
                   jax.ShapeDtypeStruct((t, PEER_HK), jnp.int32), jax.ShapeDtypeStruct((t, PEER_HK), F32)],
        scratch_shapes=[pltpu.VMEM((nq, tm), F32), pltpu.VMEM((PEER_HK, tm), F32), pltpu.VMEM((PEER_HK, tm), F32)],
        compiler_params=pltpu.CompilerParams(dimension_semantics=("parallel",), vmem_limit_bytes=VMEM_LIMIT),
        name="peer_route",
    )(h, nw, wqh, wql, kh, kl)


PEER_GROUP = 8
PEER_ROW_WORDS = 4


def _gather_group(idx_ref, tab_ref, g, stage_ref):
    rows = [idx_ref.at[g * PEER_GROUP + j] for j in range(PEER_GROUP)]
    for k in range(PEER_HK):
        for j in range(PEER_GROUP):
            off = pl.multiple_of(rows[j][k], PEER_ROW_WORDS)
            stage_ref[j, pl.ds(PEER_ROW_WORDS * k, PEER_ROW_WORDS), :] = tab_ref[pl.ds(off, PEER_ROW_WORDS), :]


def _gather_compute_pipeline(n_groups, idx_ref, tab_ref, consume, stage_a, stage_b):
    def compute(g, stage_ref):
        for j in range(PEER_GROUP):
            consume(g * PEER_GROUP + j, stage_ref.at[j])

    _gather_group(idx_ref, tab_ref,0, stage_a)

    def body(i, carry):
        compute(2 * i, stage_a)
        _gather_group(idx_ref, tab_ref, 2 * i + 1, stage_b)
        compute(2 * i + 1, stage_b)
        _gather_group(idx_ref, tab_ref, 2 * i + 2, stage_a)
        return carry

    lax.fori_loop(0, n_groups // 2 - 1, body, 0)
    _gather_group(idx_ref, tab_ref,n_groups - 1, stage_b)
    compute(n_groups - 2, stage_a)
    compute(n_groups - 1, stage_b)


def _diag_mask():
    row = lax.broadcasted_iota(jnp.int32, (SUBLANES, SUBLANES * PEER_HK), 0)
    lane = lax.broadcasted_iota(jnp.int32, (SUBLANES, SUBLANES * PEER_HK), 1)
    return (lane & (SUBLANES - 1)) == row


def _peer_hidden_kernel(idx_ref, x_ref, g_ref, sel_ref, tab_ref, o_ref, stage_a, stage_b, part_scr):
    tb = x_ref.shape[0]
    diag = _diag_mask()

    def consume(t, rows_ref):
        u = pltpu.bitcast(rows_ref[...], BF16)
        xh, xl = _split_bf16(x_ref[t])
        x16 = jnp.concatenate([xh, xl], axis=0)
        out = lax.dot_general(x16, u, NT_DIMS, preferred_element_type=F32)
        o8 = out[0:SUBLANES] + out[SUBLANES:2 * SUBLANES]
        part_scr[pl.ds(t, 1), :] = jnp.sum(jnp.where(diag, o8, 0.0), axis=0, keepdims=True)

    _gather_compute_pipeline(tb // PEER_GROUP, idx_ref, tab_ref, consume, stage_a, stage_b)
    ph, plo = _split_bf16(part_scr[...])
    sel = sel_ref[...]
    hid = jnp.dot(ph, sel, preferred_element_type=F32) + jnp.dot(plo, sel, preferred_element_type=F32)
    o_ref[...] = g_ref[...] * _gelu(hid)


def peer_hidden(idx, x8, gates, sel, tab, tb):
    t = idx.shape[0]
    return pl.pallas_call(
        _peer_hidden_kernel,
        grid=(t // tb,),
        in_specs=[
            pl.BlockSpec((tb, PEER_HK), lambda i: (i, 0), memory_space=pltpu.SMEM),
            pl.BlockSpec((tb, SUBLANES, 128), lambda i: (i, 0, 0)),
            pl.BlockSpec((tb, PEER_HK), lambda i: (i, 0)),
            pl.BlockSpec((SUBLANES * PEER_HK, PEER_HK), lambda i: (0, 0)),
            pl.BlockSpec((PEER_ROW_WORDS * PEER_EXPERTS, 128), lambda i: (0, 0), pipeline_mode=pl.Buffered(1)),
        ],
        out_specs=pl.BlockSpec((tb, PEER_HK), lambda i: (i, 0)),
        out_shape=jax.ShapeDtypeStruct((t, PEER_HK), F32),
        scratch_shapes=[pltpu.VMEM((PEER_GROUP, PEER_ROW_WORDS * PEER_HK, 128), jnp.int32),
                        pltpu.VMEM((PEER_GROUP, PEER_ROW_WORDS * PEER_HK, 128), jnp.int32),
                        pltpu.VMEM((tb, SUBLANES * PEER_HK), F32)],
        compiler_params=pltpu.CompilerParams(dimension_semantics=("arbitrary",), vmem_limit_bytes=VMEM_LIMIT_BIG),
        name="peer_hidden",
    )(idx,x8, gates, sel, tab)


def _peer_out_kernel(idx_ref, w_ref, exp_ref, tab_ref, o_ref, stage_a, stage_b, wexp_scr):
    tb = w_ref.shape[0]
    diag = _diag_mask()
    wh, wl = _split_bf16(w_ref[...])
    ex = exp_ref[...]
    wexp_scr[...] = jnp.dot(wh, ex, preferred_element_type=F32) + jnp.dot(wl, ex, preferred_element_type=F32)

    def consume(t, rows_ref):
        v = pltpu.bitcast(rows_ref[...], BF16)
        w8 = jnp.where(diag, jnp.broadcast_to(wexp_scr[pl.ds(t, 1), :], (SUBLANES, SUBLANES * PEER_HK)), 0.0)
        w8h, w8l = _split_bf16(w8)
        w16 = jnp.concatenate([w8h, w8l], axis=0)
        out = jnp.dot(w16, v, preferred_element_type=F32)
        o_ref[t] = out[0:SUBLANES] + out[SUBLANES:2 * SUBLANES]

    _gather_compute_pipeline(tb // PEER_GROUP, idx_ref, tab_ref, consume, stage_a, stage_b)


def peer_out(idx, w, expand, tab, tb):
    t = idx.shape[0]
    return pl.pallas_call(
        _peer_out_kernel,
        grid=(t // tb,),
        in_specs=[
            pl.BlockSpec((tb, PEER_HK), lambda i: (i, 0), memory_space=pltpu.SMEM),
            pl.BlockSpec((tb, PEER_HK), lambda i: (i, 0)),
            pl.BlockSpec((PEER_HK, SUBLANES * PEER_HK), lambda i: (0, 0)),
            pl.BlockSpec((PEER_ROW_WORDS * PEER_EXPERTS, 128), lambda i: (0, 0), pipeline_mode=pl.Buffered(1)),
        ],
        out_specs=pl.BlockSpec((tb, SUBLANES, 128), lambda i: (i, 0, 0)),
        out_shape=jax.ShapeDtypeStruct((t, SUBLANES, 128), F32),
        scratch_shapes=[pltpu.VMEM((PEER_GROUP, PEER_ROW_WORDS * PEER_HK, 128), jnp.int32),
                        pltpu.VMEM((PEER_GROUP, PEER_ROW_WORDS * PEER_HK, 128), jnp.int32),
                        pltpu.VMEM((tb, SUBLANES * PEER_HK), F32)],
        compiler_params=pltpu.CompilerParams(dimension_semantics=("arbitrary",), vmem_limit_bytes=VMEM_LIMIT_BIG),
        name="peer_out",
    )(idx,w, expand, tab)


def _ple_kernel(h_ref, e8_ref, p_ref, nw_ref, wg_ref, wp_ref, fw_ref, o_ref, *, final_norm):
    h = h_ref[...] + jnp.concatenate([e8_ref[:, r, :] for r in range(SUBLANES)], axis=1)
    gate = _sigmoid(jnp.dot(_rms(h, nw_ref[...]).astype(BF16), wg_ref[...], preferred_element_type=F32))
    out = h + jnp.dot(p_ref[...].astype(BF16), wp_ref[...], preferred_element_type=F32) * gate
    if final_norm:
        out = _rms(out, fw_ref[...])
    o_ref[...] = out


def ple(h, e8, p, nw, wg, wp, fw, tm, final_norm):
    t, d = h.shape
    pd = p.shape[1]
    return pl.pallas_call(
        functools.partial(_ple_kernel, final_norm=final_norm),
        grid=(t // tm,),
        in_specs=[
            pl.BlockSpec((tm, d), lambda i: (i, 0)),
            pl.BlockSpec((tm, SUBLANES, d // SUBLANES), lambda i: (i, 0, 0)),
            pl.BlockSpec((tm, pd), lambda i: (i, 0)),
            pl.BlockSpec((1, d), lambda i: (0, 0)),
            pl.BlockSpec((d, d), lambda i: (0, 0)),
            pl.BlockSpec((pd, d), lambda i: (0, 0)),
            pl.BlockSpec((1, d), lambda i: (0, 0)),
        ],
        out_specs=pl.BlockSpec((tm, d), lambda i: (i, 0)),
        out_shape=jax.ShapeDtypeStruct((t, d), F32),
        compiler_params=pltpu.CompilerParams(dimension_semantics=("parallel",), vmem_limit_bytes=VMEM_LIMIT),
        name="ple",
    )(h, e8, p, nw, wg, wp, fw)


def _pack_table_kernel(t_ref, o_ref):
    x = t_ref[...]
    te = x.shape[0]
    for s in range(PEER_ROW_WORDS):
        lo = pltpu.bitcast(x[:, 256 * s:256 * s + 128].astype(BF16).astype(F32), jnp.int32)
        hi = pltpu.bitcast(x[:, 256 * s + 128:256 * s + 256].astype(BF16).astype(F32), jnp.int32)
        o_ref[pl.ds(s, te, stride=PEER_ROW_WORDS), :] = (hi & jnp.int32(-65536)) | lax.shift_right_logical(lo, 16)


def _pack_table(tab):
    e, d = tab.shape
    te = min(512, e)
    return pl.pallas_call(
        _pack_table_kernel,
        grid=(e // te,),
        in_specs=[pl.BlockSpec((te, d), lambda i: (i, 0))],
        out_specs=pl.BlockSpec((PEER_ROW_WORDS * te, 128), lambda i: (i, 0)),
        out_shape=jax.ShapeDtypeStruct((PEER_ROW_WORDS * e, 128), jnp.int32),
        compiler_params=pltpu.CompilerParams(dimension_semantics=("parallel",), vmem_limit_bytes=VMEM_LIMIT),
        name="pack_table",
    )(tab)


def _even_in_perm():
    off = np.cumsum([0, 256, 256, 512, 512, 16, 256, 256, 512, 512])
    qa, ka, va, ga, ra, qb, kb, vb, gb = [np.arange(off[i], off[i + 1]) for i in range(9)]
    half = RET_DK // 2
    rot = np.concatenate([np.concatenate([np.arange(h * RET_DK, h * RET_DK + half) for h in range(RET_HEADS)]),
                          np.concatenate([np.arange(h * RET_DK + half, (h + 1) * RET_DK) for h in range(RET_HEADS)])])
    return np.concatenate([qa, ka, qb[rot], kb[rot], va, ga, vb, gb, ra])


def _retention_tables():
    lg = np.log(1.0 - 2.0 ** (-5.0 - np.arange(RET_HEADS, dtype=np.float64)))
    idx = np.arange(CHUNK, dtype=np.float64)
    diff = idx[:, None] - idx[None, :]
    dmask = np.where(diff >= 0, np.exp(lg[:, None, None] * np.maximum(diff, 0.0)), 0.0)
    qdec = np.exp(lg[:, None] * (idx + 1.0))
    kdec = np.exp(lg[:, None] * (CHUNK - 1.0 - idx))
    cdec = np.exp(lg * CHUNK)
    return (jnp.asarray(dmask, F32),
            jnp.asarray(np.broadcast_to(kdec[:, :, None], (RET_HEADS, CHUNK, 128)), F32),
            jnp.asarray(np.broadcast_to(qdec[:, :, None], (RET_HEADS, CHUNK, 128)), F32),
            jnp.asarray(np.broadcast_to(cdec[:, None, None], (RET_HEADS, 1, 256)), F32))


def _peer_layer(h, nw, w_q, sub_keys, u_tab, v_tab, tm_route, tb):
    keys = sub_keys.reshape(2 * PEER_HEADS, PEER_NKEYS, -1)
    xn8, experts, gates = peer_route(h, nw, w_q.T, keys, tm_route)
    kk = np.arange(SUBLANES * PEER_HK) // SUBLANES
    sel = jnp.asarray(kk[:, None] == np.arange(PEER_HK)[None, :], BF16)
    w = peer_hidden(experts, xn8, gates, sel, _pack_table(u_tab), tb)
    return peer_out(experts, w, sel.T, _pack_table(v_tab), tb)


def _row(v):
    return v.reshape(1, -1).astype(F32)


def _layer0_mixer(h, w, bsz, seq):
    t = h.shape[0]
    w_in = w["ev_w_in"][0]
    w_in0 = jnp.pad(w_in[:, _even_in_perm()], ((0, 0), (0, EVEN_COLS - w_in.shape[1]))).astype(BF16)
    proj = norm_matmul(h, _row(w["norm_mix_w"][0]), w_in0, min(256, t))
    half = RET_DK // 2
    freqs = ROPE_BASE ** (-np.arange(half, dtype=np.float32) / half)
    dmask, kdec, qdec, cdec = _retention_tables()
    consts = (
        jnp.asarray(np.tile(freqs, RET_HEADS)[None, :], F32),
        jnp.pad(w["ev_gla_w_up"][0], ((0, 128 - GLA_GATE_RANK), (0, 0))).astype(BF16),
        _row(w["ev_gla_b_up"][0]), _row(w["ev_gla_norm_w"][0]), _row(w["ev_ret_norm_w"][0]),
        w["ev_w_out"][0].astype(BF16),
        jnp.asarray(np.tril(np.ones((CHUNK, CHUNK))), BF16),
        dmask, kdec, qdec, cdec,
    )
    pos = w["positions"].astype(F32).reshape(seq, 1)
    return even_mixer(h, proj, pos, consts, bsz, seq, min(256, seq))


def _layer1_mixer(h, w, bsz, seq):
    t = h.shape[0]
    u = norm_matmul(h, _row(w["norm_mix_w"][1]), w["od_w_in"][0].astype(BF16), min(512, t))
    pw_re, pw_im, bb_re, bb_im = s5_params(w["od_s5_a_re"][0], w["od_s5_a_im"][0], w["od_s5_log_dt"][0].reshape(-1, 1),
                                           w["od_s5_b_re"][0].transpose(0, 2, 1), w["od_s5_b_im"][0].transpose(0, 2, 1))
    eye = jnp.eye(S5_GROUPS, dtype=F32)
    blockdiag = lambda m: (m[:, :, None, :] * eye[:, None, :, None]).reshape(m.shape[0] * m.shape[1], -1)
    consts = (
        w["od_pool_w"][0].astype(BF16), _row(w["od_pool_scale"][0]),
        blockdiag(bb_re).astype(BF16), blockdiag(bb_im).astype(BF16),
        blockdiag(w["od_s5_c_re"][0]).T.astype(BF16), blockdiag(w["od_s5_c_im"][0]).T.astype(BF16),
        pw_re.reshape(SUBLANES, S5_STATE), pw_im.reshape(SUBLANES, S5_STATE),
        _row(w["od_s5_d"][0]), w["od_s5_w_glu"][0].astype(BF16), _row(w["od_s5_b_glu"][0]), w["od_w_out"][0].astype(BF16),
    )
    return odd_mixer(h, u, consts, bsz, seq, min(256, seq))


def kernel(x, p, positions, norm_mix_w, norm_ffn_w, norm_ple_w, final_norm_w, ev_w_in, ev_gla_w_up, ev_gla_b_up, ev_gla_norm_w, ev_ret_norm_w, ev_w_out, od_w_in, od_pool_w, od_pool_scale, od_s5_a_re, od_s5_a_im, od_s5_log_dt, od_s5_b_re, od_s5_b_im, od_s5_c_re, od_s5_c_im, od_s5_d, od_s5_w_glu, od_s5_b_glu, od_w_out, peer_w_q, peer_sub_keys, peer_u, peer_v, ple_w_proj, ple_w_gate):
    w = dict(locals())
    bsz, seq, d = x.shape
    t = bsz * seq
    tm_tok = min(512, t)
    tm_route = min(256, t)
    tb = min(512, t)
    h = x.reshape(t, d)
    for i, mixer in enumerate((_layer0_mixer, _layer1_mixer)):
        h = mixer(h, w, bsz, seq)
        e8 = _peer_layer(h, _row(norm_ffn_w[i]), peer_w_q[i], peer_sub_keys[i], peer_u[i], peer_v[i], tm_route, tb)
        h = ple(h, e8, p[i].reshape(t, -1), _row(norm_ple_w[i]), ple_w_gate[i].astype(BF16), ple_w_proj[i].astype(BF16),
                _row(final_norm_w), tm_tok, i == 1)
    return h.reshape(bsz, seq, d)
```

```python
import functools
import math

import numpy as np
import jax
import jax.numpy as jnp
from jax import lax
from jax.experimental import pallas as pl
from jax.experimental.pallas import tpu as pltpu

F32 = jnp.float32
BF16 = jnp.bfloat16

D_MODEL = 1024
NORM_EPS = 1e-6
CHUNK = 64
GLA_HEADS = 4
GLA_DK = 64
GLA_DV = 128
GLA_GATE_RANK = 16
GLA_GATE_NORM = 16.0
RET_HEADS = 4
RET_DK = 64
RET_DV = 128
ROPE_BASE = 10000.0
POOL_WINDOWS = (2, 4, 8, 16)
POOL_WIDTH = 512
POOL_GROUP_WIDTH = 128
S5_H = 16
S5_P = 64
S5_GROUPS = 32
S5_WIDTH = 512
S5_STATE = S5_GROUPS * S5_P
PEER_HEADS = 8
PEER_NKEYS = 128
PEER_TOPK = 16
PEER_HK = PEER_HEADS * PEER_TOPK
PEER_EXPERTS = PEER_NKEYS * PEER_NKEYS
EVEN_COLS = 3200

VMEM_LIMIT_BIG = 56 * 1024 * 1024
VMEM_LIMIT = 48 * 1024 * 1024
SUBLANES = 8

NT_DIMS = (((1,), (1,)), ((), ()))


def _split_bf16(x):
    hi = x.astype(BF16)
    lo = (x - hi.astype(F32)).astype(BF16)
    return hi, lo


def _rms(x, w):
    return x * lax.rsqrt(jnp.mean(x * x, axis=-1, keepdims=True) + NORM_EPS) * w


def _sigmoid(x):
    return 1.0 / (1.0 + jnp.exp(-x))


def _gelu(x):
    return 0.5 * x * (1.0 + lax.erf(x * (2.0 ** -0.5)))


def _log_sigmoid(z):
    return jnp.minimum(z, 0.0) - jnp.log1p(jnp.exp(-jnp.abs(z)))


def _norm_mm_kernel(h_ref, nw_ref, w_ref, o_ref):
    xn = _rms(h_ref[...], nw_ref[...])
    o_ref[...] = jnp.dot(xn.astype(BF16), w_ref[...], preferred_element_type=F32)


def norm_matmul(h, nw, w, tm):
    t, d = h.shape
    n = w.shape[1]
    return pl.pallas_call(
        _norm_mm_kernel,
        grid=(t // tm,),
        in_specs=[
            pl.BlockSpec((tm, d), lambda i: (i, 0)),
            pl.BlockSpec((1, d), lambda i: (0, 0)),
            pl.BlockSpec((d, n), lambda i: (0, 0)),
        ],
        out_specs=pl.BlockSpec((tm, n), lambda i: (i, 0)),
        out_shape=jax.ShapeDtypeStruct((t, n), F32),
        compiler_params=pltpu.CompilerParams(dimension_semantics=("parallel",), vmem_limit_bytes=VMEM_LIMIT),
        name="norm_matmul",
    )(h, nw, w)


def _rotary_table_kernel(pos_ref, freq_ref, cos_ref, sin_ref):
    ang = pos_ref[...] * freq_ref[...]
    cos_ref[...] = jnp.cos(ang)
    sin_ref[...] = jnp.sin(ang)


def rotary_table(pos, freqs, ts):
    seq = pos.shape[0]
    n = freqs.shape[1]
    return pl.pallas_call(
        _rotary_table_kernel,
        grid=(seq // ts,),
        in_specs=[pl.BlockSpec((ts, 1), lambda i: (i, 0)), pl.BlockSpec((1, n), lambda i: (0, 0))],
        out_specs=[pl.BlockSpec((ts, n), lambda i: (i, 0)), pl.BlockSpec((ts, n), lambda i: (i, 0))],
        out_shape=[jax.ShapeDtypeStruct((seq, n), F32), jax.ShapeDtypeStruct((seq, n), F32)],
        compiler_params=pltpu.CompilerParams(dimension_semantics=("parallel",)),
        name="rotary_table",
    )(pos, freqs)


def _even_mixer_kernel(h_ref, qk_ref, va_ref, ga_ref, vb_ref, gb_ref, ra_ref, cos_ref, sin_ref,
                       wup_ref, bup_ref, gnw_ref, rnw_ref, wout_ref, tril_ref,
                       dmask_ref, kdec_ref, qdec_ref, cdec_ref, hmaskg_ref, hmaskr_ref,
                       o_ref, gstate, rstate, y_scr, qin_scr, dec_scr, kvg_scr, kvr_scr):
    tm = h_ref.shape[0]

    @pl.when(pl.program_id(1) == 0)
    def _():
        gstate[...] = jnp.zeros_like(gstate)
        rstate[...] = jnp.zeros_like(rstate)

    lane = lax.broadcasted_iota(jnp.int32, (1, 256), 1)
    gla_masks = [((lane >= GLA_DK * h) & (lane < GLA_DK * (h + 1))).astype(F32) for h in range(GLA_HEADS)]
    half = RET_DK // 2
    ret_masks = [(((lane >= half * h) & (lane < half * (h + 1)))
                  | ((lane >= 128 + half * h) & (lane < 128 + half * (h + 1)))).astype(F32)
                 for h in range(RET_HEADS)]
    ri = lax.broadcasted_iota(jnp.int32, (CHUNK, CHUNK), 0)
    ci = lax.broadcasted_iota(jnp.int32, (CHUNK, CHUNK), 1)
    causal = ri >= ci
    tril = tril_ref[...]
    wup = wup_ref[...]
    bup = bup_ref[...]

    n_chunks = tm // CHUNK
    causal4 = jnp.concatenate([causal] * GLA_HEADS, axis=0)
    dmask4 = jnp.concatenate([dmask_ref[h] for h in range(RET_HEADS)], axis=0)
    hmask_g = hmaskg_ref[...]
    hmask_r = hmaskr_ref[...]

    for c in range(n_chunks):
        rows = slice(c * CHUNK, (c + 1) * CHUNK)
        qk = qk_ref[rows, :]
        qa, ka, qb, kb = qk[:, 0:256], qk[:, 256:512], qk[:, 512:768], qk[:, 768:1024]

        z = jnp.dot(ra_ref[rows, :].astype(BF16), wup, preferred_element_type=F32) + bup
        la = _log_sigmoid(z) * (1.0 / GLA_GATE_NORM)
        lah, lal = _split_bf16(la)
        g = (jnp.dot(tril, lah, preferred_element_type=F32)
             + jnp.dot(tril, lal, preferred_element_type=F32))
        g_last = g[CHUNK - 1:CHUNK, :]
        ref = 0.5 * g_last
        qs = qa * (GLA_DK ** -0.5)
        qt = qs * jnp.exp(g - ref)
        kt = (ka * jnp.exp(ref - g)).astype(BF16)
        kd = ka * jnp.exp(g_last - g)
        qin_scr[rows, 0:256] = qs * jnp.exp(g)
        dec_scr[c] = jnp.exp(g_last)
        q4 = jnp.concatenate([(qt * gla_masks[h]) for h in range(GLA_HEADS)], axis=0).astype(BF16)
        s4 = jnp.where(causal4, lax.dot_general(q4, kt, NT_DIMS, preferred_element_type=F32), 0.0).astype(BF16)
        va = va_ref[rows, :]
        vab = va.astype(BF16)
        for h in range(GLA_HEADS):
            y_scr[rows, GLA_DV * h:GLA_DV * (h + 1)] = jnp.dot(
                s4[CHUNK * h:CHUNK * (h + 1), :], vab[:, GLA_DV * h:GLA_DV * (h + 1)], preferred_element_type=F32)
        kvg_scr[c] = jnp.dot(va.T.astype(BF16), kd.astype(BF16), preferred_element_type=F32) * hmask_g

        cs, sn = cos_ref[rows, :], sin_ref[rows, :]
        q1, q2 = qb[:, 0:128], qb[:, 128:256]
        k1, k2 = kb[:, 0:128], kb[:, 128:256]
        qr = jnp.concatenate([q1 * cs - q2 * sn, q2 * cs + q1 * sn], axis=1)
        kr = jnp.concatenate([k1 * cs - k2 * sn, k2 * cs + k1 * sn], axis=1) * (RET_DK ** -0.5)
        krb = kr.astype(BF16)
        qin_scr[rows, 256:512] = qr
        q4 = jnp.concatenate([(qr * ret_masks[h]) for h in range(RET_HEADS)], axis=0).astype(BF16)
        s4 = (lax.dot_general(q4, krb, NT_DIMS, preferred_element_type=F32) * dmask4).astype(BF16)
        vb = vb_ref[rows, :]
        vbb = vb.astype(BF16)
        for h in range(RET_HEADS):
            y_scr[rows, 512 + RET_DV * h:512 + RET_DV * (h + 1)] = jnp.dot(
                s4[CHUNK * h:CHUNK * (h + 1), :], vbb[:, RET_DV * h:RET_DV * (h + 1)], preferred_element_type=F32)
        kvr_scr[c] = jnp.dot((vb * kdec_ref[...]).T.astype(BF16), krb, preferred_element_type=F32) * hmask_r

    for c in range(n_chunks):
        rows = slice(c * CHUNK, (c + 1) * CHUNK)
        sg = gstate[...]
        y_scr[rows, 0:512] += lax.dot_general(qin_scr[rows, 0:256].astype(BF16), sg.astype(BF16), NT_DIMS,
                                              preferred_element_type=F32)
        gstate[...] = sg * dec_scr[c] + kvg_scr[c]
        sr = rstate[...]
        y_scr[rows, 512:1024] += lax.dot_general(qin_scr[rows, 256:512].astype(BF16), sr.astype(BF16), NT_DIMS,
                                                 preferred_element_type=F32) * qdec_ref[...]
        rstate[...] = sr * cdec_ref[...] + kvr_scr[c]

    pieces = []
    gnw = gnw_ref[...]
    for h in range(GLA_HEADS):
        o = y_scr[:, GLA_DV * h:GLA_DV * (h + 1)]
        gt = ga_ref[:, GLA_DV * h:GLA_DV * (h + 1)]
        pieces.append(_rms(o, gnw) * (gt * _sigmoid(gt)))
    for h in range(RET_HEADS):
        o = y_scr[:, 512 + RET_DV * h:512 + RET_DV * (h + 1)]
        gt = gb_ref[:, RET_DV * h:RET_DV * (h + 1)]
        mu = jnp.mean(o, axis=-1, keepdims=True)
        oc = o - mu
        var = jnp.mean(oc * oc, axis=-1, keepdims=True)
        nrm = oc * lax.rsqrt(var + NORM_EPS) * rnw_ref[:, RET_DV * h:RET_DV * (h + 1)]
        pieces.append(nrm * (gt * _sigmoid(gt)))
    y = jnp.concatenate(pieces, axis=1).astype(BF16)
    o_ref[...] = h_ref[...] + jnp.dot(y, wout_ref[...], preferred_element_type=F32)


def even_mixer(h, proj, pos, consts, bsz, seq, tm):
    nj = seq // tm
    tok = lambda b, j: (b * nj + j, 0)
    col = lambda cb: (lambda b, j: (b * nj + j, cb))
    full2 = lambda b, j: (0, 0)
    full3 = lambda b, j: (0, 0, 0)
    (freqs, wup, bup, gnw, rnw, wout, tril, dmask, kdec, qdec, cdec, hmask_g, hmask_r) = consts
    cos_t, sin_t = rotary_table(pos, freqs, min(512, seq))
    n_state = GLA_HEADS * GLA_DV
    return pl.pallas_call(
        _even_mixer_kernel,
        grid=(bsz, nj),
        in_specs=[
            pl.BlockSpec((tm, D_MODEL), tok),
            pl.BlockSpec((tm, 1024), col(0)),
            pl.BlockSpec((tm, 512), col(2)),
            pl.BlockSpec((tm, 512), col(3)),
            pl.BlockSpec((tm, 512), col(4)),
            pl.BlockSpec((tm, 512), col(5)),
            pl.BlockSpec((tm, 128), col(24)),
            pl.BlockSpec((tm, 128), lambda b, j: (j, 0)),
            pl.BlockSpec((tm, 128), lambda b, j: (j, 0)),
            pl.BlockSpec((128, 256), full2),
            pl.BlockSpec((1, 256), full2),
            pl.BlockSpec((1, 128), full2),
            pl.BlockSpec((1, 512), full2),
            pl.BlockSpec((1024, D_MODEL), full2),
            pl.BlockSpec((CHUNK, CHUNK), full2),
            pl.BlockSpec((RET_HEADS, CHUNK, CHUNK), full3),
            pl.BlockSpec((CHUNK, n_state), full2),
            pl.BlockSpec((CHUNK, n_state), full2),
            pl.BlockSpec((n_state, 256), full2),
            pl.BlockSpec((n_state, 256), full2),
            pl.BlockSpec((n_state, 256), full2),
        ],
        out_specs=pl.BlockSpec((tm, D_MODEL), tok),
        out_shape=jax.ShapeDtypeStruct(h.shape, F32),
        scratch_shapes=[
            pltpu.VMEM((n_state, 256), F32),
            pltpu.VMEM((n_state, 256), F32),
            pltpu.VMEM((tm, 1024), F32),
            pltpu.VMEM((tm, 512), F32),
            pltpu.VMEM((tm // CHUNK, 1, 256), F32),
            pltpu.VMEM((tm // CHUNK, n_state, 256), F32),
            pltpu.VMEM((tm // CHUNK, n_state, 256), F32),
        ],
        compiler_params=pltpu.CompilerParams(dimension_semantics=("arbitrary", "arbitrary"),
                                             vmem_limit_bytes=VMEM_LIMIT),
        name="even_mixer",
    )(h, proj, proj, proj, proj, proj, proj, cos_t, sin_t, wup, bup, gnw, rnw, wout, tril, dmask, kdec, qdec, cdec,
      hmask_g, hmask_r)


def _s5_param_kernel(are_ref, aim_ref, ldt_ref, bre_ref, bim_ref,
                     pw_re_ref, pw_im_ref, bbre_ref, bbim_ref):
    a_re = are_ref[...]
    a_im = aim_ref[...]
    dt = jnp.exp(ldt_ref[...])
    for r in range(SUBLANES):
        mag = jnp.exp(a_re * dt * (r + 1.0))
        pw_re_ref[r] = mag * jnp.cos(a_im * dt * (r + 1.0))
        pw_im_ref[r] = mag * jnp.sin(a_im * dt * (r + 1.0))
    mag = jnp.exp(a_re * dt)
    abar_re, abar_im = mag * jnp.cos(a_im * dt), mag * jnp.sin(a_im * dt)
    den = a_re * a_re + a_im * a_im
    nr, ni = abar_re - 1.0, abar_im
    coef_re = (nr * a_re + ni * a_im) / den
    coef_im = (ni * a_re - nr * a_im) / den
    b_re = bre_ref[...]
    b_im = bim_ref[...]
    c_re = jnp.concatenate([coef_re] * S5_H, axis=1)
    c_im = jnp.concatenate([coef_im] * S5_H, axis=1)
    bbre_ref[...] = c_re * b_re - c_im * b_im
    bbim_ref[...] = c_re * b_im + c_im * b_re


def s5_params(a_re, a_im, log_dt, b_re_t, b_im_t):
    g, p = a_re.shape
    hh = b_re_t.shape[1]
    pw_re, pw_im, bb_re, bb_im = pl.pallas_call(
        _s5_param_kernel,
        out_shape=[jax.ShapeDtypeStruct((SUBLANES, g, p), F32), jax.ShapeDtypeStruct((SUBLANES, g, p), F32),
                   jax.ShapeDtypeStruct((g, hh * p), F32), jax.ShapeDtypeStruct((g, hh * p), F32)],
        name="s5_params",
    )(a_re, a_im, log_dt, b_re_t.reshape(g, hh * p), b_im_t.reshape(g, hh * p))
    return pw_re, pw_im, bb_re.reshape(g, hh, p), bb_im.reshape(g, hh, p)


def _odd_mixer_kernel(h_ref, u_ref, poolw_ref, pscale_ref, wbre_ref, wbim_ref, wcre_ref, wcim_ref,
                      pwre_ref, pwim_ref, dskip_ref, wglu_ref, bglu_ref, wout_ref,
                      o_ref, tail_scr, car_re, car_im, xre_scr, xim_scr):
    tm = h_ref.shape[0]
    j = pl.program_id(1)
    halo = POOL_WINDOWS[-1]

    @pl.when(j == 0)
    def _():
        tail_scr[...] = jnp.zeros_like(tail_scr)
        car_re[...] = jnp.zeros_like(car_re)
        car_im[...] = jnp.zeros_like(car_im)

    uc = u_ref[:, 0:POOL_WIDTH]
    ud = u_ref[:, POOL_WIDTH:POOL_WIDTH + S5_WIDTH]

    ext = jnp.concatenate([tail_scr[...], uc], axis=0)
    tail_scr[...] = uc[tm - halo:tm, :]
    pos = (j * tm + lax.broadcasted_iota(jnp.int32, (tm, 1), 0)).astype(F32)
    mixed = []
    for gi, win in enumerate(POOL_WINDOWS):
        a = ext[:, POOL_GROUP_WIDTH * gi:POOL_GROUP_WIDTH * (gi + 1)]
        n = tm + halo
        step = 1
        end = 0
        while step < win:
            a = a[step:n, :] + a[0:n - step, :]
            n -= step
            end += step
            step *= 2
        wsum = a[halo - end:halo - end + tm, :]
        cnt = jnp.minimum(pos + 1.0, float(win))
        pooled = wsum / cnt - uc[:, POOL_GROUP_WIDTH * gi:POOL_GROUP_WIDTH * (gi + 1)]
        mixed.append(jnp.dot(pooled.astype(BF16), poolw_ref[gi], preferred_element_type=F32))
    y_c = jnp.concatenate(mixed, axis=1) * pscale_ref[...]

    udb = ud.astype(BF16)
    hw, hs = S5_WIDTH // 2, S5_STATE // 2
    for hf in range(2):
        cols = slice(hs * hf, hs * (hf + 1))
        uh = udb[:, hw * hf:hw * (hf + 1)]
        xre_scr[:, cols] = jnp.dot(uh, wbre_ref[hw * hf:hw * (hf + 1), cols], preferred_element_type=F32)
        xim_scr[:, cols] = jnp.dot(uh, wbim_ref[hw * hf:hw * (hf + 1), cols], preferred_element_type=F32)
    rowi = lax.broadcasted_iota(jnp.int32, (SUBLANES, S5_STATE), 0)
    pw_re = pwre_ref[...]
    pw_im = pwim_ref[...]
    step_pw = [(jnp.where(rowi >= d, pw_re[d - 1:d, :], 0.0), jnp.where(rowi >= d, pw_im[d - 1:d, :], 0.0))
               for d in (1, 2, 4)]

    def slab(s, carry):
        cr, ci = carry
        rows = pl.ds(pl.multiple_of(s * SUBLANES, SUBLANES), SUBLANES)
        xr = xre_scr[rows, :]
        xi = xim_scr[rows, :]
        for dsh, (pr, pi) in zip((1, 2, 4), step_pw):
            sr = pltpu.roll(xr, dsh, axis=0)
            si = pltpu.roll(xi, dsh, axis=0)
            xr, xi = xr + (pr * sr - pi * si), xi + (pr * si + pi * sr)
        xr, xi = xr + (pw_re * cr - pw_im * ci), xi + (pw_re * ci + pw_im * cr)
        xre_scr[rows, :] = xr
        xim_scr[rows, :] = xi
        return xr[SUBLANES - 1:SUBLANES, :], xi[SUBLANES - 1:SUBLANES, :]

    cr, ci = lax.fori_loop(0, tm // SUBLANES, slab, (car_re[...], car_im[...]))
    car_re[...] = cr
    car_im[...] = ci

    yh = []
    for hf in range(2):
        rws = slice(hs * hf, hs * (hf + 1))
        cls = slice(hw * hf, hw * (hf + 1))
        yh.append(jnp.dot(xre_scr[:, rws].astype(BF16), wcre_ref[rws, cls], preferred_element_type=F32)
                  - jnp.dot(xim_scr[:, rws].astype(BF16), wcim_ref[rws, cls], preferred_element_type=F32))
    y = jnp.concatenate(yh, axis=1) + dskip_ref[...] * ud
    z = _gelu(y)
    y_d = z * _sigmoid(jnp.dot(z.astype(BF16), wglu_ref[...], preferred_element_type=F32) + bglu_ref[...])

    ycat = jnp.concatenate([y_c, y_d], axis=1).astype(BF16)
    o_ref[...] = h_ref[...] + jnp.dot(ycat, wout_ref[...], preferred_element_type=F32)


def odd_mixer(h, u, consts, bsz, seq, tm):
    nj = seq // tm
    tok = lambda b, j: (b * nj + j, 0)
    full2 = lambda b, j: (0, 0)
    full3 = lambda b, j: (0, 0, 0)
    (poolw, pscale, wbre, wbim, wcre, wcim, pwre, pwim, dskip, wglu, bglu, wout) = consts
    return pl.pallas_call(
        _odd_mixer_kernel,
        grid=(bsz, nj),
        in_specs=[
            pl.BlockSpec((tm, D_MODEL), tok),
            pl.BlockSpec((tm, 1024), tok),
            pl.BlockSpec((4, 128, 128), full3),
            pl.BlockSpec((1, POOL_WIDTH), full2),
            pl.BlockSpec((S5_WIDTH, S5_STATE), full2),
            pl.BlockSpec((S5_WIDTH, S5_STATE), full2),
            pl.BlockSpec((S5_STATE, S5_WIDTH), full2),
            pl.BlockSpec((S5_STATE, S5_WIDTH), full2),
            pl.BlockSpec((SUBLANES, S5_STATE), full2),
            pl.BlockSpec((SUBLANES, S5_STATE), full2),
            pl.BlockSpec((1, S5_WIDTH), full2),
            pl.BlockSpec((S5_WIDTH, S5_WIDTH), full2),
            pl.BlockSpec((1, S5_WIDTH), full2),
            pl.BlockSpec((1024, D_MODEL), full2),
        ],
        out_specs=pl.BlockSpec((tm, D_MODEL), tok),
        out_shape=jax.ShapeDtypeStruct(h.shape, F32),
        scratch_shapes=[
            pltpu.VMEM((POOL_WINDOWS[-1], POOL_WIDTH), F32),
            pltpu.VMEM((1, S5_STATE), F32),
            pltpu.VMEM((1, S5_STATE), F32),
            pltpu.VMEM((tm, S5_STATE), F32),
            pltpu.VMEM((tm, S5_STATE), F32),
        ],
        compiler_params=pltpu.CompilerParams(dimension_semantics=("arbitrary", "arbitrary"),
                                             vmem_limit_bytes=VMEM_LIMIT),
        name="odd_mixer",
    )(h, u, poolw, pscale, wbre, wbim, wcre, wcim, pwre, pwim, dskip, wglu, bglu, wout)


def _top16_rows(s, ids, id_bound):
    vals, idxs = [], []
    for _ in range(PEER_TOPK):
        m = jnp.max(s, axis=0, keepdims=True)
        am = jnp.min(jnp.where(s == m, ids, float(id_bound)), axis=0, keepdims=True)
        vals.append(m)
        idxs.append(am)
        s = jnp.where(ids == am, -jnp.inf, s)
    return jnp.concatenate(vals, axis=0), jnp.concatenate(idxs, axis=0)


_PAIR_BLOCKS = (("b", 0, 0), ("b", 1, 0), ("b", 2, 0), ("b", 3, 0), ("b", 4, 0),
                ("a", 0, 8), ("a", 0, 0), ("a", 1, 0), ("b", 0, 8))
_PAIR_ID_BOUND = 4 * PEER_TOPK * PEER_TOPK


def _pair_block_ids(tm):
    r = lax.broadcasted_iota(jnp.int32, (SUBLANES, tm), 0).astype(F32)
    seen = set()
    out = []
    for side, fixed, start in _PAIR_BLOCKS:
        ids = jnp.zeros((SUBLANES, tm), F32)
        for q in range(SUBLANES):
            i, j = (fixed, start + q) if side == "a" else (start + q, fixed)
            ok = (i + 1) * (j + 1) <= PEER_TOPK and (i, j) not in seen
            seen.add((i, j))
            ids = jnp.where(r == q, float(i * PEER_TOPK + j if ok else _PAIR_ID_BOUND + len(seen)), ids)
        out.append(ids)
    assert len({p for p in seen if (p[0] + 1) * (p[1] + 1) <= PEER_TOPK}) == 50
    return jnp.concatenate(out, axis=0)


def _pair_block_sums(av, bv):
    out = []
    for side, fixed, start in _PAIR_BLOCKS:
        if side == "a":
            out.append(av[fixed:fixed + 1, :] + bv[start:start + SUBLANES, :])
        else:
            out.append(av[start:start + SUBLANES, :] + bv[fixed:fixed + 1, :])
    return jnp.concatenate(out, axis=0)


def _take16(table, sel):
    out = jnp.zeros(sel.shape, table.dtype)
    for i in range(PEER_TOPK):
        out = jnp.where(sel == i, table[i:i + 1, :], out)
    return out


def _dot3(ah, al, bh, bl, dims):
    return (lax.dot_general(ah, bh, dims, preferred_element_type=F32)
            + lax.dot_general(al, bh, dims, preferred_element_type=F32)
            + lax.dot_general(ah, bl, dims, preferred_element_type=F32))


def _peer_route_kernel(h_ref, nw_ref, wqh_ref, wql_ref, kh_ref, kl_ref, xn_ref, exp_ref, gate_ref,
                       qt_scr, et_scr, gt_scr):
    xn = _rms(h_ref[...], nw_ref[...])
    for r in range(SUBLANES):
        xn_ref[:, r, :] = xn[:, 128 * r:128 * (r + 1)]
    xh, xl = _split_bf16(xn)
    qt_scr[...] = _dot3(wqh_ref[...], wql_ref[...], xh, xl, NT_DIMS)
    mm = (((1,), (0,)), ((), ()))
    tm = h_ref.shape[0]
    key_ids = lax.broadcasted_iota(jnp.int32, (PEER_NKEYS, tm), 0).astype(F32)
    pair_ids = _pair_block_ids(tm)
    pair_ok = pair_ids < float(_PAIR_ID_BOUND)

    def head(hd, carry):
        ra = pl.ds(pl.multiple_of(hd * 256, 256), 128)
        rb = pl.ds(pl.multiple_of(hd * 256 + 128, 128), 128)
        qah, qal = _split_bf16(qt_scr[ra, :])
        qbh, qbl = _split_bf16(qt_scr[rb, :])
        sa = _dot3(kh_ref[2 * hd], kl_ref[2 * hd], qah, qal, mm)
        sb = _dot3(kh_ref[2 * hd + 1], kl_ref[2 * hd + 1], qbh, qbl, mm)
        av, ai = _top16_rows(sa, key_ids, PEER_NKEYS)
        bv, bi = _top16_rows(sb, key_ids, PEER_NKEYS)
        cand = jnp.where(pair_ok, _pair_block_sums(av, bv), -jnp.inf)
        cv, flat = _top16_rows(cand, pair_ids, _PAIR_ID_BOUND)
        flat = flat.astype(jnp.int32)
        e_a = _take16(ai, flat >> 4)
        e_b = _take16(bi, flat & (PEER_TOPK - 1))
        ex = jnp.exp(cv - cv[0:1, :])
        rows = pl.ds(pl.multiple_of(hd * PEER_TOPK, PEER_TOPK), PEER_TOPK)
        et_scr[rows, :] = (e_a * float(PEER_NKEYS) + e_b) * float(PEER_ROW_WORDS)
        gt_scr[rows, :] = ex / jnp.sum(ex, axis=0, keepdims=True)
        return carry

    def head_pair(i, carry):
        head(2 * i, carry)
        return head(2 * i + 1, carry)

    lax.fori_loop(0, PEER_HEADS // 2, head_pair, 0)
    exp_ref[...] = et_scr[...].T.astype(jnp.int32)
    gate_ref[...] = gt_scr[...].T


def peer_route(h, nw, wqt, keys, tm):
    t, d = h.shape
    nq = wqt.shape[0]
    wqh, wql = _split_bf16(wqt)
    kh, kl = _split_bf16(keys)
    return pl.pallas_call(
        _peer_route_kernel,
        grid=(t // tm,),
        in_specs=[
            pl.BlockSpec((tm, d), lambda i: (i, 0)),
            pl.BlockSpec((1, d), lambda i: (0, 0)),
            pl.BlockSpec((nq, d), lambda i: (0, 0)),
            pl.BlockSpec((nq, d), lambda i: (0, 0)),
            pl.BlockSpec((2 * PEER_HEADS, PEER_NKEYS, 128), lambda i: (0, 0, 0)),
            pl.BlockSpec((2 * PEER_HEADS, PEER_NKEYS, 128), lambda i: (0, 0, 0)),
        ],
        out_specs=[
            pl.BlockSpec((tm, SUBLANES, d // SUBLANES), lambda i: (i, 0, 0)),
            pl.BlockSpec((tm, PEER_HK), lambda i: (i, 0)),
            pl.BlockSpec((tm, PEER_HK), lambda i: (i, 0)),
        ],
        out_shape=[jax.ShapeDtypeStruct((t, SUBLANES, d // SUBLANES), F32),
                   jax.ShapeDtypeStruct((t, PEER_HK), jnp.int32), jax.ShapeDtypeStruct((t, PEER_HK), F32)],
        scratch_shapes=[pltpu.VMEM((nq, tm), F32), pltpu.VMEM((PEER_HK, tm), F32), pltpu.VMEM((PEER_HK, tm), F32)],
        compiler_params=pltpu.CompilerParams(dimension_semantics=("parallel",), vmem_limit_bytes=VMEM_LIMIT),
        name="peer_route",
    )(h, nw, wqh, wql, kh, kl)


PEER_GROUP = 8
PEER_ROW_WORDS = 4


def _gather_group(idx_ref, tab_ref, g, stage_ref):
    rows = [idx_ref.at[g * PEER_GROUP + j] for j in range(PEER_GROUP)]
    for k in range(PEER_HK):
        for j in range(PEER_GROUP):
            off = pl.multiple_of(rows[j][k], PEER_ROW_WORDS)
            stage_ref[j, pl.ds(PEER_ROW_WORDS * k, PEER_ROW_WORDS), :] = tab_ref[pl.ds(off, PEER_ROW_WORDS), :]


def _gather_compute_pipeline(n_groups, idx_ref, tab_ref, consume, stage_a, stage_b):
    def compute(g, stage_ref):
        for j in range(PEER_GROUP):
            consume(g * PEER_GROUP + j, stage_ref.at[j])

    _gather_group(idx_ref, tab_ref,0, stage_a)

    def body(i, carry):
        compute(2 * i, stage_a)
        _gather_group(idx_ref, tab_ref, 2 * i + 1, stage_b)
        compute(2 * i + 1, stage_b)
        _gather_group(idx_ref, tab_ref, 2 * i + 2, stage_a)
        return carry

    lax.fori_loop(0, n_groups // 2 - 1, body, 0)
    _gather_group(idx_ref, tab_ref,n_groups - 1, stage_b)
    compute(n_groups - 2, stage_a)
    compute(n_groups - 1, stage_b)


def _diag_mask():
    row = lax.broadcasted_iota(jnp.int32, (SUBLANES, SUBLANES * PEER_HK), 0)
    lane = lax.broadcasted_iota(jnp.int32, (SUBLANES, SUBLANES * PEER_HK), 1)
    return (lane & (SUBLANES - 1)) == row


def _peer_hidden_kernel(idx_ref, x_ref, g_ref, sel_ref, tab_ref, o_ref, stage_a, stage_b, part_scr):
    tb = x_ref.shape[0]
    diag = _diag_mask()

    def consume(t, rows_ref):
        u = pltpu.bitcast(rows_ref[...], BF16)
        xh, xl = _split_bf16(x_ref[t])
        x16 = jnp.concatenate([xh, xl], axis=0)
        out = lax.dot_general(x16, u, NT_DIMS, preferred_element_type=F32)
        o8 = out[0:SUBLANES] + out[SUBLANES:2 * SUBLANES]
        part_scr[pl.ds(t, 1), :] = jnp.sum(jnp.where(diag, o8, 0.0), axis=0, keepdims=True)

    _gather_compute_pipeline(tb // PEER_GROUP, idx_ref, tab_ref, consume, stage_a, stage_b)
    ph, plo = _split_bf16(part_scr[...])
    sel = sel_ref[...]
    hid = jnp.dot(ph, sel, preferred_element_type=F32) + jnp.dot(plo, sel, preferred_element_type=F32)
    o_ref[...] = g_ref[...] * _gelu(hid)


def peer_hidden(idx, x8, gates, sel, tab, tb):
    t = idx.shape[0]
    return pl.pallas_call(
        _peer_hidden_kernel,
        grid=(t // tb,),
        in_specs=[
            pl.BlockSpec((tb, PEER_HK), lambda i: (i, 0), memory_space=pltpu.SMEM),
            pl.BlockSpec((tb, SUBLANES, 128), lambda i: (i, 0, 0)),
            pl.BlockSpec((tb, PEER_HK), lambda i: (i, 0)),
            pl.BlockSpec((SUBLANES * PEER_HK, PEER_HK), lambda i: (0, 0)),
            pl.BlockSpec((PEER_ROW_WORDS * PEER_EXPERTS, 128), lambda i: (0, 0), pipeline_mode=pl.Buffered(1)),
        ],
        out_specs=pl.BlockSpec((tb, PEER_HK), lambda i: (i, 0)),
        out_shape=jax.ShapeDtypeStruct((t, PEER_HK), F32),
        scratch_shapes=[pltpu.VMEM((PEER_GROUP, PEER_ROW_WORDS * PEER_HK, 128), jnp.int32),
                        pltpu.VMEM((PEER_GROUP, PEER_ROW_WORDS * PEER_HK, 128), jnp.int32),
                        pltpu.VMEM((tb, SUBLANES * PEER_HK), F32)],
        compiler_params=pltpu.CompilerParams(dimension_semantics=("arbitrary",), vmem_limit_bytes=VMEM_LIMIT_BIG),
        name="peer_hidden",
    )(idx,x8, gates, sel, tab)


def _peer_out_kernel(idx_ref, w_ref, exp_ref, tab_ref, o_ref, stage_a, stage_b, wexp_scr):
    tb = w_ref.shape[0]
    diag = _diag_mask()
    wh, wl = _split_bf16(w_ref[...])
    ex = exp_ref[...]
    wexp_scr[...] = jnp.dot(wh, ex, preferred_element_type=F32) + jnp.dot(wl, ex, preferred_element_type=F32)

    def consume(t, rows_ref):
        v = pltpu.bitcast(rows_ref[...], BF16)
        w8 = jnp.where(diag, jnp.broadcast_to(wexp_scr[pl.ds(t, 1), :], (SUBLANES, SUBLANES * PEER_HK)), 0.0)
        w8h, w8l = _split_bf16(w8)
        w16 = jnp.concatenate([w8h, w8l], axis=0)
        out = jnp.dot(w16, v, preferred_element_type=F32)
        o_ref[t] = out[0:SUBLANES] + out[SUBLANES:2 * SUBLANES]

    _gather_compute_pipeline(tb // PEER_GROUP, idx_ref, tab_ref, consume, stage_a, stage_b)


def peer_out(idx, w, expand, tab, tb):
    t = idx.shape[0]
    return pl.pallas_call(
        _peer_out_kernel,
        grid=(t // tb,),
        in_specs=[
            pl.BlockSpec((tb, PEER_HK), lambda i: (i, 0), memory_space=pltpu.SMEM),
            pl.BlockSpec((tb, PEER_HK), lambda i: (i, 0)),
            pl.BlockSpec((PEER_HK, SUBLANES * PEER_HK), lambda i: (0, 0)),
            pl.BlockSpec((PEER_ROW_WORDS * PEER_EXPERTS, 128), lambda i: (0, 0), pipeline_mode=pl.Buffered(1)),
        ],
        out_specs=pl.BlockSpec((tb, SUBLANES, 128), lambda i: (i, 0, 0)),
        out_shape=jax.ShapeDtypeStruct((t, SUBLANES, 128), F32),
        scratch_shapes=[pltpu.VMEM((PEER_GROUP, PEER_ROW_WORDS * PEER_HK, 128), jnp.int32),
                        pltpu.VMEM((PEER_GROUP, PEER_ROW_WORDS * PEER_HK, 128), jnp.int32),
                        pltpu.VMEM((tb, SUBLANES * PEER_HK), F32)],
        compiler_params=pltpu.CompilerParams(dimension_semantics=("arbitrary",), vmem_limit_bytes=VMEM_LIMIT_BIG),
        name="peer_out",
    )(idx,w, expand, tab)


def _ple_kernel(h_ref, e8_ref, p_ref, nw_ref, wg_ref, wp_ref, fw_ref, o_ref, *, final_norm):
    h = h_ref[...] + jnp.concatenate([e8_ref[:, r, :] for r in range(SUBLANES)], axis=1)
    gate = _sigmoid(jnp.dot(_rms(h, nw_ref[...]).astype(BF16), wg_ref[...], preferred_element_type=F32))
    out = h + jnp.dot(p_ref[...].astype(BF16), wp_ref[...], preferred_element_type=F32) * gate
    if final_norm:
        out = _rms(out, fw_ref[...])
    o_ref[...] = out


def ple(h, e8, p, nw, wg, wp, fw, tm, final_norm):
    t, d = h.shape
    pd = p.shape[1]
    return pl.pallas_call(
        functools.partial(_ple_kernel, final_norm=final_norm),
        grid=(t // tm,),
        in_specs=[
            pl.BlockSpec((tm, d), lambda i: (i, 0)),
            pl.BlockSpec((tm, SUBLANES, d // SUBLANES), lambda i: (i, 0, 0)),
            pl.BlockSpec((tm, pd), lambda i: (i, 0)),
            pl.BlockSpec((1, d), lambda i: (0, 0)),
            pl.BlockSpec((d, d), lambda i: (0, 0)),
            pl.BlockSpec((pd, d), lambda i: (0, 0)),
            pl.BlockSpec((1, d), lambda i: (0, 0)),
        ],
        out_specs=pl.BlockSpec((tm, d), lambda i: (i, 0)),
        out_shape=jax.ShapeDtypeStruct((t, d), F32),
        compiler_params=pltpu.CompilerParams(dimension_semantics=("parallel",), vmem_limit_bytes=VMEM_LIMIT),
        name="ple",
    )(h, e8, p, nw, wg, wp, fw)


def _pack_table_kernel(t_ref, o_ref):
    x = t_ref[...]
    te = x.shape[0]
    for s in range(PEER_ROW_WORDS):
        lo = pltpu.bitcast(x[:, 256 * s:256 * s + 128].astype(BF16).astype(F32), jnp.int32)
        hi = pltpu.bitcast(x[:, 256 * s + 128:256 * s + 256].astype(BF16).astype(F32), jnp.int32)
        o_ref[pl.ds(s, te, stride=PEER_ROW_WORDS), :] = (hi & jnp.int32(-65536)) | lax.shift_right_logical(lo, 16)


def _pack_table(tab):
    e, d = tab.shape
    te = min(512, e)
    return pl.pallas_call(
        _pack_table_kernel,
        grid=(e // te,),
        in_specs=[pl.BlockSpec((te, d), lambda i: (i, 0))],
        out_specs=pl.BlockSpec((PEER_ROW_WORDS * te, 128), lambda i: (i, 0)),
        out_shape=jax.ShapeDtypeStruct((PEER_ROW_WORDS * e, 128), jnp.int32),
        compiler_params=pltpu.CompilerParams(dimension_semantics=("parallel",), vmem_limit_bytes=VMEM_LIMIT),
        name="pack_table",
    )(tab)


def _even_in_perm():
    off = np.cumsum([0, 256, 256, 512, 512, 16, 256, 256, 512, 512])
    qa, ka, va, ga, ra, qb, kb, vb, gb = [np.arange(off[i], off[i + 1]) for i in range(9)]
    half = RET_DK // 2
    rot = np.concatenate([np.concatenate([np.arange(h * RET_DK, h * RET_DK + half) for h in range(RET_HEADS)]),
                          np.concatenate([np.arange(h * RET_DK + half, (h + 1) * RET_DK) for h in range(RET_HEADS)])])
    return np.concatenate([qa, ka, qb[rot], kb[rot], va, ga, vb, gb, ra])


def _retention_tables():
    lg = np.log(1.0 - 2.0 ** (-5.0 - np.arange(RET_HEADS, dtype=np.float64)))
    idx = np.arange(CHUNK, dtype=np.float64)
    diff = idx[:, None] - idx[None, :]
    dmask = np.where(diff >= 0, np.exp(lg[:, None, None] * np.maximum(diff, 0.0)), 0.0)
    qdec = np.exp(lg[:, None] * (idx + 1.0))
    kdec = np.exp(lg[:, None] * (CHUNK - 1.0 - idx))
    cdec = np.exp(lg * CHUNK)
    per_col = lambda t: np.repeat(t.T, RET_DV, axis=1)
    lanes = np.arange(256)
    rows_head = np.repeat(np.arange(RET_HEADS), RET_DV)[:, None]
    half = RET_DK // 2
    hmask_g = (lanes[None, :] // GLA_DK) == rows_head
    hmask_r = ((lanes[None, :] % 128) // half == rows_head) & ((lanes[None, :] % 128) < RET_HEADS * half)
    return (jnp.asarray(dmask, F32), jnp.asarray(per_col(kdec), F32), jnp.asarray(per_col(qdec), F32),
            jnp.asarray(np.broadcast_to(np.repeat(cdec, RET_DV)[:, None], (RET_HEADS * RET_DV, 256)), F32),
            jnp.asarray(hmask_g, F32), jnp.asarray(hmask_r, F32))


def _peer_layer(h, nw, w_q, sub_keys, u_tab, v_tab, tm_route, tb):
    keys = sub_keys.reshape(2 * PEER_HEADS, PEER_NKEYS, -1)
    xn8, experts, gates = peer_route(h, nw, w_q.T, keys, tm_route)
    kk = np.arange(SUBLANES * PEER_HK) // SUBLANES
    sel = jnp.asarray(kk[:, None] == np.arange(PEER_HK)[None, :], BF16)
    w = peer_hidden(experts, xn8, gates, sel, _pack_table(u_tab), tb)
    return peer_out(experts, w, sel.T, _pack_table(v_tab), tb)


def _row(v):
    return v.reshape(1, -1).astype(F32)


def _layer0_mixer(h, w, bsz, seq):
    t = h.shape[0]
    w_in = w["ev_w_in"][0]
    w_in0 = jnp.pad(w_in[:, _even_in_perm()], ((0, 0), (0, EVEN_COLS - w_in.shape[1]))).astype(BF16)
    proj = norm_matmul(h, _row(w["norm_mix_w"][0]), w_in0, min(256, t))
    half = RET_DK // 2
    freqs = ROPE_BASE ** (-np.arange(half, dtype=np.float32) / half)
    dmask, kdec, qdec, cdec, hmask_g, hmask_r = _retention_tables()
    consts = (
        jnp.asarray(np.tile(freqs, RET_HEADS)[None, :], F32),
        jnp.pad(w["ev_gla_w_up"][0], ((0, 128 - GLA_GATE_RANK), (0, 0))).astype(BF16),
        _row(w["ev_gla_b_up"][0]), _row(w["ev_gla_norm_w"][0]), _row(w["ev_ret_norm_w"][0]),
        w["ev_w_out"][0].astype(BF16),
        jnp.asarray(np.tril(np.ones((CHUNK, CHUNK))), BF16),
        dmask, kdec, qdec, cdec, hmask_g, hmask_r,
    )
    pos = w["positions"].astype(F32).reshape(seq, 1)
    return even_mixer(h, proj, pos, consts, bsz, seq, min(256, seq))


def _layer1_mixer(h, w, bsz, seq):
    t = h.shape[0]
    u = norm_matmul(h, _row(w["norm_mix_w"][1]), w["od_w_in"][0].astype(BF16), min(512, t))
    pw_re, pw_im, bb_re, bb_im = s5_params(w["od_s5_a_re"][0], w["od_s5_a_im"][0], w["od_s5_log_dt"][0].reshape(-1, 1),
                                           w["od_s5_b_re"][0].transpose(0, 2, 1), w["od_s5_b_im"][0].transpose(0, 2, 1))
    eye = jnp.eye(S5_GROUPS, dtype=F32)
    blockdiag = lambda m: (m[:, :, None, :] * eye[:, None, :, None]).reshape(m.shape[0] * m.shape[1], -1)
    consts = (
        w["od_pool_w"][0].astype(BF16), _row(w["od_pool_scale"][0]),
        blockdiag(bb_re).astype(BF16), blockdiag(bb_im).astype(BF16),
        blockdiag(w["od_s5_c_re"][0]).T.astype(BF16), blockdiag(w["od_s5_c_im"][0]).T.astype(BF16),
        pw_re.reshape(SUBLANES, S5_STATE), pw_im.reshape(SUBLANES, S5_STATE),
        _row(w["od_s5_d"][0]), w["od_s5_w_glu"][0].astype(BF16), _row(w["od_s5_b_glu"][0]), w["od_w_out"][0].astype(BF16),
    )
    return odd_mixer(h, u, consts, bsz, seq, min(256, seq))


def kernel(x, p, positions, norm_mix_w, norm_ffn_w, norm_ple_w, final_norm_w, ev_w_in, ev_gla_w_up, ev_gla_b_up, ev_gla_norm_w, ev_ret_norm_w, ev_w_out, od_w_in, od_pool_w, od_pool_scale, od_s5_a_re, od_s5_a_im, od_s5_log_dt, od_s5_b_re, od_s5_b_im, od_s5_c_re, od_s5_c_im, od_s5_d, od_s5_w_glu, od_s5_b_glu, od_w_out, peer_w_q, peer_sub_keys, peer_u, peer_v, ple_w_proj, ple_w_gate):
    w = dict(locals())
    bsz, seq, d = x.shape
    t = bsz * seq
    tm_tok = min(512, t)
    tm_route = min(256, t)
    tb = min(512, t)
    h = x.reshape(t, d)
    for i, mixer in enumerate((_layer0_mixer, _layer1_mixer)):
        h = mixer(h, w, bsz, seq)
        e8 = _peer_layer(h, _row(norm_ffn_w[i]), peer_w_q[i], peer_sub_keys[i], peer_u[i], peer_v[i], tm_route, tb)
        h = ple(h, e8, p[i].reshape(t, -1), _row(norm_ple_w[i]), ple_w_gate[i].astype(BF16), ple_w_proj[i].astype(BF16),
                _row(final_norm_w), tm_tok, i == 1)
    return h.reshape(bsz, seq, d)
```

```python
import functools
import math

import numpy as np
import jax
import jax.numpy as jnp
from jax import lax
from jax.experimental import pallas as pl
from jax.experimental.pallas import tpu as pltpu

F32 = jnp.float32
BF16 = jnp.bfloat16

D_MODEL = 1024
NORM_EPS = 1e-6
CHUNK = 64
GLA_HEADS = 4
GLA_DK = 64
GLA_DV = 128
GLA_GATE_RANK = 16
GLA_GATE_NORM = 16.0
RET_HEADS = 4
RET_DK = 64
RET_DV = 128
ROPE_BASE = 10000.0
POOL_WINDOWS = (2, 4, 8, 16)
POOL_WIDTH = 512
POOL_GROUP_WIDTH = 128
S5_H = 16
S5_P = 64
S5_GROUPS = 32
S5_WIDTH = 512
S5_STATE = S5_GROUPS * S5_P
PEER_HEADS = 8
PEER_NKEYS = 128
PEER_TOPK = 16
PEER_HK = PEER_HEADS * PEER_TOPK
PEER_EXPERTS = PEER_NKEYS * PEER_NKEYS
EVEN_COLS = 3200

VMEM_LIMIT_BIG = 56 * 1024 * 1024
VMEM_LIMIT = 48 * 1024 * 1024
SUBLANES = 8

NT_DIMS = (((1,), (1,)), ((), ()))


def _split_bf16(x):
    hi = x.astype(BF16)
    lo = (x - hi.astype(F32)).astype(BF16)
    return hi, lo


def _rms(x, w):
    return x * lax.rsqrt(jnp.mean(x * x, axis=-1, keepdims=True) + NORM_EPS) * w


def _sigmoid(x):
    return 1.0 / (1.0 + jnp.exp(-x))


def _gelu(x):
    return 0.5 * x * (1.0 + lax.erf(x * (2.0 ** -0.5)))


def _log_sigmoid(z):
    return jnp.minimum(z, 0.0) - jnp.log1p(jnp.exp(-jnp.abs(z)))


def _norm_mm_kernel(h_ref, nw_ref, w_ref, o_ref):
    xn = _rms(h_ref[...], nw_ref[...])
    o_ref[...] = jnp.dot(xn.astype(BF16), w_ref[...], preferred_element_type=F32)


def norm_matmul(h, nw, w, tm):
    t, d = h.shape
    n = w.shape[1]
    return pl.pallas_call(
        _norm_mm_kernel,
        grid=(t // tm,),
        in_specs=[
            pl.BlockSpec((tm, d), lambda i: (i, 0)),
            pl.BlockSpec((1, d), lambda i: (0, 0)),
            pl.BlockSpec((d, n), lambda i: (0, 0)),
        ],
        out_specs=pl.BlockSpec((tm, n), lambda i: (i, 0)),
        out_shape=jax.ShapeDtypeStruct((t, n), F32),
        compiler_params=pltpu.CompilerParams(dimension_semantics=("parallel",), vmem_limit_bytes=VMEM_LIMIT),
        name="norm_matmul",
    )(h, nw, w)


def _rotary_table_kernel(pos_ref, freq_ref, cos_ref, sin_ref):
    ang = pos_ref[...] * freq_ref[...]
    cos_ref[...] = jnp.cos(ang)
    sin_ref[...] = jnp.sin(ang)


def rotary_table(pos, freqs, ts):
    seq = pos.shape[0]
    n = freqs.shape[1]
    return pl.pallas_call(
        _rotary_table_kernel,
        grid=(seq // ts,),
        in_specs=[pl.BlockSpec((ts, 1), lambda i: (i, 0)), pl.BlockSpec((1, n), lambda i: (0, 0))],
        out_specs=[pl.BlockSpec((ts, n), lambda i: (i, 0)), pl.BlockSpec((ts, n), lambda i: (i, 0))],
        out_shape=[jax.ShapeDtypeStruct((seq, n), F32), jax.ShapeDtypeStruct((seq, n), F32)],
        compiler_params=pltpu.CompilerParams(dimension_semantics=("parallel",)),
        name="rotary_table",
    )(pos, freqs)


def _even_mixer_kernel(h_ref, qk_ref, va_ref, ga_ref, vb_ref, gb_ref, ra_ref, cos_ref, sin_ref,
                       wup_ref, bup_ref, gnw_ref, rnw_ref, wout_ref, tril_ref,
                       dmask_ref, kdec_ref, qdec_ref, cdec_ref, hmaskg_ref, hmaskr_ref,
                       o_ref, gstate, rstate, y_scr, qin_scr, dec_scr, kvg_scr, kvr_scr):
    tm = h_ref.shape[0]

    @pl.when(pl.program_id(1) == 0)
    def _():
        gstate[...] = jnp.zeros_like(gstate)
        rstate[...] = jnp.zeros_like(rstate)

    lane = lax.broadcasted_iota(jnp.int32, (1, 256), 1)
    gla_masks = [((lane >= GLA_DK * h) & (lane < GLA_DK * (h + 1))).astype(F32) for h in range(GLA_HEADS)]
    half = RET_DK // 2
    ret_masks = [(((lane >= half * h) & (lane < half * (h + 1)))
                  | ((lane >= 128 + half * h) & (lane < 128 + half * (h + 1)))).astype(F32)
                 for h in range(RET_HEADS)]
    ri = lax.broadcasted_iota(jnp.int32, (CHUNK, CHUNK), 0)
    ci = lax.broadcasted_iota(jnp.int32, (CHUNK, CHUNK), 1)
    causal = ri >= ci
    tril = tril_ref[...]
    wup = wup_ref[...]
    bup = bup_ref[...]

    n_chunks = tm // CHUNK
    causal4 = jnp.concatenate([causal] * GLA_HEADS, axis=0)
    dmask4 = jnp.concatenate([dmask_ref[h] for h in range(RET_HEADS)], axis=0)
    hmask_g = hmaskg_ref[...]
    hmask_r = hmaskr_ref[...]

    for c in range(n_chunks):
        rows = slice(c * CHUNK, (c + 1) * CHUNK)
        qk = qk_ref[rows, :]
        qa, ka, qb, kb = qk[:, 0:256], qk[:, 256:512], qk[:, 512:768], qk[:, 768:1024]

        z = jnp.dot(ra_ref[rows, :].astype(BF16), wup, preferred_element_type=F32) + bup
        la = _log_sigmoid(z) * (1.0 / GLA_GATE_NORM)
        lah, lal = _split_bf16(la)
        g = (jnp.dot(tril, lah, preferred_element_type=F32)
             + jnp.dot(tril, lal, preferred_element_type=F32))
        g_last = g[CHUNK - 1:CHUNK, :]
        ref = 0.5 * g_last
        qs = qa * (GLA_DK ** -0.5)
        qt = qs * jnp.exp(g - ref)
        kt = (ka * jnp.exp(ref - g)).astype(BF16)
        kd = ka * jnp.exp(g_last - g)
        qin_scr[rows, 0:256] = qs * jnp.exp(g)
        dec_scr[c] = jnp.exp(g_last)
        q4 = jnp.concatenate([(qt * gla_masks[h]) for h in range(GLA_HEADS)], axis=0).astype(BF16)
        s4 = jnp.where(causal4, lax.dot_general(q4, kt, NT_DIMS, preferred_element_type=F32), 0.0).astype(BF16)
        va = va_ref[rows, :]
        vab = va.astype(BF16)
        for h in range(GLA_HEADS):
            y_scr[rows, GLA_DV * h:GLA_DV * (h + 1)] = jnp.dot(
                s4[CHUNK * h:CHUNK * (h + 1), :], vab[:, GLA_DV * h:GLA_DV * (h + 1)], preferred_element_type=F32)
        kvg_scr[c] = jnp.dot(va.T.astype(BF16), kd.astype(BF16), preferred_element_type=F32) * hmask_g

        cs, sn = cos_ref[rows, :], sin_ref[rows, :]
        q1, q2 = qb[:, 0:128], qb[:, 128:256]
        k1, k2 = kb[:, 0:128], kb[:, 128:256]
        qr = jnp.concatenate([q1 * cs - q2 * sn, q2 * cs + q1 * sn], axis=1)
        kr = jnp.concatenate([k1 * cs - k2 * sn, k2 * cs + k1 * sn], axis=1) * (RET_DK ** -0.5)
        krb = kr.astype(BF16)
        qin_scr[rows, 256:512] = qr
        q4 = jnp.concatenate([(qr * ret_masks[h]) for h in range(RET_HEADS)], axis=0).astype(BF16)
        s4 = (lax.dot_general(q4, krb, NT_DIMS, preferred_element_type=F32) * dmask4).astype(BF16)
        vb = vb_ref[rows, :]
        vbb = vb.astype(BF16)
        for h in range(RET_HEADS):
            y_scr[rows, 512 + RET_DV * h:512 + RET_DV * (h + 1)] = jnp.dot(
                s4[CHUNK * h:CHUNK * (h + 1), :], vbb[:, RET_DV * h:RET_DV * (h + 1)], preferred_element_type=F32)
        kvr_scr[c] = jnp.dot((vb * kdec_ref[...]).T.astype(BF16), krb, preferred_element_type=F32) * hmask_r

    for c in range(n_chunks):
        rows = slice(c * CHUNK, (c + 1) * CHUNK)
        sg = gstate[...]
        y_scr[rows, 0:512] += lax.dot_general(qin_scr[rows, 0:256].astype(BF16), sg.astype(BF16), NT_DIMS,
                                              preferred_element_type=F32)
        gstate[...] = sg * dec_scr[c] + kvg_scr[c]
        sr = rstate[...]
        y_scr[rows, 512:1024] += lax.dot_general(qin_scr[rows, 256:512].astype(BF16), sr.astype(BF16), NT_DIMS,
                                                 preferred_element_type=F32) * qdec_ref[...]
        rstate[...] = sr * cdec_ref[...] + kvr_scr[c]

    pieces = []
    gnw = gnw_ref[...]
    for h in range(GLA_HEADS):
        o = y_scr[:, GLA_DV * h:GLA_DV * (h + 1)]
        gt = ga_ref[:, GLA_DV * h:GLA_DV * (h + 1)]
        pieces.append(_rms(o, gnw) * (gt * _sigmoid(gt)))
    for h in range(RET_HEADS):
        o = y_scr[:, 512 + RET_DV * h:512 + RET_DV * (h + 1)]
        gt = gb_ref[:, RET_DV * h:RET_DV * (h + 1)]
        mu = jnp.mean(o, axis=-1, keepdims=True)
        oc = o - mu
        var = jnp.mean(oc * oc, axis=-1, keepdims=True)
        nrm = oc * lax.rsqrt(var + NORM_EPS) * rnw_ref[:, RET_DV * h:RET_DV * (h + 1)]
        pieces.append(nrm * (gt * _sigmoid(gt)))
    y = jnp.concatenate(pieces, axis=1).astype(BF16)
    o_ref[...] = h_ref[...] + jnp.dot(y, wout_ref[...], preferred_element_type=F32)


def even_mixer(h, proj, pos, consts, bsz, seq, tm):
    nj = seq // tm
    tok = lambda b, j: (b * nj + j, 0)
    col = lambda cb: (lambda b, j: (b * nj + j, cb))
    full2 = lambda b, j: (0, 0)
    full3 = lambda b, j: (0, 0, 0)
    (freqs, wup, bup, gnw, rnw, wout, tril, dmask, kdec, qdec, cdec, hmask_g, hmask_r) = consts
    cos_t, sin_t = rotary_table(pos, freqs, min(512, seq))
    n_state = GLA_HEADS * GLA_DV
    return pl.pallas_call(
        _even_mixer_kernel,
        grid=(bsz, nj),
        in_specs=[
            pl.BlockSpec((tm, D_MODEL), tok),
            pl.BlockSpec((tm, 1024), col(0)),
            pl.BlockSpec((tm, 512), col(2)),
            pl.BlockSpec((tm, 512), col(3)),
            pl.BlockSpec((tm, 512), col(4)),
            pl.BlockSpec((tm, 512), col(5)),
            pl.BlockSpec((tm, 128), col(24)),
            pl.BlockSpec((tm, 128), lambda b, j: (j, 0)),
            pl.BlockSpec((tm, 128), lambda b, j: (j, 0)),
            pl.BlockSpec((128, 256), full2),
            pl.BlockSpec((1, 256), full2),
            pl.BlockSpec((1, 128), full2),
            pl.BlockSpec((1, 512), full2),
            pl.BlockSpec((1024, D_MODEL), full2),
            pl.BlockSpec((CHUNK, CHUNK), full2),
            pl.BlockSpec((RET_HEADS, CHUNK, CHUNK), full3),
            pl.BlockSpec((CHUNK, n_state), full2),
            pl.BlockSpec((CHUNK, n_state), full2),
            pl.BlockSpec((n_state, 256), full2),
            pl.BlockSpec((n_state, 256), full2),
            pl.BlockSpec((n_state, 256), full2),
        ],
        out_specs=pl.BlockSpec((tm, D_MODEL), tok),
        out_shape=jax.ShapeDtypeStruct(h.shape, F32),
        scratch_shapes=[
            pltpu.VMEM((n_state, 256), F32),
            pltpu.VMEM((n_state, 256), F32),
            pltpu.VMEM((tm, 1024), F32),
            pltpu.VMEM((tm, 512), F32),
            pltpu.VMEM((tm // CHUNK, 1, 256), F32),
            pltpu.VMEM((tm // CHUNK, n_state, 256), F32),
            pltpu.VMEM((tm // CHUNK, n_state, 256), F32),
        ],
        compiler_params=pltpu.CompilerParams(dimension_semantics=("arbitrary", "arbitrary"),
                                             vmem_limit_bytes=VMEM_LIMIT),
        name="even_mixer",
    )(h, proj, proj, proj, proj, proj, proj, cos_t, sin_t, wup, bup, gnw, rnw, wout, tril, dmask, kdec, qdec, cdec,
      hmask_g, hmask_r)


def _s5_param_kernel(are_ref, aim_ref, ldt_ref, bre_ref, bim_ref,
                     pw_re_ref, pw_im_ref, bbre_ref, bbim_ref):
    a_re = are_ref[...]
    a_im = aim_ref[...]
    dt = jnp.exp(ldt_ref[...])
    for r in range(SUBLANES):
        mag = jnp.exp(a_re * dt * (r + 1.0))
        pw_re_ref[r] = mag * jnp.cos(a_im * dt * (r + 1.0))
        pw_im_ref[r] = mag * jnp.sin(a_im * dt * (r + 1.0))
    mag = jnp.exp(a_re * dt)
    abar_re, abar_im = mag * jnp.cos(a_im * dt), mag * jnp.sin(a_im * dt)
    den = a_re * a_re + a_im * a_im
    nr, ni = abar_re - 1.0, abar_im
    coef_re = (nr * a_re + ni * a_im) / den
    coef_im = (ni * a_re - nr * a_im) / den
    b_re = bre_ref[...]
    b_im = bim_ref[...]
    c_re = jnp.concatenate([coef_re] * S5_H, axis=1)
    c_im = jnp.concatenate([coef_im] * S5_H, axis=1)
    bbre_ref[...] = c_re * b_re - c_im * b_im
    bbim_ref[...] = c_re * b_im + c_im * b_re


def s5_params(a_re, a_im, log_dt, b_re_t, b_im_t):
    g, p = a_re.shape
    hh = b_re_t.shape[1]
    pw_re, pw_im, bb_re, bb_im = pl.pallas_call(
        _s5_param_kernel,
        out_shape=[jax.ShapeDtypeStruct((SUBLANES, g, p), F32), jax.ShapeDtypeStruct((SUBLANES, g, p), F32),
                   jax.ShapeDtypeStruct((g, hh * p), F32), jax.ShapeDtypeStruct((g, hh * p), F32)],
        name="s5_params",
    )(a_re, a_im, log_dt, b_re_t.reshape(g, hh * p), b_im_t.reshape(g, hh * p))
    return pw_re, pw_im, bb_re.reshape(g, hh, p), bb_im.reshape(g, hh, p)


def _odd_mixer_kernel(h_ref, u_ref, poolw_ref, pscale_ref, wbre_ref, wbim_ref, wcre_ref, wcim_ref,
                      pwre_ref, pwim_ref, dskip_ref, wglu_ref, bglu_ref, wout_ref,
                      o_ref, tail_scr, car_re, car_im, xre_scr, xim_scr):
    tm = h_ref.shape[0]
    j = pl.program_id(1)
    halo = POOL_WINDOWS[-1]

    @pl.when(j == 0)
    def _():
        tail_scr[...] = jnp.zeros_like(tail_scr)
        car_re[...] = jnp.zeros_like(car_re)
        car_im[...] = jnp.zeros_like(car_im)

    uc = u_ref[:, 0:POOL_WIDTH]
    ud = u_ref[:, POOL_WIDTH:POOL_WIDTH + S5_WIDTH]

    ext = jnp.concatenate([tail_scr[...], uc], axis=0)
    tail_scr[...] = uc[tm - halo:tm, :]
    pos = (j * tm + lax.broadcasted_iota(jnp.int32, (tm, 1), 0)).astype(F32)
    mixed = []
    for gi, win in enumerate(POOL_WINDOWS):
        a = ext[:, POOL_GROUP_WIDTH * gi:POOL_GROUP_WIDTH * (gi + 1)]
        n = tm + halo
        step = 1
        end = 0
        while step < win:
            a = a[step:n, :] + a[0:n - step, :]
            n -= step
            end += step
            step *= 2
        wsum = a[halo - end:halo - end + tm, :]
        cnt = jnp.minimum(pos + 1.0, float(win))
        pooled = wsum / cnt - uc[:, POOL_GROUP_WIDTH * gi:POOL_GROUP_WIDTH * (gi + 1)]
        mixed.append(jnp.dot(pooled.astype(BF16), poolw_ref[gi], preferred_element_type=F32))
    y_c = jnp.concatenate(mixed, axis=1) * pscale_ref[...]

    udb = ud.astype(BF16)
    hw, hs = S5_WIDTH // 2, S5_STATE // 2
    for hf in range(2):
        cols = slice(hs * hf, hs * (hf + 1))
        uh = udb[:, hw * hf:hw * (hf + 1)]
        xre_scr[:, cols] = jnp.dot(uh, wbre_ref[hw * hf:hw * (hf + 1), cols], preferred_element_type=F32)
        xim_scr[:, cols] = jnp.dot(uh, wbim_ref[hw * hf:hw * (hf + 1), cols], preferred_element_type=F32)
    rowi = lax.broadcasted_iota(jnp.int32, (SUBLANES, S5_STATE), 0)
    pw_re = pwre_ref[...]
    pw_im = pwim_ref[...]
    step_pw = [(jnp.where(rowi >= d, pw_re[d - 1:d, :], 0.0), jnp.where(rowi >= d, pw_im[d - 1:d, :], 0.0))
               for d in (1, 2, 4)]

    def slab(s, carry):
        cr, ci = carry
        rows = pl.ds(pl.multiple_of(s * SUBLANES, SUBLANES), SUBLANES)
        xr = xre_scr[rows, :]
        xi = xim_scr[rows, :]
        for dsh, (pr, pi) in zip((1, 2, 4), step_pw):
            sr = pltpu.roll(xr, dsh, axis=0)
            si = pltpu.roll(xi, dsh, axis=0)
            xr, xi = xr + (pr * sr - pi * si), xi + (pr * si + pi * sr)
        xr, xi = xr + (pw_re * cr - pw_im * ci), xi + (pw_re * ci + pw_im * cr)
        xre_scr[rows, :] = xr
        xim_scr[rows, :] = xi
        return xr[SUBLANES - 1:SUBLANES, :], xi[SUBLANES - 1:SUBLANES, :]

    cr, ci = lax.fori_loop(0, tm // SUBLANES, slab, (car_re[...], car_im[...]))
    car_re[...] = cr
    car_im[...] = ci

    yh = []
    for hf in range(2):
        rws = slice(hs * hf, hs * (hf + 1))
        cls = slice(hw * hf, hw * (hf + 1))
        yh.append(jnp.dot(xre_scr[:, rws].astype(BF16), wcre_ref[rws, cls], preferred_element_type=F32)
                  - jnp.dot(xim_scr[:, rws].astype(BF16), wcim_ref[rws, cls], preferred_element_type=F32))
    y = jnp.concatenate(yh, axis=1) + dskip_ref[...] * ud
    z = _gelu(y)
    y_d = z * _sigmoid(jnp.dot(z.astype(BF16), wglu_ref[...], preferred_element_type=F32) + bglu_ref[...])

    ycat = jnp.concatenate([y_c, y_d], axis=1).astype(BF16)
    o_ref[...] = h_ref[...] + jnp.dot(ycat, wout_ref[...], preferred_element_type=F32)


def odd_mixer(h, u, consts, bsz, seq, tm):
    nj = seq // tm
    tok = lambda b, j: (b * nj + j, 0)
    full2 = lambda b, j: (0, 0)
    full3 = lambda b, j: (0, 0, 0)
    (poolw, pscale, wbre, wbim, wcre, wcim, pwre, pwim, dskip, wglu, bglu, wout) = consts
    return pl.pallas_call(
        _odd_mixer_kernel,
        grid=(bsz, nj),
        in_specs=[
            pl.BlockSpec((tm, D_MODEL), tok),
            pl.BlockSpec((tm, 1024), tok),
            pl.BlockSpec((4, 128, 128), full3),
            pl.BlockSpec((1, POOL_WIDTH), full2),
            pl.BlockSpec((S5_WIDTH, S5_STATE), full2),
            pl.BlockSpec((S5_WIDTH, S5_STATE), full2),
            pl.BlockSpec((S5_STATE, S5_WIDTH), full2),
            pl.BlockSpec((S5_STATE, S5_WIDTH), full2),
            pl.BlockSpec((SUBLANES, S5_STATE), full2),
            pl.BlockSpec((SUBLANES, S5_STATE), full2),
            pl.BlockSpec((1, S5_WIDTH), full2),
            pl.BlockSpec((S5_WIDTH, S5_WIDTH), full2),
            pl.BlockSpec((1, S5_WIDTH), full2),
            pl.BlockSpec((1024, D_MODEL), full2),
        ],
        out_specs=pl.BlockSpec((tm, D_MODEL), tok),
        out_shape=jax.ShapeDtypeStruct(h.shape, F32),
        scratch_shapes=[
            pltpu.VMEM((POOL_WINDOWS[-1], POOL_WIDTH), F32),
            pltpu.VMEM((1, S5_STATE), F32),
            pltpu.VMEM((1, S5_STATE), F32),
            pltpu.VMEM((tm, S5_STATE), F32),
            pltpu.VMEM((tm, S5_STATE), F32),
        ],
        compiler_params=pltpu.CompilerParams(dimension_semantics=("arbitrary", "arbitrary"),
                                             vmem_limit_bytes=VMEM_LIMIT),
        name="odd_mixer",
    )(h, u, poolw, pscale, wbre, wbim, wcre, wcim, pwre, pwim, dskip, wglu, bglu, wout)


def _top16_rows(s, ids, id_bound):
    vals, idxs = [], []
    for _ in range(PEER_TOPK):
        m = jnp.max(s, axis=0, keepdims=True)
        am = jnp.min(jnp.where(s == m, ids, float(id_bound)), axis=0, keepdims=True)
        vals.append(m)
        idxs.append(am)
        s = jnp.where(ids == am, -jnp.inf, s)
    return jnp.concatenate(vals, axis=0), jnp.concatenate(idxs, axis=0)


def _batcher_pairs(lo, hi):
    def merge(lo, hi, r):
        step = 2 * r
        if step < hi - lo:
            yield from merge(lo, hi, step)
            yield from merge(lo + r, hi, step)
            yield from ((i, i + r) for i in range(lo + r, hi - r, step))
        else:
            yield (lo, lo + r)

    if hi > lo:
        mid = lo + (hi - lo) // 2
        yield from _batcher_pairs(lo, mid)
        yield from _batcher_pairs(mid + 1, hi)
        yield from merge(lo, hi, 1)


_SORT16 = tuple(_batcher_pairs(0, PEER_TOPK - 1))
_BITONIC16 = tuple((i, i + d) for d in (8, 4, 2, 1) for i in range(PEER_TOPK) if not i & d)


def _top16_network(s):
    n = PEER_TOPK
    v = [s[SUBLANES * i:SUBLANES * (i + 1), :] for i in range(n)]
    sub = lax.broadcasted_iota(jnp.int32, (SUBLANES, s.shape[1]), 0).astype(F32)
    k = [sub + float(SUBLANES * i) for i in range(n)]

    def exchange(i, j):
        swap = v[j] > v[i]
        v[i], v[j] = jnp.where(swap, v[j], v[i]), jnp.where(swap, v[i], v[j])
        k[i], k[j] = jnp.where(swap, k[j], k[i]), jnp.where(swap, k[i], k[j])

    for i, j in _SORT16:
        exchange(i, j)
    for shift in (4, 2, 1):
        bv = [pltpu.roll(x, shift, axis=0) for x in v]
        bk = [pltpu.roll(x, shift, axis=0) for x in k]
        for i in range(n):
            take = bv[n - 1 - i] > v[i]
            v[i] = jnp.where(take, bv[n - 1 - i], v[i])
            k[i] = jnp.where(take, bk[n - 1 - i], k[i])
        for i, j in _BITONIC16:
            exchange(i, j)
    tie = jnp.zeros_like(v[0])
    for i in range(n - 1):
        tie = jnp.where(v[i] == v[i + 1], 1.0, tie)
    cnt = jnp.zeros_like(v[0])
    for i in range(n):
        cnt = cnt + jnp.where(s[SUBLANES * i:SUBLANES * (i + 1), :] >= v[n - 1], 1.0, 0.0)
    for shift in (4, 2, 1):
        cnt = cnt + pltpu.roll(cnt, shift, axis=0)
    tie = jnp.where(cnt > float(n), 1.0, tie)
    return (jnp.concatenate([x[0:1, :] for x in v], axis=0), jnp.concatenate([x[0:1, :] for x in k], axis=0), tie)


_PAIR_BLOCKS = (("b", 0, 0), ("b", 1, 0), ("b", 2, 0), ("b", 3, 0), ("b", 4, 0),
                ("a", 0, 8), ("a", 0, 0), ("a", 1, 0), ("b", 0, 8))
_PAIR_ID_BOUND = 4 * PEER_TOPK * PEER_TOPK


def _pair_block_ids(tm):
    r = lax.broadcasted_iota(jnp.int32, (SUBLANES, tm), 0).astype(F32)
    seen = set()
    out = []
    for side, fixed, start in _PAIR_BLOCKS:
        ids = jnp.zeros((SUBLANES, tm), F32)
        for q in range(SUBLANES):
            i, j = (fixed, start + q) if side == "a" else (start + q, fixed)
            ok = (i + 1) * (j + 1) <= PEER_TOPK and (i, j) not in seen
            seen.add((i, j))
            ids = jnp.where(r == q, float(i * PEER_TOPK + j if ok else _PAIR_ID_BOUND + len(seen)), ids)
        out.append(ids)
    assert len({p for p in seen if (p[0] + 1) * (p[1] + 1) <= PEER_TOPK}) == 50
    return jnp.concatenate(out, axis=0)


def _pair_block_sums(av, bv):
    out = []
    for side, fixed, start in _PAIR_BLOCKS:
        if side == "a":
            out.append(av[fixed:fixed + 1, :] + bv[start:start + SUBLANES, :])
        else:
            out.append(av[start:start + SUBLANES, :] + bv[fixed:fixed + 1, :])
    return jnp.concatenate(out, axis=0)


def _take16(table, sel):
    out = jnp.zeros(sel.shape, table.dtype)
    for i in range(PEER_TOPK):
        out = jnp.where(sel == i, table[i:i + 1, :], out)
    return out


def _dot3(ah, al, bh, bl, dims):
    return (lax.dot_general(ah, bh, dims, preferred_element_type=F32)
            + lax.dot_general(al, bh, dims, preferred_element_type=F32)
            + lax.dot_general(ah, bl, dims, preferred_element_type=F32))


def _peer_route_kernel(h_ref, nw_ref, wqh_ref, wql_ref, kh_ref, kl_ref, xn_ref, exp_ref, gate_ref,
                       qt_scr, et_scr, gt_scr):
    xn = _rms(h_ref[...], nw_ref[...])
    for r in range(SUBLANES):
        xn_ref[:, r, :] = xn[:, 128 * r:128 * (r + 1)]
    xh, xl = _split_bf16(xn)
    qt_scr[...] = _dot3(wqh_ref[...], wql_ref[...], xh, xl, NT_DIMS)
    mm = (((1,), (0,)), ((), ()))
    tm = h_ref.shape[0]
    key_ids = lax.broadcasted_iota(jnp.int32, (PEER_NKEYS, tm), 0).astype(F32)
    pair_ids = _pair_block_ids(tm)
    pair_ok = pair_ids < float(_PAIR_ID_BOUND)

    def head(hd, tie, exact):
        ra = pl.ds(pl.multiple_of(hd * 256, 256), 128)
        rb = pl.ds(pl.multiple_of(hd * 256 + 128, 128), 128)
        qah, qal = _split_bf16(qt_scr[ra, :])
        qbh, qbl = _split_bf16(qt_scr[rb, :])
        sa = _dot3(kh_ref[2 * hd], kl_ref[2 * hd], qah, qal, mm)
        sb = _dot3(kh_ref[2 * hd + 1], kl_ref[2 * hd + 1], qbh, qbl, mm)
        if exact:
            av, ai = _top16_rows(sa, key_ids, PEER_NKEYS)
            bv, bi = _top16_rows(sb, key_ids, PEER_NKEYS)
        else:
            av, ai, ta = _top16_network(sa)
            bv, bi, tb_ = _top16_network(sb)
            tie = jnp.maximum(tie, jnp.maximum(ta, tb_))
        cand = jnp.where(pair_ok, _pair_block_sums(av, bv), -jnp.inf)
        cv, flat = _top16_rows(cand, pair_ids, _PAIR_ID_BOUND)
        flat = flat.astype(jnp.int32)
        e_a = _take16(ai, flat >> 4)
        e_b = _take16(bi, flat & (PEER_TOPK - 1))
        ex = jnp.exp(cv - cv[0:1, :])
        rows = pl.ds(pl.multiple_of(hd * PEER_TOPK, PEER_TOPK), PEER_TOPK)
        et_scr[rows, :] = (e_a * float(PEER_NKEYS) + e_b) * float(PEER_ROW_WORDS)
        gt_scr[rows, :] = ex / jnp.sum(ex, axis=0, keepdims=True)
        return tie

    def head_pair(i, tie):
        return head(2 * i + 1, head(2 * i, tie, exact=False), exact=False)

    tie = lax.fori_loop(0, PEER_HEADS // 2, head_pair, jnp.zeros((SUBLANES, tm), F32))

    @pl.when(jnp.max(tie) > 0.0)
    def _():
        lax.fori_loop(0, PEER_HEADS, functools.partial(head, exact=True), jnp.zeros((SUBLANES, tm), F32))

    exp_ref[...] = et_scr[...].T.astype(jnp.int32)
    gate_ref[...] = gt_scr[...].T


def peer_route(h, nw, wqt, keys, tm):
    t, d = h.shape
    nq = wqt.shape[0]
    wqh, wql = _split_bf16(wqt)
    kh, kl = _split_bf16(keys)
    return pl.pallas_call(
        _peer_route_kernel,
        grid=(t // tm,),
        in_specs=[
            pl.BlockSpec((tm, d), lambda i: (i, 0)),
            pl.BlockSpec((1, d), lambda i: (0, 0)),
            pl.BlockSpec((nq, d), lambda i: (0, 0)),
            pl.BlockSpec((nq, d), lambda i: (0, 0)),
            pl.BlockSpec((2 * PEER_HEADS, PEER_NKEYS, 128), lambda i: (0, 0, 0)),
            pl.BlockSpec((2 * PEER_HEADS, PEER_NKEYS, 128), lambda i: (0, 0, 0)),
        ],
        out_specs=[
            pl.BlockSpec((tm, SUBLANES, d // SUBLANES), lambda i: (i, 0, 0)),
            pl.BlockSpec((tm, PEER_HK), lambda i: (i, 0)),
            pl.BlockSpec((tm, PEER_HK), lambda i: (i, 0)),
        ],
        out_shape=[jax.ShapeDtypeStruct((t, SUBLANES, d // SUBLANES), F32),
                   jax.ShapeDtypeStruct((t, PEER_HK), jnp.int32), jax.ShapeDtypeStruct((t, PEER_HK), F32)],
        scratch_shapes=[pltpu.VMEM((nq, tm), F32), pltpu.VMEM((PEER_HK, tm), F32), pltpu.VMEM((PEER_HK, tm), F32)],
        compiler_params=pltpu.CompilerParams(dimension_semantics=("parallel",), vmem_limit_bytes=VMEM_LIMIT),
        name="peer_route",
    )(h, nw, wqh, wql, kh, kl)


PEER_GROUP = 8
PEER_ROW_WORDS = 4


def _gather_group(idx_ref, tab_ref, g, stage_ref):
    rows = [idx_ref.at[g * PEER_GROUP + j] for j in range(PEER_GROUP)]
    for k in range(PEER_HK):
        for j in range(PEER_GROUP):
            off = pl.multiple_of(rows[j][k], PEER_ROW_WORDS)
            stage_ref[j, pl.ds(PEER_ROW_WORDS * k, PEER_ROW_WORDS), :] = tab_ref[pl.ds(off, PEER_ROW_WORDS), :]


def _gather_compute_pipeline(n_groups, idx_ref, tab_ref, consume, stage_a, stage_b):
    def compute(g, stage_ref):
        for j in range(PEER_GROUP):
            consume(g * PEER_GROUP + j, stage_ref.at[j])

    _gather_group(idx_ref, tab_ref,0, stage_a)

    def body(i, carry):
        compute(2 * i, stage_a)
        _gather_group(idx_ref, tab_ref, 2 * i + 1, stage_b)
        compute(2 * i + 1, stage_b)
        _gather_group(idx_ref, tab_ref, 2 * i + 2, stage_a)
        return carry

    lax.fori_loop(0, n_groups // 2 - 1, body, 0)
    _gather_group(idx_ref, tab_ref,n_groups - 1, stage_b)
    compute(n_groups - 2, stage_a)
    compute(n_groups - 1, stage_b)


def _diag_mask():
    row = lax.broadcasted_iota(jnp.int32, (SUBLANES, SUBLANES * PEER_HK), 0)
    lane = lax.broadcasted_iota(jnp.int32, (SUBLANES, SUBLANES * PEER_HK), 1)
    return (lane & (SUBLANES - 1)) == row


def _peer_hidden_kernel(idx_ref, x_ref, g_ref, sel_ref, tab_ref, o_ref, stage_a, stage_b, part_scr):
    tb = x_ref.shape[0]
    diag = _diag_mask()

    def consume(t, rows_ref):
        u = pltpu.bitcast(rows_ref[...], BF16)
        xh, xl = _split_bf16(x_ref[t])
        x16 = jnp.concatenate([xh, xl], axis=0)
        out = lax.dot_general(x16, u, NT_DIMS, preferred_element_type=F32)
        o8 = out[0:SUBLANES] + out[SUBLANES:2 * SUBLANES]
        part_scr[pl.ds(t, 1), :] = jnp.sum(jnp.where(diag, o8, 0.0), axis=0, keepdims=True)

    _gather_compute_pipeline(tb // PEER_GROUP, idx_ref, tab_ref, consume, stage_a, stage_b)
    ph, plo = _split_bf16(part_scr[...])
    sel = sel_ref[...]
    hid = jnp.dot(ph, sel, preferred_element_type=F32) + jnp.dot(plo, sel, preferred_element_type=F32)
    o_ref[...] = g_ref[...] * _gelu(hid)


def peer_hidden(idx, x8, gates, sel, tab, tb):
    t = idx.shape[0]
    return pl.pallas_call(
        _peer_hidden_kernel,
        grid=(t // tb,),
        in_specs=[
            pl.BlockSpec((tb, PEER_HK), lambda i: (i, 0), memory_space=pltpu.SMEM),
            pl.BlockSpec((tb, SUBLANES, 128), lambda i: (i, 0, 0)),
            pl.BlockSpec((tb, PEER_HK), lambda i: (i, 0)),
            pl.BlockSpec((SUBLANES * PEER_HK, PEER_HK), lambda i: (0, 0)),
            pl.BlockSpec((PEER_ROW_WORDS * PEER_EXPERTS, 128), lambda i: (0, 0), pipeline_mode=pl.Buffered(1)),
        ],
        out_specs=pl.BlockSpec((tb, PEER_HK), lambda i: (i, 0)),
        out_shape=jax.ShapeDtypeStruct((t, PEER_HK), F32),
        scratch_shapes=[pltpu.VMEM((PEER_GROUP, PEER_ROW_WORDS * PEER_HK, 128), jnp.int32),
                        pltpu.VMEM((PEER_GROUP, PEER_ROW_WORDS * PEER_HK, 128), jnp.int32),
                        pltpu.VMEM((tb, SUBLANES * PEER_HK), F32)],
        compiler_params=pltpu.CompilerParams(dimension_semantics=("arbitrary",), vmem_limit_bytes=VMEM_LIMIT_BIG),
        name="peer_hidden",
    )(idx,x8, gates, sel, tab)


def _peer_out_kernel(idx_ref, w_ref, exp_ref, tab_ref, o_ref, stage_a, stage_b, wexp_scr):
    tb = w_ref.shape[0]
    diag = _diag_mask()
    wh, wl = _split_bf16(w_ref[...])
    ex = exp_ref[...]
    wexp_scr[...] = jnp.dot(wh, ex, preferred_element_type=F32) + jnp.dot(wl, ex, preferred_element_type=F32)

    def consume(t, rows_ref):
        v = pltpu.bitcast(rows_ref[...], BF16)
        w8 = jnp.where(diag, jnp.broadcast_to(wexp_scr[pl.ds(t, 1), :], (SUBLANES, SUBLANES * PEER_HK)), 0.0)
        w8h, w8l = _split_bf16(w8)
        w16 = jnp.concatenate([w8h, w8l], axis=0)
        out = jnp.dot(w16, v, preferred_element_type=F32)
        o_ref[t] = out[0:SUBLANES] + out[SUBLANES:2 * SUBLANES]

    _gather_compute_pipeline(tb // PEER_GROUP, idx_ref, tab_ref, consume, stage_a, stage_b)


def peer_out(idx, w, expand, tab, tb):
    t = idx.shape[0]
    return pl.pallas_call(
        _peer_out_kernel,
        grid=(t // tb,),
        in_specs=[
            pl.BlockSpec((tb, PEER_HK), lambda i: (i, 0), memory_space=pltpu.SMEM),
            pl.BlockSpec((tb, PEER_HK), lambda i: (i, 0)),
            pl.BlockSpec((PEER_HK, SUBLANES * PEER_HK), lambda i: (0, 0)),
            pl.BlockSpec((PEER_ROW_WORDS * PEER_EXPERTS, 128), lambda i: (0, 0), pipeline_mode=pl.Buffered(1)),
        ],
        out_specs=pl.BlockSpec((tb, SUBLANES, 128), lambda i: (i, 0, 0)),
        out_shape=jax.ShapeDtypeStruct((t, SUBLANES, 128), F32),
        scratch_shapes=[pltpu.VMEM((PEER_GROUP, PEER_ROW_WORDS * PEER_HK, 128), jnp.int32),
                        pltpu.VMEM((PEER_GROUP, PEER_ROW_WORDS * PEER_HK, 128), jnp.int32),
                        pltpu.VMEM((tb, SUBLANES * PEER_HK), F32)],
        compiler_params=pltpu.CompilerParams(dimension_semantics=("arbitrary",), vmem_limit_bytes=VMEM_LIMIT_BIG),
        name="peer_out",
    )(idx,w, expand, tab)


def _ple_kernel(h_ref, e8_ref, p_ref, nw_ref, wg_ref, wp_ref, fw_ref, o_ref, *, final_norm):
    h = h_ref[...] + jnp.concatenate([e8_ref[:, r, :] for r in range(SUBLANES)], axis=1)
    gate = _sigmoid(jnp.dot(_rms(h, nw_ref[...]).astype(BF16), wg_ref[...], preferred_element_type=F32))
    out = h + jnp.dot(p_ref[...].astype(BF16), wp_ref[...], preferred_element_type=F32) * gate
    if final_norm:
        out = _rms(out, fw_ref[...])
    o_ref[...] = out


def ple(h, e8, p, nw, wg, wp, fw, tm, final_norm):
    t, d = h.shape
    pd = p.shape[1]
    return pl.pallas_call(
        functools.partial(_ple_kernel, final_norm=final_norm),
        grid=(t // tm,),
        in_specs=[
            pl.BlockSpec((tm, d), lambda i: (i, 0)),
            pl.BlockSpec((tm, SUBLANES, d // SUBLANES), lambda i: (i, 0, 0)),
            pl.BlockSpec((tm, pd), lambda i: (i, 0)),
            pl.BlockSpec((1, d), lambda i: (0, 0)),
            pl.BlockSpec((d, d), lambda i: (0, 0)),
            pl.BlockSpec((pd, d), lambda i: (0, 0)),
            pl.BlockSpec((1, d), lambda i: (0, 0)),
        ],
        out_specs=pl.BlockSpec((tm, d), lambda i: (i, 0)),
        out_shape=jax.ShapeDtypeStruct((t, d), F32),
        compiler_params=pltpu.CompilerParams(dimension_semantics=("parallel",), vmem_limit_bytes=VMEM_LIMIT),
        name="ple",
    )(h, e8, p, nw, wg, wp, fw)


def _pack_table_kernel(t_ref, o_ref):
    x = t_ref[...]
    te = x.shape[0]
    for s in range(PEER_ROW_WORDS):
        lo = pltpu.bitcast(x[:, 256 * s:256 * s + 128].astype(BF16).astype(F32), jnp.int32)
        hi = pltpu.bitcast(x[:, 256 * s + 128:256 * s + 256].astype(BF16).astype(F32), jnp.int32)
        o_ref[pl.ds(s, te, stride=PEER_ROW_WORDS), :] = (hi & jnp.int32(-65536)) | lax.shift_right_logical(lo, 16)


def _pack_table(tab):
    e, d = tab.shape
    te = min(512, e)
    return pl.pallas_call(
        _pack_table_kernel,
        grid=(e // te,),
        in_specs=[pl.BlockSpec((te, d), lambda i: (i, 0))],
        out_specs=pl.BlockSpec((PEER_ROW_WORDS * te, 128), lambda i: (i, 0)),
        out_shape=jax.ShapeDtypeStruct((PEER_ROW_WORDS * e, 128), jnp.int32),
        compiler_params=pltpu.CompilerParams(dimension_semantics=("parallel",), vmem_limit_bytes=VMEM_LIMIT),
        name="pack_table",
    )(tab)


def _even_in_perm():
    off = np.cumsum([0, 256, 256, 512, 512, 16, 256, 256, 512, 512])
    qa, ka, va, ga, ra, qb, kb, vb, gb = [np.arange(off[i], off[i + 1]) for i in range(9)]
    half = RET_DK // 2
    rot = np.concatenate([np.concatenate([np.arange(h * RET_DK, h * RET_DK + half) for h in range(RET_HEADS)]),
                          np.concatenate([np.arange(h * RET_DK + half, (h + 1) * RET_DK) for h in range(RET_HEADS)])])
    return np.concatenate([qa, ka, qb[rot], kb[rot], va, ga, vb, gb, ra])


def _retention_tables():
    lg = np.log(1.0 - 2.0 ** (-5.0 - np.arange(RET_HEADS, dtype=np.float64)))
    idx = np.arange(CHUNK, dtype=np.float64)
    diff = idx[:, None] - idx[None, :]
    dmask = np.where(diff >= 0, np.exp(lg[:, None, None] * np.maximum(diff, 0.0)), 0.0)
    qdec = np.exp(lg[:, None] * (idx + 1.0))
    kdec = np.exp(lg[:, None] * (CHUNK - 1.0 - idx))
    cdec = np.exp(lg * CHUNK)
    per_col = lambda t: np.repeat(t.T, RET_DV, axis=1)
    lanes = np.arange(256)
    rows_head = np.repeat(np.arange(RET_HEADS), RET_DV)[:, None]
    half = RET_DK // 2
    hmask_g = (lanes[None, :] // GLA_DK) == rows_head
    hmask_r = ((lanes[None, :] % 128) // half == rows_head) & ((lanes[None, :] % 128) < RET_HEADS * half)
    return (jnp.asarray(dmask, F32), jnp.asarray(per_col(kdec), F32), jnp.asarray(per_col(qdec), F32),
            jnp.asarray(np.broadcast_to(np.repeat(cdec, RET_DV)[:, None], (RET_HEADS * RET_DV, 256)), F32),
            jnp.asarray(hmask_g, F32), jnp.asarray(hmask_r, F32))


def _peer_layer(h, nw, w_q, sub_keys, u_tab, v_tab, tm_route, tb):
    keys = sub_keys.reshape(2 * PEER_HEADS, PEER_NKEYS, -1)
    xn8, experts, gates = peer_route(h, nw, w_q.T, keys, tm_route)
    kk = np.arange(SUBLANES * PEER_HK) // SUBLANES
    sel = jnp.asarray(kk[:, None] == np.arange(PEER_HK)[None, :], BF16)
    w = peer_hidden(experts, xn8, gates, sel, _pack_table(u_tab), tb)
    return peer_out(experts, w, sel.T, _pack_table(v_tab), tb)


def _row(v):
    return v.reshape(1, -1).astype(F32)


def _layer0_mixer(h, w, bsz, seq):
    t = h.shape[0]
    w_in = w["ev_w_in"][0]
    w_in0 = jnp.pad(w_in[:, _even_in_perm()], ((0, 0), (0, EVEN_COLS - w_in.shape[1]))).astype(BF16)
    proj = norm_matmul(h, _row(w["norm_mix_w"][0]), w_in0, min(256, t))
    half = RET_DK // 2
    freqs = ROPE_BASE ** (-np.arange(half, dtype=np.float32) / half)
    dmask, kdec, qdec, cdec, hmask_g, hmask_r = _retention_tables()
    consts = (
        jnp.asarray(np.tile(freqs, RET_HEADS)[None, :], F32),
        jnp.pad(w["ev_gla_w_up"][0], ((0, 128 - GLA_GATE_RANK), (0, 0))).astype(BF16),
        _row(w["ev_gla_b_up"][0]), _row(w["ev_gla_norm_w"][0]), _row(w["ev_ret_norm_w"][0]),
        w["ev_w_out"][0].astype(BF16),
        jnp.asarray(np.tril(np.ones((CHUNK, CHUNK))), BF16),
        dmask, kdec, qdec, cdec, hmask_g, hmask_r,
    )
    pos = w["positions"].astype(F32).reshape(seq, 1)
    return even_mixer(h, proj, pos, consts, bsz, seq, min(256, seq))


def _layer1_mixer(h, w, bsz, seq):
    t = h.shape[0]
    u = norm_matmul(h, _row(w["norm_mix_w"][1]), w["od_w_in"][0].astype(BF16), min(512, t))
    pw_re, pw_im, bb_re, bb_im = s5_params(w["od_s5_a_re"][0], w["od_s5_a_im"][0], w["od_s5_log_dt"][0].reshape(-1, 1),
                                           w["od_s5_b_re"][0].transpose(0, 2, 1), w["od_s5_b_im"][0].transpose(0, 2, 1))
    eye = jnp.eye(S5_GROUPS, dtype=F32)
    blockdiag = lambda m: (m[:, :, None, :] * eye[:, None, :, None]).reshape(m.shape[0] * m.shape[1], -1)
    consts = (
        w["od_pool_w"][0].astype(BF16), _row(w["od_pool_scale"][0]),
        blockdiag(bb_re).astype(BF16), blockdiag(bb_im).astype(BF16),
        blockdiag(w["od_s5_c_re"][0]).T.astype(BF16), blockdiag(w["od_s5_c_im"][0]).T.astype(BF16),
        pw_re.reshape(SUBLANES, S5_STATE), pw_im.reshape(SUBLANES, S5_STATE),
        _row(w["od_s5_d"][0]), w["od_s5_w_glu"][0].astype(BF16), _row(w["od_s5_b_glu"][0]), w["od_w_out"][0].astype(BF16),
    )
    return odd_mixer(h, u, consts, bsz, seq, min(256, seq))


def kernel(x, p, positions, norm_mix_w, norm_ffn_w, norm_ple_w, final_norm_w, ev_w_in, ev_gla_w_up, ev_gla_b_up, ev_gla_norm_w, ev_ret_norm_w, ev_w_out, od_w_in, od_pool_w, od_pool_scale, od_s5_a_re, od_s5_a_im, od_s5_log_dt, od_s5_b_re, od_s5_b_im, od_s5_c_re, od_s5_c_im, od_s5_d, od_s5_w_glu, od_s5_b_glu, od_w_out, peer_w_q, peer_sub_keys, peer_u, peer_v, ple_w_proj, ple_w_gate):
    w = dict(locals())
    bsz, seq, d = x.shape
    t = bsz * seq
    tm_tok = min(512, t)
    tm_route = min(256, t)
    tb = min(512, t)
    h = x.reshape(t, d)
    for i, mixer in enumerate((_layer0_mixer, _layer1_mixer)):
        h = mixer(h, w, bsz, seq)
        e8 = _peer_layer(h, _row(norm_ffn_w[i]), peer_w_q[i], peer_sub_keys[i], peer_u[i], peer_v[i], tm_route, tb)
        h = ple(h, e8, p[i].reshape(t, -1), _row(norm_ple_w[i]), ple_w_gate[i].astype(BF16), ple_w_proj[i].astype(BF16),
                _row(final_norm_w), tm_tok, i == 1)
    return h.reshape(bsz, seq, d)
```

```python
import functools
import math

import numpy as np
import jax
import jax.numpy as jnp
from jax import lax
from jax.experimental import pallas as pl
from jax.experimental.pallas import tpu as pltpu

F32 = jnp.float32
BF16 = jnp.bfloat16

D_MODEL = 1024
NORM_EPS = 1e-6
CHUNK = 64
GLA_HEADS = 4
GLA_DK = 64
GLA_DV = 128
GLA_GATE_RANK = 16
GLA_GATE_NORM = 16.0
RET_HEADS = 4
RET_DK = 64
RET_DV = 128
ROPE_BASE = 10000.0
POOL_WINDOWS = (2, 4, 8, 16)
POOL_WIDTH = 512
POOL_GROUP_WIDTH = 128
S5_H = 16
S5_P = 64
S5_GROUPS = 32
S5_WIDTH = 512
S5_STATE = S5_GROUPS * S5_P
PEER_HEADS = 8
PEER_NKEYS = 128
PEER_TOPK = 16
PEER_HK = PEER_HEADS * PEER_TOPK
PEER_EXPERTS = PEER_NKEYS * PEER_NKEYS
EVEN_COLS = 3200

VMEM_LIMIT_BIG = 56 * 1024 * 1024
VMEM_LIMIT = 48 * 1024 * 1024
SUBLANES = 8

NT_DIMS = (((1,), (1,)), ((), ()))


def _split_bf16(x):
    hi = x.astype(BF16)
    lo = (x - hi.astype(F32)).astype(BF16)
    return hi, lo


def _rms(x, w):
    return x * lax.rsqrt(jnp.mean(x * x, axis=-1, keepdims=True) + NORM_EPS) * w


def _sigmoid(x):
    return 1.0 / (1.0 + jnp.exp(-x))


def _gelu(x):
    return 0.5 * x * (1.0 + lax.erf(x * (2.0 ** -0.5)))


def _log_sigmoid(z):
    return jnp.minimum(z, 0.0) - jnp.log1p(jnp.exp(-jnp.abs(z)))


def _norm_mm_kernel(h_ref, nw_ref, w_ref, o_ref):
    xn = _rms(h_ref[...], nw_ref[...])
    o_ref[...] = jnp.dot(xn.astype(BF16), w_ref[...], preferred_element_type=F32)


def norm_matmul(h, nw, w, tm):
    t, d = h.shape
    n = w.shape[1]
    return pl.pallas_call(
        _norm_mm_kernel,
        grid=(t // tm,),
        in_specs=[
            pl.BlockSpec((tm, d), lambda i: (i, 0)),
            pl.BlockSpec((1, d), lambda i: (0, 0)),
            pl.BlockSpec((d, n), lambda i: (0, 0)),
        ],
        out_specs=pl.BlockSpec((tm, n), lambda i: (i, 0)),
        out_shape=jax.ShapeDtypeStruct((t, n), F32),
        compiler_params=pltpu.CompilerParams(dimension_semantics=("parallel",), vmem_limit_bytes=VMEM_LIMIT),
        name="norm_matmul",
    )(h, nw, w)


def _rotary_table_kernel(pos_ref, freq_ref, cos_ref, sin_ref):
    ang = pos_ref[...] * freq_ref[...]
    cos_ref[...] = jnp.cos(ang)
    sin_ref[...] = jnp.sin(ang)


def rotary_table(pos, freqs, ts):
    seq = pos.shape[0]
    n = freqs.shape[1]
    return pl.pallas_call(
        _rotary_table_kernel,
        grid=(seq // ts,),
        in_specs=[pl.BlockSpec((ts, 1), lambda i: (i, 0)), pl.BlockSpec((1, n), lambda i: (0, 0))],
        out_specs=[pl.BlockSpec((ts, n), lambda i: (i, 0)), pl.BlockSpec((ts, n), lambda i: (i, 0))],
        out_shape=[jax.ShapeDtypeStruct((seq, n), F32), jax.ShapeDtypeStruct((seq, n), F32)],
        compiler_params=pltpu.CompilerParams(dimension_semantics=("parallel",)),
        name="rotary_table",
    )(pos, freqs)


def _even_mixer_kernel(h_ref, qk_ref, va_ref, ga_ref, vb_ref, gb_ref, ra_ref, cos_ref, sin_ref,
                       wup_ref, bup_ref, gnw_ref, rnw_ref, wout_ref, tril_ref,
                       dmask_ref, kdec_ref, qdec_ref, cdec_ref, hmaskg_ref, hmaskr_ref,
                       o_ref, gstate, rstate, y_scr, qin_scr, dec_scr, kvg_scr, kvr_scr):
    tm = h_ref.shape[0]

    @pl.when(pl.program_id(1) == 0)
    def _():
        gstate[...] = jnp.zeros_like(gstate)
        rstate[...] = jnp.zeros_like(rstate)

    lane = lax.broadcasted_iota(jnp.int32, (1, 256), 1)
    gla_masks = [((lane >= GLA_DK * h) & (lane < GLA_DK * (h + 1))).astype(F32) for h in range(GLA_HEADS)]
    half = RET_DK // 2
    ret_masks = [(((lane >= half * h) & (lane < half * (h + 1)))
                  | ((lane >= 128 + half * h) & (lane < 128 + half * (h + 1)))).astype(F32)
                 for h in range(RET_HEADS)]
    ri = lax.broadcasted_iota(jnp.int32, (CHUNK, CHUNK), 0)
    ci = lax.broadcasted_iota(jnp.int32, (CHUNK, CHUNK), 1)
    causal = ri >= ci
    tril = tril_ref[...]
    wup = wup_ref[...]
    bup = bup_ref[...]

    n_chunks = tm // CHUNK
    causal4 = jnp.concatenate([causal] * GLA_HEADS, axis=0)
    dmask4 = jnp.concatenate([dmask_ref[h] for h in range(RET_HEADS)], axis=0)
    hmask_g = hmaskg_ref[...]
    hmask_r = hmaskr_ref[...]

    for c in range(n_chunks):
        rows = slice(c * CHUNK, (c + 1) * CHUNK)
        qk = qk_ref[rows, :]
        qa, ka, qb, kb = qk[:, 0:256], qk[:, 256:512], qk[:, 512:768], qk[:, 768:1024]

        z = jnp.dot(ra_ref[rows, :].astype(BF16), wup, preferred_element_type=F32) + bup
        la = _log_sigmoid(z) * (1.0 / GLA_GATE_NORM)
        lah, lal = _split_bf16(la)
        g = (jnp.dot(tril, lah, preferred_element_type=F32)
             + jnp.dot(tril, lal, preferred_element_type=F32))
        g_last = g[CHUNK - 1:CHUNK, :]
        ref = 0.5 * g_last
        qs = qa * (GLA_DK ** -0.5)
        qt = qs * jnp.exp(g - ref)
        kt = (ka * jnp.exp(ref - g)).astype(BF16)
        kd = ka * jnp.exp(g_last - g)
        qin_scr[rows, 0:256] = qs * jnp.exp(g)
        dec_scr[c] = jnp.exp(g_last)
        q4 = jnp.concatenate([(qt * gla_masks[h]) for h in range(GLA_HEADS)], axis=0).astype(BF16)
        s4 = jnp.where(causal4, lax.dot_general(q4, kt, NT_DIMS, preferred_element_type=F32), 0.0).astype(BF16)
        va = va_ref[rows, :]
        vab = va.astype(BF16)
        for h in range(GLA_HEADS):
            y_scr[rows, GLA_DV * h:GLA_DV * (h + 1)] = jnp.dot(
                s4[CHUNK * h:CHUNK * (h + 1), :], vab[:, GLA_DV * h:GLA_DV * (h + 1)], preferred_element_type=F32)
        kvg_scr[c] = jnp.dot(va.T.astype(BF16), kd.astype(BF16), preferred_element_type=F32) * hmask_g

        cs, sn = cos_ref[rows, :], sin_ref[rows, :]
        q1, q2 = qb[:, 0:128], qb[:, 128:256]
        k1, k2 = kb[:, 0:128], kb[:, 128:256]
        qr = jnp.concatenate([q1 * cs - q2 * sn, q2 * cs + q1 * sn], axis=1)
        kr = jnp.concatenate([k1 * cs - k2 * sn, k2 * cs + k1 * sn], axis=1) * (RET_DK ** -0.5)
        krb = kr.astype(BF16)
        qin_scr[rows, 256:512] = qr
        q4 = jnp.concatenate([(qr * ret_masks[h]) for h in range(RET_HEADS)], axis=0).astype(BF16)
        s4 = (lax.dot_general(q4, krb, NT_DIMS, preferred_element_type=F32) * dmask4).astype(BF16)
        vb = vb_ref[rows, :]
        vbb = vb.astype(BF16)
        for h in range(RET_HEADS):
            y_scr[rows, 512 + RET_DV * h:512 + RET_DV * (h + 1)] = jnp.dot(
                s4[CHUNK * h:CHUNK * (h + 1), :], vbb[:, RET_DV * h:RET_DV * (h + 1)], preferred_element_type=F32)
        kvr_scr[c] = jnp.dot((vb * kdec_ref[...]).T.astype(BF16), krb, preferred_element_type=F32) * hmask_r

    for c in range(n_chunks):
        rows = slice(c * CHUNK, (c + 1) * CHUNK)
        sg = gstate[...]
        y_scr[rows, 0:512] += lax.dot_general(qin_scr[rows, 0:256].astype(BF16), sg.astype(BF16), NT_DIMS,
                                              preferred_element_type=F32)
        gstate[...] = sg * dec_scr[c] + kvg_scr[c]
        sr = rstate[...]
        y_scr[rows, 512:1024] += lax.dot_general(qin_scr[rows, 256:512].astype(BF16), sr.astype(BF16), NT_DIMS,
                                                 preferred_element_type=F32) * qdec_ref[...]
        rstate[...] = sr * cdec_ref[...] + kvr_scr[c]

    pieces = []
    gnw = gnw_ref[...]
    for h in range(GLA_HEADS):
        o = y_scr[:, GLA_DV * h:GLA_DV * (h + 1)]
        gt = ga_ref[:, GLA_DV * h:GLA_DV * (h + 1)]
        pieces.append(_rms(o, gnw) * (gt * _sigmoid(gt)))
    for h in range(RET_HEADS):
        o = y_scr[:, 512 + RET_DV * h:512 + RET_DV * (h + 1)]
        gt = gb_ref[:, RET_DV * h:RET_DV * (h + 1)]
        mu = jnp.mean(o, axis=-1, keepdims=True)
        oc = o - mu
        var = jnp.mean(oc * oc, axis=-1, keepdims=True)
        nrm = oc * lax.rsqrt(var + NORM_EPS) * rnw_ref[:, RET_DV * h:RET_DV * (h + 1)]
        pieces.append(nrm * (gt * _sigmoid(gt)))
    y = jnp.concatenate(pieces, axis=1).astype(BF16)
    o_ref[...] = h_ref[...] + jnp.dot(y, wout_ref[...], preferred_element_type=F32)


def even_mixer(h, proj, pos, consts, bsz, seq, tm):
    nj = seq // tm
    tok = lambda b, j: (b * nj + j, 0)
    col = lambda cb: (lambda b, j: (b * nj + j, cb))
    full2 = lambda b, j: (0, 0)
    full3 = lambda b, j: (0, 0, 0)
    (freqs, wup, bup, gnw, rnw, wout, tril, dmask, kdec, qdec, cdec, hmask_g, hmask_r) = consts
    cos_t, sin_t = rotary_table(pos, freqs, min(512, seq))
    n_state = GLA_HEADS * GLA_DV
    return pl.pallas_call(
        _even_mixer_kernel,
        grid=(bsz, nj),
        in_specs=[
            pl.BlockSpec((tm, D_MODEL), tok),
            pl.BlockSpec((tm, 1024), col(0)),
            pl.BlockSpec((tm, 512), col(2)),
            pl.BlockSpec((tm, 512), col(3)),
            pl.BlockSpec((tm, 512), col(4)),
            pl.BlockSpec((tm, 512), col(5)),
            pl.BlockSpec((tm, 128), col(24)),
            pl.BlockSpec((tm, 128), lambda b, j: (j, 0)),
            pl.BlockSpec((tm, 128), lambda b, j: (j, 0)),
            pl.BlockSpec((128, 256), full2),
            pl.BlockSpec((1, 256), full2),
            pl.BlockSpec((1, 128), full2),
            pl.BlockSpec((1, 512), full2),
            pl.BlockSpec((1024, D_MODEL), full2),
            pl.BlockSpec((CHUNK, CHUNK), full2),
            pl.BlockSpec((RET_HEADS, CHUNK, CHUNK), full3),
            pl.BlockSpec((CHUNK, n_state), full2),
            pl.BlockSpec((CHUNK, n_state), full2),
            pl.BlockSpec((n_state, 256), full2),
            pl.BlockSpec((n_state, 256), full2),
            pl.BlockSpec((n_state, 256), full2),
        ],
        out_specs=pl.BlockSpec((tm, D_MODEL), tok),
        out_shape=jax.ShapeDtypeStruct(h.shape, F32),
        scratch_shapes=[
            pltpu.VMEM((n_state, 256), F32),
            pltpu.VMEM((n_state, 256), F32),
            pltpu.VMEM((tm, 1024), F32),
            pltpu.VMEM((tm, 512), F32),
            pltpu.VMEM((tm // CHUNK, 1, 256), F32),
            pltpu.VMEM((tm // CHUNK, n_state, 256), F32),
            pltpu.VMEM((tm // CHUNK, n_state, 256), F32),
        ],
        compiler_params=pltpu.CompilerParams(dimension_semantics=("arbitrary", "arbitrary"),
                                             vmem_limit_bytes=VMEM_LIMIT),
        name="even_mixer",
    )(h, proj, proj, proj, proj, proj, proj, cos_t, sin_t, wup, bup, gnw, rnw, wout, tril, dmask, kdec, qdec, cdec,
      hmask_g, hmask_r)


def _s5_param_kernel(are_ref, aim_ref, ldt_ref, bre_ref, bim_ref,
                     pw_re_ref, pw_im_ref, bbre_ref, bbim_ref):
    a_re = are_ref[...]
    a_im = aim_ref[...]
    dt = jnp.exp(ldt_ref[...])
    for r in range(SUBLANES):
        mag = jnp.exp(a_re * dt * (r + 1.0))
        pw_re_ref[r] = mag * jnp.cos(a_im * dt * (r + 1.0))
        pw_im_ref[r] = mag * jnp.sin(a_im * dt * (r + 1.0))
    mag = jnp.exp(a_re * dt)
    abar_re, abar_im = mag * jnp.cos(a_im * dt), mag * jnp.sin(a_im * dt)
    den = a_re * a_re + a_im * a_im
    nr, ni = abar_re - 1.0, abar_im
    coef_re = (nr * a_re + ni * a_im) / den
    coef_im = (ni * a_re - nr * a_im) / den
    b_re = bre_ref[...]
    b_im = bim_ref[...]
    c_re = jnp.concatenate([coef_re] * S5_H, axis=1)
    c_im = jnp.concatenate([coef_im] * S5_H, axis=1)
    bbre_ref[...] = c_re * b_re - c_im * b_im
    bbim_ref[...] = c_re * b_im + c_im * b_re


def s5_params(a_re, a_im, log_dt, b_re_t, b_im_t):
    g, p = a_re.shape
    hh = b_re_t.shape[1]
    pw_re, pw_im, bb_re, bb_im = pl.pallas_call(
        _s5_param_kernel,
        out_shape=[jax.ShapeDtypeStruct((SUBLANES, g, p), F32), jax.ShapeDtypeStruct((SUBLANES, g, p), F32),
                   jax.ShapeDtypeStruct((g, hh * p), F32), jax.ShapeDtypeStruct((g, hh * p), F32)],
        name="s5_params",
    )(a_re, a_im, log_dt, b_re_t.reshape(g, hh * p), b_im_t.reshape(g, hh * p))
    return pw_re, pw_im, bb_re.reshape(g, hh, p), bb_im.reshape(g, hh, p)


def _odd_mixer_kernel(h_ref, u_ref, poolw_ref, pscale_ref, wbre_ref, wbim_ref, wcre_ref, wcim_ref,
                      pwre_ref, pwim_ref, dskip_ref, wglu_ref, bglu_ref, wout_ref,
                      o_ref, tail_scr, car_re, car_im, xre_scr, xim_scr):
    tm = h_ref.shape[0]
    j = pl.program_id(1)
    halo = POOL_WINDOWS[-1]

    @pl.when(j == 0)
    def _():
        tail_scr[...] = jnp.zeros_like(tail_scr)
        car_re[...] = jnp.zeros_like(car_re)
        car_im[...] = jnp.zeros_like(car_im)

    uc = u_ref[:, 0:POOL_WIDTH]
    ud = u_ref[:, POOL_WIDTH:POOL_WIDTH + S5_WIDTH]

    ext = jnp.concatenate([tail_scr[...], uc], axis=0)
    tail_scr[...] = uc[tm - halo:tm, :]
    pos = (j * tm + lax.broadcasted_iota(jnp.int32, (tm, 1), 0)).astype(F32)
    mixed = []
    for gi, win in enumerate(POOL_WINDOWS):
        a = ext[:, POOL_GROUP_WIDTH * gi:POOL_GROUP_WIDTH * (gi + 1)]
        n = tm + halo
        step = 1
        end = 0
        while step < win:
            a = a[step:n, :] + a[0:n - step, :]
            n -= step
            end += step
            step *= 2
        wsum = a[halo - end:halo - end + tm, :]
        cnt = jnp.minimum(pos + 1.0, float(win))
        pooled = wsum / cnt - uc[:, POOL_GROUP_WIDTH * gi:POOL_GROUP_WIDTH * (gi + 1)]
        mixed.append(jnp.dot(pooled.astype(BF16), poolw_ref[gi], preferred_element_type=F32))
    y_c = jnp.concatenate(mixed, axis=1) * pscale_ref[...]

    udb = ud.astype(BF16)
    hw, hs = S5_WIDTH // 2, S5_STATE // 2
    for hf in range(2):
        cols = slice(hs * hf, hs * (hf + 1))
        uh = udb[:, hw * hf:hw * (hf + 1)]
        xre_scr[:, cols] = jnp.dot(uh, wbre_ref[hw * hf:hw * (hf + 1), cols], preferred_element_type=F32)
        xim_scr[:, cols] = jnp.dot(uh, wbim_ref[hw * hf:hw * (hf + 1), cols], preferred_element_type=F32)
    rowi = lax.broadcasted_iota(jnp.int32, (SUBLANES, S5_STATE), 0)
    pw_re = pwre_ref[...]
    pw_im = pwim_ref[...]
    step_pw = [(jnp.where(rowi >= d, pw_re[d - 1:d, :], 0.0), jnp.where(rowi >= d, pw_im[d - 1:d, :], 0.0))
               for d in (1, 2, 4)]

    def slab(s, carry):
        cr, ci = carry
        rows = pl.ds(pl.multiple_of(s * SUBLANES, SUBLANES), SUBLANES)
        xr = xre_scr[rows, :]
        xi = xim_scr[rows, :]
        for dsh, (pr, pi) in zip((1, 2, 4), step_pw):
            sr = pltpu.roll(xr, dsh, axis=0)
            si = pltpu.roll(xi, dsh, axis=0)
            xr, xi = xr + (pr * sr - pi * si), xi + (pr * si + pi * sr)
        xr, xi = xr + (pw_re * cr - pw_im * ci), xi + (pw_re * ci + pw_im * cr)
        xre_scr[rows, :] = xr
        xim_scr[rows, :] = xi
        return xr[SUBLANES - 1:SUBLANES, :], xi[SUBLANES - 1:SUBLANES, :]

    cr, ci = lax.fori_loop(0, tm // SUBLANES, slab, (car_re[...], car_im[...]))
    car_re[...] = cr
    car_im[...] = ci

    yh = []
    for hf in range(2):
        rws = slice(hs * hf, hs * (hf + 1))
        cls = slice(hw * hf, hw * (hf + 1))
        yh.append(jnp.dot(xre_scr[:, rws].astype(BF16), wcre_ref[rws, cls], preferred_element_type=F32)
                  - jnp.dot(xim_scr[:, rws].astype(BF16), wcim_ref[rws, cls], preferred_element_type=F32))
    y = jnp.concatenate(yh, axis=1) + dskip_ref[...] * ud
    z = _gelu(y)
    y_d = z * _sigmoid(jnp.dot(z.astype(BF16), wglu_ref[...], preferred_element_type=F32) + bglu_ref[...])

    ycat = jnp.concatenate([y_c, y_d], axis=1).astype(BF16)
    o_ref[...] = h_ref[...] + jnp.dot(ycat, wout_ref[...], preferred_element_type=F32)


def odd_mixer(h, u, consts, bsz, seq, tm):
    nj = seq // tm
    tok = lambda b, j: (b * nj + j, 0)
    full2 = lambda b, j: (0, 0)
    full3 = lambda b, j: (0, 0, 0)
    (poolw, pscale, wbre, wbim, wcre, wcim, pwre, pwim, dskip, wglu, bglu, wout) = consts
    return pl.pallas_call(
        _odd_mixer_kernel,
        grid=(bsz, nj),
        in_specs=[
            pl.BlockSpec((tm, D_MODEL), tok),
            pl.BlockSpec((tm, 1024), tok),
            pl.BlockSpec((4, 128, 128), full3),
            pl.BlockSpec((1, POOL_WIDTH), full2),
            pl.BlockSpec((S5_WIDTH, S5_STATE), full2),
            pl.BlockSpec((S5_WIDTH, S5_STATE), full2),
            pl.BlockSpec((S5_STATE, S5_WIDTH), full2),
            pl.BlockSpec((S5_STATE, S5_WIDTH), full2),
            pl.BlockSpec((SUBLANES, S5_STATE), full2),
            pl.BlockSpec((SUBLANES, S5_STATE), full2),
            pl.BlockSpec((1, S5_WIDTH), full2),
            pl.BlockSpec((S5_WIDTH, S5_WIDTH), full2),
            pl.BlockSpec((1, S5_WIDTH), full2),
            pl.BlockSpec((1024, D_MODEL), full2),
        ],
        out_specs=pl.BlockSpec((tm, D_MODEL), tok),
        out_shape=jax.ShapeDtypeStruct(h.shape, F32),
        scratch_shapes=[
            pltpu.VMEM((POOL_WINDOWS[-1], POOL_WIDTH), F32),
            pltpu.VMEM((1, S5_STATE), F32),
            pltpu.VMEM((1, S5_STATE), F32),
            pltpu.VMEM((tm, S5_STATE), F32),
            pltpu.VMEM((tm, S5_STATE), F32),
        ],
        compiler_params=pltpu.CompilerParams(dimension_semantics=("arbitrary", "arbitrary"),
                                             vmem_limit_bytes=VMEM_LIMIT),
        name="odd_mixer",
    )(h, u, poolw, pscale, wbre, wbim, wcre, wcim, pwre, pwim, dskip, wglu, bglu, wout)


def _top16_rows(s, ids, id_bound):
    vals, idxs = [], []
    for _ in range(PEER_TOPK):
        m = jnp.max(s, axis=0, keepdims=True)
        am = jnp.min(jnp.where(s == m, ids, float(id_bound)), axis=0, keepdims=True)
        vals.append(m)
        idxs.append(am)
        s = jnp.where(ids == am, -jnp.inf, s)
    return jnp.concatenate(vals, axis=0), jnp.concatenate(idxs, axis=0)


def _batcher_pairs(lo, hi):
    def merge(lo, hi, r):
        step = 2 * r
        if step < hi - lo:
            yield from merge(lo, hi, step)
            yield from merge(lo + r, hi, step)
            yield from ((i, i + r) for i in range(lo + r, hi - r, step))
        else:
            yield (lo, lo + r)

    if hi > lo:
        mid = lo + (hi - lo) // 2
        yield from _batcher_pairs(lo, mid)
        yield from _batcher_pairs(mid + 1, hi)
        yield from merge(lo, hi, 1)


_SORT16 = tuple(_batcher_pairs(0, PEER_TOPK - 1))
_BITONIC16 = tuple((i, i + d) for d in (8, 4, 2, 1) for i in range(PEER_TOPK) if not i & d)


def _top16_network(s):
    n = PEER_TOPK
    v = [s[SUBLANES * i:SUBLANES * (i + 1), :] for i in range(n)]
    sub = lax.broadcasted_iota(jnp.int32, (SUBLANES, s.shape[1]), 0).astype(F32)
    k = [sub + float(SUBLANES * i) for i in range(n)]

    def exchange(i, j):
        swap = v[j] > v[i]
        v[i], v[j] = jnp.where(swap, v[j], v[i]), jnp.where(swap, v[i], v[j])
        k[i], k[j] = jnp.where(swap, k[j], k[i]), jnp.where(swap, k[i], k[j])

    for i, j in _SORT16:
        exchange(i, j)
    for shift in (4, 2, 1):
        bv = [pltpu.roll(x, shift, axis=0) for x in v]
        bk = [pltpu.roll(x, shift, axis=0) for x in k]
        for i in range(n):
            take = bv[n - 1 - i] > v[i]
            v[i] = jnp.where(take, bv[n - 1 - i], v[i])
            k[i] = jnp.where(take, bk[n - 1 - i], k[i])
        for i, j in _BITONIC16:
            exchange(i, j)
    tie = jnp.zeros_like(v[0])
    for i in range(n - 1):
        tie = jnp.where(v[i] == v[i + 1], 1.0, tie)
    cnt = jnp.zeros_like(v[0])
    for i in range(n):
        cnt = cnt + jnp.where(s[SUBLANES * i:SUBLANES * (i + 1), :] >= v[n - 1], 1.0, 0.0)
    for shift in (4, 2, 1):
        cnt = cnt + pltpu.roll(cnt, shift, axis=0)
    tie = jnp.where(cnt > float(n), 1.0, tie)
    return (jnp.concatenate([x[0:1, :] for x in v], axis=0), jnp.concatenate([x[0:1, :] for x in k], axis=0), tie)


_PAIR_BLOCKS = (("b", 0, 0), ("b", 1, 0), ("b", 2, 0), ("b", 3, 0), ("b", 4, 0),
                ("a", 0, 8), ("a", 0, 0), ("a", 1, 0), ("b", 0, 8))
_PAIR_ID_BOUND = 4 * PEER_TOPK * PEER_TOPK


def _pair_block_ids(tm):
    r = lax.broadcasted_iota(jnp.int32, (SUBLANES, tm), 0).astype(F32)
    seen = set()
    out = []
    for side, fixed, start in _PAIR_BLOCKS:
        ids = jnp.zeros((SUBLANES, tm), F32)
        for q in range(SUBLANES):
            i, j = (fixed, start + q) if side == "a" else (start + q, fixed)
            ok = (i + 1) * (j + 1) <= PEER_TOPK and (i, j) not in seen
            seen.add((i, j))
            ids = jnp.where(r == q, float(i * PEER_TOPK + j if ok else _PAIR_ID_BOUND + len(seen)), ids)
        out.append(ids)
    assert len({p for p in seen if (p[0] + 1) * (p[1] + 1) <= PEER_TOPK}) == 50
    return jnp.concatenate(out, axis=0)


def _pair_block_sums(av, bv):
    out = []
    for side, fixed, start in _PAIR_BLOCKS:
        if side == "a":
            out.append(av[fixed:fixed + 1, :] + bv[start:start + SUBLANES, :])
        else:
            out.append(av[start:start + SUBLANES, :] + bv[fixed:fixed + 1, :])
    return jnp.concatenate(out, axis=0)


def _take16(table, sel):
    out = jnp.zeros(sel.shape, table.dtype)
    for i in range(PEER_TOPK):
        out = jnp.where(sel == i, table[i:i + 1, :], out)
    return out


def _dot3(ah, al, bh, bl, dims):
    return (lax.dot_general(ah, bh, dims, preferred_element_type=F32)
            + lax.dot_general(al, bh, dims, preferred_element_type=F32)
            + lax.dot_general(ah, bl, dims, preferred_element_type=F32))


def _score_weight_kernel(wq_ref, key_ref, o_ref):
    kh, kl = _split_bf16(key_ref[0])
    wh, wl = _split_bf16(wq_ref[...])
    o_ref[...] = _dot3(kh, kl, wh, wl, NT_DIMS)


def score_weights(w_q, keys):
    d, nq = w_q.shape
    n_hp, nk, dk = keys.shape
    return pl.pallas_call(
        _score_weight_kernel,
        grid=(n_hp,),
        in_specs=[pl.BlockSpec((d, dk), lambda i: (0, i)), pl.BlockSpec((1, nk, dk), lambda i: (i, 0, 0))],
        out_specs=pl.BlockSpec((nk, d), lambda i: (i, 0)),
        out_shape=jax.ShapeDtypeStruct((n_hp * nk, d), F32),
        compiler_params=pltpu.CompilerParams(dimension_semantics=("parallel",)),
        name="score_weights",
    )(w_q, keys)


def _peer_route_kernel(h_ref, nw_ref, wsh_ref, wsl_ref, xn_ref, exp_ref, gate_ref, qt_scr, et_scr, gt_scr):
    xn = _rms(h_ref[...], nw_ref[...])
    for r in range(SUBLANES):
        xn_ref[:, r, :] = xn[:, 128 * r:128 * (r + 1)]
    xh, xl = _split_bf16(xn)
    qt_scr[...] = _dot3(wsh_ref[...], wsl_ref[...], xh, xl, NT_DIMS)
    tm = h_ref.shape[0]
    key_ids = lax.broadcasted_iota(jnp.int32, (PEER_NKEYS, tm), 0).astype(F32)
    pair_ids = _pair_block_ids(tm)
    pair_ok = pair_ids < float(_PAIR_ID_BOUND)

    def head(hd, tie, exact):
        sa = qt_scr[pl.ds(pl.multiple_of(hd * 256, 256), 128), :]
        sb = qt_scr[pl.ds(pl.multiple_of(hd * 256 + 128, 128), 128), :]
        if exact:
            av, ai = _top16_rows(sa, key_ids, PEER_NKEYS)
            bv, bi = _top16_rows(sb, key_ids, PEER_NKEYS)
        else:
            av, ai, ta = _top16_network(sa)
            bv, bi, tb_ = _top16_network(sb)
            tie = jnp.maximum(tie, jnp.maximum(ta, tb_))
        cand = jnp.where(pair_ok, _pair_block_sums(av, bv), -jnp.inf)
        cv, flat = _top16_rows(cand, pair_ids, _PAIR_ID_BOUND)
        flat = flat.astype(jnp.int32)
        e_a = _take16(ai, flat >> 4)
        e_b = _take16(bi, flat & (PEER_TOPK - 1))
        ex = jnp.exp(cv - cv[0:1, :])
        rows = pl.ds(pl.multiple_of(hd * PEER_TOPK, PEER_TOPK), PEER_TOPK)
        et_scr[rows, :] = (e_a * float(PEER_NKEYS) + e_b) * float(PEER_ROW_WORDS)
        gt_scr[rows, :] = ex / jnp.sum(ex, axis=0, keepdims=True)
        return tie

    def head_pair(i, tie):
        return head(2 * i + 1, head(2 * i, tie, exact=False), exact=False)

    tie = lax.fori_loop(0, PEER_HEADS // 2, head_pair, jnp.zeros((SUBLANES, tm), F32))

    @pl.when(jnp.max(tie) > 0.0)
    def _():
        lax.fori_loop(0, PEER_HEADS, functools.partial(head, exact=True), jnp.zeros((SUBLANES, tm), F32))

    exp_ref[...] = et_scr[...].T.astype(jnp.int32)
    gate_ref[...] = gt_scr[...].T


def peer_route(h, nw, w_q, keys, tm):
    t, d = h.shape
    wsh, wsl = _split_bf16(score_weights(w_q, keys))
    nq = wsh.shape[0]
    return pl.pallas_call(
        _peer_route_kernel,
        grid=(t // tm,),
        in_specs=[
            pl.BlockSpec((tm, d), lambda i: (i, 0)),
            pl.BlockSpec((1, d), lambda i: (0, 0)),
            pl.BlockSpec((nq, d), lambda i: (0, 0)),
            pl.BlockSpec((nq, d), lambda i: (0, 0)),
        ],
        out_specs=[
            pl.BlockSpec((tm, SUBLANES, d // SUBLANES), lambda i: (i, 0, 0)),
            pl.BlockSpec((tm, PEER_HK), lambda i: (i, 0)),
            pl.BlockSpec((tm, PEER_HK), lambda i: (i, 0)),
        ],
        out_shape=[jax.ShapeDtypeStruct((t, SUBLANES, d // SUBLANES), F32),
                   jax.ShapeDtypeStruct((t, PEER_HK), jnp.int32), jax.ShapeDtypeStruct((t, PEER_HK), F32)],
        scratch_shapes=[pltpu.VMEM((nq, tm), F32), pltpu.VMEM((PEER_HK, tm), F32), pltpu.VMEM((PEER_HK, tm), F32)],
        compiler_params=pltpu.CompilerParams(dimension_semantics=("parallel",), vmem_limit_bytes=VMEM_LIMIT),
        name="peer_route",
    )(h, nw, wsh, wsl)


PEER_GROUP = 16
PEER_ROW_WORDS = 4


def _gather_group(idx_ref, tab_ref, g, stage_ref):
    rows = [idx_ref.at[g * PEER_GROUP + j] for j in range(PEER_GROUP)]
    for k in range(PEER_HK):
        for j in range(PEER_GROUP):
            off = pl.multiple_of(rows[j][k], PEER_ROW_WORDS)
            stage_ref[j, pl.ds(PEER_ROW_WORDS * k, PEER_ROW_WORDS), :] = tab_ref[pl.ds(off, PEER_ROW_WORDS), :]


def _gather_compute_pipeline(n_groups, idx_ref, tab_ref, consume, stage_a, stage_b):
    def compute(g, stage_ref):
        for j in range(PEER_GROUP):
            consume(g * PEER_GROUP + j, stage_ref.at[j])

    _gather_group(idx_ref, tab_ref,0, stage_a)

    def body(i, carry):
        compute(2 * i, stage_a)
        _gather_group(idx_ref, tab_ref, 2 * i + 1, stage_b)
        compute(2 * i + 1, stage_b)
        _gather_group(idx_ref, tab_ref, 2 * i + 2, stage_a)
        return carry

    lax.fori_loop(0, n_groups // 2 - 1, body, 0)
    _gather_group(idx_ref, tab_ref,n_groups - 1, stage_b)
    compute(n_groups - 2, stage_a)
    compute(n_groups - 1, stage_b)


def _diag_mask():
    row = lax.broadcasted_iota(jnp.int32, (SUBLANES, SUBLANES * PEER_HK), 0)
    lane = lax.broadcasted_iota(jnp.int32, (SUBLANES, SUBLANES * PEER_HK), 1)
    return (lane & (SUBLANES - 1)) == row


def _peer_hidden_kernel(idx_ref, x_ref, g_ref, sel_ref, tab_ref, o_ref, stage_a, stage_b, part_scr):
    tb = x_ref.shape[0]
    diag = _diag_mask()

    def consume(t, rows_ref):
        u = pltpu.bitcast(rows_ref[...], BF16)
        xh, xl = _split_bf16(x_ref[t])
        x16 = jnp.concatenate([xh, xl], axis=0)
        out = lax.dot_general(x16, u, NT_DIMS, preferred_element_type=F32)
        o8 = out[0:SUBLANES] + out[SUBLANES:2 * SUBLANES]
        part_scr[pl.ds(t, 1), :] = jnp.sum(jnp.where(diag, o8, 0.0), axis=0, keepdims=True)

    _gather_compute_pipeline(tb // PEER_GROUP, idx_ref, tab_ref, consume, stage_a, stage_b)
    ph, plo = _split_bf16(part_scr[...])
    sel = sel_ref[...]
    hid = jnp.dot(ph, sel, preferred_element_type=F32) + jnp.dot(plo, sel, preferred_element_type=F32)
    o_ref[...] = g_ref[...] * _gelu(hid)


def peer_hidden(idx, x8, gates, sel, tab, tb):
    t = idx.shape[0]
    return pl.pallas_call(
        _peer_hidden_kernel,
        grid=(t // tb,),
        in_specs=[
            pl.BlockSpec((tb, PEER_HK), lambda i: (i, 0), memory_space=pltpu.SMEM),
            pl.BlockSpec((tb, SUBLANES, 128), lambda i: (i, 0, 0)),
            pl.BlockSpec((tb, PEER_HK), lambda i: (i, 0)),
            pl.BlockSpec((SUBLANES * PEER_HK, PEER_HK), lambda i: (0, 0)),
            pl.BlockSpec((PEER_ROW_WORDS * PEER_EXPERTS, 128), lambda i: (0, 0), pipeline_mode=pl.Buffered(1)),
        ],
        out_specs=pl.BlockSpec((tb, PEER_HK), lambda i: (i, 0)),
        out_shape=jax.ShapeDtypeStruct((t, PEER_HK), F32),
        scratch_shapes=[pltpu.VMEM((PEER_GROUP, PEER_ROW_WORDS * PEER_HK, 128), jnp.int32),
                        pltpu.VMEM((PEER_GROUP, PEER_ROW_WORDS * PEER_HK, 128), jnp.int32),
                        pltpu.VMEM((tb, SUBLANES * PEER_HK), F32)],
        compiler_params=pltpu.CompilerParams(dimension_semantics=("arbitrary",), vmem_limit_bytes=VMEM_LIMIT_BIG),
        name="peer_hidden",
    )(idx,x8, gates, sel, tab)


def _peer_out_kernel(idx_ref, w_ref, exp_ref, tab_ref, o_ref, stage_a, stage_b, wexp_scr):
    tb = w_ref.shape[0]
    diag = _diag_mask()
    wh, wl = _split_bf16(w_ref[...])
    ex = exp_ref[...]
    wexp_scr[...] = jnp.dot(wh, ex, preferred_element_type=F32) + jnp.dot(wl, ex, preferred_element_type=F32)

    def consume(t, rows_ref):
        v = pltpu.bitcast(rows_ref[...], BF16)
        w8 = jnp.where(diag, jnp.broadcast_to(wexp_scr[pl.ds(t, 1), :], (SUBLANES, SUBLANES * PEER_HK)), 0.0)
        w8h, w8l = _split_bf16(w8)
        w16 = jnp.concatenate([w8h, w8l], axis=0)
        out = jnp.dot(w16, v, preferred_element_type=F32)
        o_ref[t] = out[0:SUBLANES] + out[SUBLANES:2 * SUBLANES]

    _gather_compute_pipeline(tb // PEER_GROUP, idx_ref, tab_ref, consume, stage_a, stage_b)


def peer_out(idx, w, expand, tab, tb):
    t = idx.shape[0]
    return pl.pallas_call(
        _peer_out_kernel,
        grid=(t // tb,),
        in_specs=[
            pl.BlockSpec((tb, PEER_HK), lambda i: (i, 0), memory_space=pltpu.SMEM),
            pl.BlockSpec((tb, PEER_HK), lambda i: (i, 0)),
            pl.BlockSpec((PEER_HK, SUBLANES * PEER_HK), lambda i: (0, 0)),
            pl.BlockSpec((PEER_ROW_WORDS * PEER_EXPERTS, 128), lambda i: (0, 0), pipeline_mode=pl.Buffered(1)),
        ],
        out_specs=pl.BlockSpec((tb, SUBLANES, 128), lambda i: (i, 0, 0)),
        out_shape=jax.ShapeDtypeStruct((t, SUBLANES, 128), F32),
        scratch_shapes=[pltpu.VMEM((PEER_GROUP, PEER_ROW_WORDS * PEER_HK, 128), jnp.int32),
                        pltpu.VMEM((PEER_GROUP, PEER_ROW_WORDS * PEER_HK, 128), jnp.int32),
                        pltpu.VMEM((tb, SUBLANES * PEER_HK), F32)],
        compiler_params=pltpu.CompilerParams(dimension_semantics=("arbitrary",), vmem_limit_bytes=VMEM_LIMIT_BIG),
        name="peer_out",
    )(idx,w, expand, tab)


def _ple_kernel(h_ref, e8_ref, p_ref, nw_ref, wg_ref, wp_ref, fw_ref, o_ref, *, final_norm):
    h = h_ref[...] + jnp.concatenate([e8_ref[:, r, :] for r in range(SUBLANES)], axis=1)
    gate = _sigmoid(jnp.dot(_rms(h, nw_ref[...]).astype(BF16), wg_ref[...], preferred_element_type=F32))
    out = h + jnp.dot(p_ref[...].astype(BF16), wp_ref[...], preferred_element_type=F32) * gate
    if final_norm:
        out = _rms(out, fw_ref[...])
    o_ref[...] = out


def ple(h, e8, p, nw, wg, wp, fw, tm, final_norm):
    t, d = h.shape
    pd = p.shape[1]
    return pl.pallas_call(
        functools.partial(_ple_kernel, final_norm=final_norm),
        grid=(t // tm,),
        in_specs=[
            pl.BlockSpec((tm, d), lambda i: (i, 0)),
            pl.BlockSpec((tm, SUBLANES, d // SUBLANES), lambda i: (i, 0, 0)),
            pl.BlockSpec((tm, pd), lambda i: (i, 0)),
            pl.BlockSpec((1, d), lambda i: (0, 0)),
            pl.BlockSpec((d, d), lambda i: (0, 0)),
            pl.BlockSpec((pd, d), lambda i: (0, 0)),
            pl.BlockSpec((1, d), lambda i: (0, 0)),
        ],
        out_specs=pl.BlockSpec((tm, d), lambda i: (i, 0)),
        out_shape=jax.ShapeDtypeStruct((t, d), F32),
        compiler_params=pltpu.CompilerParams(dimension_semantics=("parallel",), vmem_limit_bytes=VMEM_LIMIT),
        name="ple",
    )(h, e8, p, nw, wg, wp, fw)


def _pack_table_kernel(t_ref, o_ref):
    x = t_ref[...]
    te = x.shape[0]
    for s in range(PEER_ROW_WORDS):
        lo = pltpu.bitcast(x[:, 256 * s:256 * s + 128].astype(BF16).astype(F32), jnp.int32)
        hi = pltpu.bitcast(x[:, 256 * s + 128:256 * s + 256].astype(BF16).astype(F32), jnp.int32)
        o_ref[pl.ds(s, te, stride=PEER_ROW_WORDS), :] = (hi & jnp.int32(-65536)) | lax.shift_right_logical(lo, 16)


def _pack_table(tab):
    e, d = tab.shape
    te = min(512, e)
    return pl.pallas_call(
        _pack_table_kernel,
        grid=(e // te,),
        in_specs=[pl.BlockSpec((te, d), lambda i: (i, 0))],
        out_specs=pl.BlockSpec((PEER_ROW_WORDS * te, 128), lambda i: (i, 0)),
        out_shape=jax.ShapeDtypeStruct((PEER_ROW_WORDS * e, 128), jnp.int32),
        compiler_params=pltpu.CompilerParams(dimension_semantics=("parallel",), vmem_limit_bytes=VMEM_LIMIT),
        name="pack_table",
    )(tab)


def _even_in_perm():
    off = np.cumsum([0, 256, 256, 512, 512, 16, 256, 256, 512, 512])
    qa, ka, va, ga, ra, qb, kb, vb, gb = [np.arange(off[i], off[i + 1]) for i in range(9)]
    half = RET_DK // 2
    rot = np.concatenate([np.concatenate([np.arange(h * RET_DK, h * RET_DK + half) for h in range(RET_HEADS)]),
                          np.concatenate([np.arange(h * RET_DK + half, (h + 1) * RET_DK) for h in range(RET_HEADS)])])
    return np.concatenate([qa, ka, qb[rot], kb[rot], va, ga, vb, gb, ra])


def _retention_tables():
    lg = np.log(1.0 - 2.0 ** (-5.0 - np.arange(RET_HEADS, dtype=np.float64)))
    idx = np.arange(CHUNK, dtype=np.float64)
    diff = idx[:, None] - idx[None, :]
    dmask = np.where(diff >= 0, np.exp(lg[:, None, None] * np.maximum(diff, 0.0)), 0.0)
    qdec = np.exp(lg[:, None] * (idx + 1.0))
    kdec = np.exp(lg[:, None] * (CHUNK - 1.0 - idx))
    cdec = np.exp(lg * CHUNK)
    per_col = lambda t: np.repeat(t.T, RET_DV, axis=1)
    lanes = np.arange(256)
    rows_head = np.repeat(np.arange(RET_HEADS), RET_DV)[:, None]
    half = RET_DK // 2
    hmask_g = (lanes[None, :] // GLA_DK) == rows_head
    hmask_r = ((lanes[None, :] % 128) // half == rows_head) & ((lanes[None, :] % 128) < RET_HEADS * half)
    return (jnp.asarray(dmask, F32), jnp.asarray(per_col(kdec), F32), jnp.asarray(per_col(qdec), F32),
            jnp.asarray(np.broadcast_to(np.repeat(cdec, RET_DV)[:, None], (RET_HEADS * RET_DV, 256)), F32),
            jnp.asarray(hmask_g, F32), jnp.asarray(hmask_r, F32))


def _peer_layer(h, nw, w_q, sub_keys, u_tab, v_tab, tm_route, tb):
    keys = sub_keys.reshape(2 * PEER_HEADS, PEER_NKEYS, -1)
    xn8, experts, gates = peer_route(h, nw, w_q, keys, tm_route)
    kk = np.arange(SUBLANES * PEER_HK) // SUBLANES
    sel = jnp.asarray(kk[:, None] == np.arange(PEER_HK)[None, :], BF16)
    w = peer_hidden(experts, xn8, gates, sel, _pack_table(u_tab), tb)
    return peer_out(experts, w, sel.T, _pack_table(v_tab), tb)


def _row(v):
    return v.reshape(1, -1).astype(F32)


def _layer0_mixer(h, w, bsz, seq):
    t = h.shape[0]
    w_in = w["ev_w_in"][0]
    w_in0 = jnp.pad(w_in[:, _even_in_perm()], ((0, 0), (0, EVEN_COLS - w_in.shape[1]))).astype(BF16)
    proj = norm_matmul(h, _row(w["norm_mix_w"][0]), w_in0, min(256, t))
    half = RET_DK // 2
    freqs = ROPE_BASE ** (-np.arange(half, dtype=np.float32) / half)
    dmask, kdec, qdec, cdec, hmask_g, hmask_r = _retention_tables()
    consts = (
        jnp.asarray(np.tile(freqs, RET_HEADS)[None, :], F32),
        jnp.pad(w["ev_gla_w_up"][0], ((0, 128 - GLA_GATE_RANK), (0, 0))).astype(BF16),
        _row(w["ev_gla_b_up"][0]), _row(w["ev_gla_norm_w"][0]), _row(w["ev_ret_norm_w"][0]),
        w["ev_w_out"][0].astype(BF16),
        jnp.asarray(np.tril(np.ones((CHUNK, CHUNK))), BF16),
        dmask, kdec, qdec, cdec, hmask_g, hmask_r,
    )
    pos = w["positions"].astype(F32).reshape(seq, 1)
    return even_mixer(h, proj, pos, consts, bsz, seq, min(256, seq))


def _layer1_mixer(h, w, bsz, seq):
    t = h.shape[0]
    u = norm_matmul(h, _row(w["norm_mix_w"][1]), w["od_w_in"][0].astype(BF16), min(512, t))
    pw_re, pw_im, bb_re, bb_im = s5_params(w["od_s5_a_re"][0], w["od_s5_a_im"][0], w["od_s5_log_dt"][0].reshape(-1, 1),
                                           w["od_s5_b_re"][0].transpose(0, 2, 1), w["od_s5_b_im"][0].transpose(0, 2, 1))
    eye = jnp.eye(S5_GROUPS, dtype=F32)
    blockdiag = lambda m: (m[:, :, None, :] * eye[:, None, :, None]).reshape(m.shape[0] * m.shape[1], -1)
    consts = (
        w["od_pool_w"][0].astype(BF16), _row(w["od_pool_scale"][0]),
        blockdiag(bb_re).astype(BF16), blockdiag(bb_im).astype(BF16),
        blockdiag(w["od_s5_c_re"][0]).T.astype(BF16), blockdiag(w["od_s5_c_im"][0]).T.astype(BF16),
        pw_re.reshape(SUBLANES, S5_STATE), pw_im.reshape(SUBLANES, S5_STATE),
        _row(w["od_s5_d"][0]), w["od_s5_w_glu"][0].astype(BF16), _row(w["od_s5_b_glu"][0]), w["od_w_out"][0].astype(BF16),
    )
    return odd_mixer(h, u, consts, bsz, seq, min(256, seq))


def kernel(x, p, positions, norm_mix_w, norm_ffn_w, norm_ple_w, final_norm_w, ev_w_in, ev_gla_w_up, ev_gla_b_up, ev_gla_norm_w, ev_ret_norm_w, ev_w_out, od_w_in, od_pool_w, od_pool_scale, od_s5_a_re, od_s5_a_im, od_s5_log_dt, od_s5_b_re, od_s5_b_im, od_s5_c_re, od_s5_c_im, od_s5_d, od_s5_w_glu, od_s5_b_glu, od_w_out, peer_w_q, peer_sub_keys, peer_u, peer_v, ple_w_proj, ple_w_gate):
    w = dict(locals())
    bsz, seq, d = x.shape
    t = bsz * seq
    tm_tok = min(512, t)
    tm_route = min(256, t)
    tb = min(512, t)
    h = x.reshape(t, d)
    for i, mixer in enumerate((_layer0_mixer, _layer1_mixer)):
        h = mixer(h, w, bsz, seq)
        e8 = _peer_layer(h, _row(norm_ffn_w[i]), peer_w_q[i], peer_sub_keys[i], peer_u[i], peer_v[i], tm_route, tb)
        h = ple(h, e8, p[i].reshape(t, -1), _row(norm_ple_w[i]), ple_w_gate[i].astype(BF16), ple_w_proj[i].astype(BF16),
                _row(final_norm_w), tm_tok, i == 1)
    return h.reshape(bsz, seq, d)
```

```python
import functools
import math

import numpy as np
import jax
import jax.numpy as jnp
from jax import lax
from jax.experimental import pallas as pl
from jax.experimental.pallas import tpu as pltpu

F32 = jnp.float32
BF16 = jnp.bfloat16

D_MODEL = 1024
NORM_EPS = 1e-6
CHUNK = 64
GLA_HEADS = 4
GLA_DK = 64
GLA_DV = 128
GLA_GATE_RANK = 16
GLA_GATE_NORM = 16.0
RET_HEADS = 4
RET_DK = 64
RET_DV = 128
ROPE_BASE = 10000.0
POOL_WINDOWS = (2, 4, 8, 16)
POOL_WIDTH = 512
POOL_GROUP_WIDTH = 128
S5_H = 16
S5_P = 64
S5_GROUPS = 32
S5_WIDTH = 512
S5_STATE = S5_GROUPS * S5_P
PEER_HEADS = 8
PEER_NKEYS = 128
PEER_TOPK = 16
PEER_HK = PEER_HEADS * PEER_TOPK
PEER_EXPERTS = PEER_NKEYS * PEER_NKEYS
EVEN_COLS = 3200

VMEM_LIMIT_BIG = 56 * 1024 * 1024
VMEM_LIMIT = 48 * 1024 * 1024
SUBLANES = 8

NT_DIMS = (((1,), (1,)), ((), ()))


def _split_bf16(x):
    hi = x.astype(BF16)
    lo = (x - hi.astype(F32)).astype(BF16)
    return hi, lo


def _rms(x, w):
    return x * lax.rsqrt(jnp.mean(x * x, axis=-1, keepdims=True) + NORM_EPS) * w


def _sigmoid(x):
    return 1.0 / (1.0 + jnp.exp(-x))


def _gelu(x):
    return 0.5 * x * (1.0 + lax.erf(x * (2.0 ** -0.5)))


def _log_sigmoid(z):
    return jnp.minimum(z, 0.0) - jnp.log1p(jnp.exp(-jnp.abs(z)))


def _norm_mm_kernel(h_ref, nw_ref, w_ref, o_ref):
    xn = _rms(h_ref[...], nw_ref[...])
    o_ref[...] = jnp.dot(xn.astype(BF16), w_ref[...], preferred_element_type=F32)


def norm_matmul(h, nw, w, tm):
    t, d = h.shape
    n = w.shape[1]
    return pl.pallas_call(
        _norm_mm_kernel,
        grid=(t // tm,),
        in_specs=[
            pl.BlockSpec((tm, d), lambda i: (i, 0)),
            pl.BlockSpec((1, d), lambda i: (0, 0)),
            pl.BlockSpec((d, n), lambda i: (0, 0)),
        ],
        out_specs=pl.BlockSpec((tm, n), lambda i: (i, 0)),
        out_shape=jax.ShapeDtypeStruct((t, n), F32),
        compiler_params=pltpu.CompilerParams(dimension_semantics=("parallel",), vmem_limit_bytes=VMEM_LIMIT),
        name="norm_matmul",
    )(h, nw, w)


def _rotary_table_kernel(pos_ref, freq_ref, cos_ref, sin_ref):
    ang = pos_ref[...] * freq_ref[...]
    cos_ref[...] = jnp.cos(ang)
    sin_ref[...] = jnp.sin(ang)


def rotary_table(pos, freqs, ts):
    seq = pos.shape[0]
    n = freqs.shape[1]
    return pl.pallas_call(
        _rotary_table_kernel,
        grid=(seq // ts,),
        in_specs=[pl.BlockSpec((ts, 1), lambda i: (i, 0)), pl.BlockSpec((1, n), lambda i: (0, 0))],
        out_specs=[pl.BlockSpec((ts, n), lambda i: (i, 0)), pl.BlockSpec((ts, n), lambda i: (i, 0))],
        out_shape=[jax.ShapeDtypeStruct((seq, n), F32), jax.ShapeDtypeStruct((seq, n), F32)],
        compiler_params=pltpu.CompilerParams(dimension_semantics=("parallel",)),
        name="rotary_table",
    )(pos, freqs)


def _even_mixer_kernel(h_ref, qk_ref, va_ref, ga_ref, vb_ref, gb_ref, ra_ref, cos_ref, sin_ref,
                       wup_ref, bup_ref, gnw_ref, rnw_ref, wout_ref, tril_ref,
                       dmask_ref, kdec_ref, qdec_ref, cdec_ref, hmaskg_ref, hmaskr_ref,
                       o_ref, gstate, rstate, y_scr, qin_scr, dec_scr, kvg_scr, kvr_scr):
    tm = h_ref.shape[0]

    @pl.when(pl.program_id(1) == 0)
    def _():
        gstate[...] = jnp.zeros_like(gstate)
        rstate[...] = jnp.zeros_like(rstate)

    lane = lax.broadcasted_iota(jnp.int32, (1, 256), 1)
    gla_masks = [((lane >= GLA_DK * h) & (lane < GLA_DK * (h + 1))).astype(F32) for h in range(GLA_HEADS)]
    half = RET_DK // 2
    ret_masks = [(((lane >= half * h) & (lane < half * (h + 1)))
                  | ((lane >= 128 + half * h) & (lane < 128 + half * (h + 1)))).astype(F32)
                 for h in range(RET_HEADS)]
    ri = lax.broadcasted_iota(jnp.int32, (CHUNK, CHUNK), 0)
    ci = lax.broadcasted_iota(jnp.int32, (CHUNK, CHUNK), 1)
    causal = ri >= ci
    tril = tril_ref[...]
    wup = wup_ref[...]
    bup = bup_ref[...]

    n_chunks = tm // CHUNK
    causal4 = jnp.concatenate([causal] * GLA_HEADS, axis=0)
    dmask4 = jnp.concatenate([dmask_ref[h] for h in range(RET_HEADS)], axis=0)
    hmask_g = hmaskg_ref[...]
    hmask_r = hmaskr_ref[...]

    for c in range(n_chunks):
        rows = slice(c * CHUNK, (c + 1) * CHUNK)
        qk = qk_ref[rows, :]
        qa, ka, qb, kb = qk[:, 0:256], qk[:, 256:512], qk[:, 512:768], qk[:, 768:1024]

        z = jnp.dot(ra_ref[rows, :].astype(BF16), wup, preferred_element_type=F32) + bup
        la = _log_sigmoid(z) * (1.0 / GLA_GATE_NORM)
        lah, lal = _split_bf16(la)
        g = (jnp.dot(tril, lah, preferred_element_type=F32)
             + jnp.dot(tril, lal, preferred_element_type=F32))
        g_last = g[CHUNK - 1:CHUNK, :]
        ref = 0.5 * g_last
        qs = qa * (GLA_DK ** -0.5)
        qt = qs * jnp.exp(g - ref)
        kt = (ka * jnp.exp(ref - g)).astype(BF16)
        kd = ka * jnp.exp(g_last - g)
        qin_scr[rows, 0:256] = qs * jnp.exp(g)
        dec_scr[c] = jnp.exp(g_last)
        q4 = jnp.concatenate([(qt * gla_masks[h]) for h in range(GLA_HEADS)], axis=0).astype(BF16)
        s4 = jnp.where(causal4, lax.dot_general(q4, kt, NT_DIMS, preferred_element_type=F32), 0.0).astype(BF16)
        va = va_ref[rows, :]
        vab = va.astype(BF16)
        for h in range(GLA_HEADS):
            y_scr[rows, GLA_DV * h:GLA_DV * (h + 1)] = jnp.dot(
                s4[CHUNK * h:CHUNK * (h + 1), :], vab[:, GLA_DV * h:GLA_DV * (h + 1)], preferred_element_type=F32)
        kvg_scr[c] = jnp.dot(va.T.astype(BF16), kd.astype(BF16), preferred_element_type=F32) * hmask_g

        cs, sn = cos_ref[rows, :], sin_ref[rows, :]
        q1, q2 = qb[:, 0:128], qb[:, 128:256]
        k1, k2 = kb[:, 0:128], kb[:, 128:256]
        qr = jnp.concatenate([q1 * cs - q2 * sn, q2 * cs + q1 * sn], axis=1)
        kr = jnp.concatenate([k1 * cs - k2 * sn, k2 * cs + k1 * sn], axis=1) * (RET_DK ** -0.5)
        krb = kr.astype(BF16)
        qin_scr[rows, 256:512] = qr
        q4 = jnp.concatenate([(qr * ret_masks[h]) for h in range(RET_HEADS)], axis=0).astype(BF16)
        s4 = (lax.dot_general(q4, krb, NT_DIMS, preferred_element_type=F32) * dmask4).astype(BF16)
        vb = vb_ref[rows, :]
        vbb = vb.astype(BF16)
        for h in range(RET_HEADS):
            y_scr[rows, 512 + RET_DV * h:512 + RET_DV * (h + 1)] = jnp.dot(
                s4[CHUNK * h:CHUNK * (h + 1), :], vbb[:, RET_DV * h:RET_DV * (h + 1)], preferred_element_type=F32)
        kvr_scr[c] = jnp.dot((vb * kdec_ref[...]).T.astype(BF16), krb, preferred_element_type=F32) * hmask_r

    for c in range(n_chunks):
        rows = slice(c * CHUNK, (c + 1) * CHUNK)
        sg = gstate[...]
        y_scr[rows, 0:512] += lax.dot_general(qin_scr[rows, 0:256].astype(BF16), sg.astype(BF16), NT_DIMS,
                                              preferred_element_type=F32)
        gstate[...] = sg * dec_scr[c] + kvg_scr[c]
        sr = rstate[...]
        y_scr[rows, 512:1024] += lax.dot_general(qin_scr[rows, 256:512].astype(BF16), sr.astype(BF16), NT_DIMS,
                                                 preferred_element_type=F32) * qdec_ref[...]
        rstate[...] = sr * cdec_ref[...] + kvr_scr[c]

    pieces = []
    gnw = gnw_ref[...]
    for h in range(GLA_HEADS):
        o = y_scr[:, GLA_DV * h:GLA_DV * (h + 1)]
        gt = ga_ref[:, GLA_DV * h:GLA_DV * (h + 1)]
        pieces.append(_rms(o, gnw) * (gt * _sigmoid(gt)))
    for h in range(RET_HEADS):
        o = y_scr[:, 512 + RET_DV * h:512 + RET_DV * (h + 1)]
        gt = gb_ref[:, RET_DV * h:RET_DV * (h + 1)]
        mu = jnp.mean(o, axis=-1, keepdims=True)
        oc = o - mu
        var = jnp.mean(oc * oc, axis=-1, keepdims=True)
        nrm = oc * lax.rsqrt(var + NORM_EPS) * rnw_ref[:, RET_DV * h:RET_DV * (h + 1)]
        pieces.append(nrm * (gt * _sigmoid(gt)))
    y = jnp.concatenate(pieces, axis=1).astype(BF16)
    o_ref[...] = h_ref[...] + jnp.dot(y, wout_ref[...], preferred_element_type=F32)


def even_mixer(h, proj, pos, consts, bsz, seq, tm):
    nj = seq // tm
    tok = lambda b, j: (b * nj + j, 0)
    col = lambda cb: (lambda b, j: (b * nj + j, cb))
    full2 = lambda b, j: (0, 0)
    full3 = lambda b, j: (0, 0, 0)
    (freqs, wup, bup, gnw, rnw, wout, tril, dmask, kdec, qdec, cdec, hmask_g, hmask_r) = consts
    cos_t, sin_t = rotary_table(pos, freqs, min(512, seq))
    n_state = GLA_HEADS * GLA_DV
    return pl.pallas_call(
        _even_mixer_kernel,
        grid=(bsz, nj),
        in_specs=[
            pl.BlockSpec((tm, D_MODEL), tok),
            pl.BlockSpec((tm, 1024), col(0)),
            pl.BlockSpec((tm, 512), col(2)),
            pl.BlockSpec((tm, 512), col(3)),
            pl.BlockSpec((tm, 512), col(4)),
            pl.BlockSpec((tm, 512), col(5)),
            pl.BlockSpec((tm, 128), col(24)),
            pl.BlockSpec((tm, 128), lambda b, j: (j, 0)),
            pl.BlockSpec((tm, 128), lambda b, j: (j, 0)),
            pl.BlockSpec((128, 256), full2),
            pl.BlockSpec((1, 256), full2),
            pl.BlockSpec((1, 128), full2),
            pl.BlockSpec((1, 512), full2),
            pl.BlockSpec((1024, D_MODEL), full2),
            pl.BlockSpec((CHUNK, CHUNK), full2),
            pl.BlockSpec((RET_HEADS, CHUNK, CHUNK), full3),
            pl.BlockSpec((CHUNK, n_state), full2),
            pl.BlockSpec((CHUNK, n_state), full2),
            pl.BlockSpec((n_state, 256), full2),
            pl.BlockSpec((n_state, 256), full2),
            pl.BlockSpec((n_state, 256), full2),
        ],
        out_specs=pl.BlockSpec((tm, D_MODEL), tok),
        out_shape=jax.ShapeDtypeStruct(h.shape, F32),
        scratch_shapes=[
            pltpu.VMEM((n_state, 256), F32),
            pltpu.VMEM((n_state, 256), F32),
            pltpu.VMEM((tm, 1024), F32),
            pltpu.VMEM((tm, 512), F32),
            pltpu.VMEM((tm // CHUNK, 1, 256), F32),
            pltpu.VMEM((tm // CHUNK, n_state, 256), F32),
            pltpu.VMEM((tm // CHUNK, n_state, 256), F32),
        ],
        compiler_params=pltpu.CompilerParams(dimension_semantics=("arbitrary", "arbitrary"),
                                             vmem_limit_bytes=VMEM_LIMIT),
        name="even_mixer",
    )(h, proj, proj, proj, proj, proj, proj, cos_t, sin_t, wup, bup, gnw, rnw, wout, tril, dmask, kdec, qdec, cdec,
      hmask_g, hmask_r)


def _s5_param_kernel(are_ref, aim_ref, ldt_ref, bre_ref, bim_ref,
                     pw_re_ref, pw_im_ref, bbre_ref, bbim_ref):
    a_re = are_ref[...]
    a_im = aim_ref[...]
    dt = jnp.exp(ldt_ref[...])
    for r in range(SUBLANES):
        mag = jnp.exp(a_re * dt * (r + 1.0))
        pw_re_ref[r] = mag * jnp.cos(a_im * dt * (r + 1.0))
        pw_im_ref[r] = mag * jnp.sin(a_im * dt * (r + 1.0))
    mag = jnp.exp(a_re * dt)
    abar_re, abar_im = mag * jnp.cos(a_im * dt), mag * jnp.sin(a_im * dt)
    den = a_re * a_re + a_im * a_im
    nr, ni = abar_re - 1.0, abar_im
    coef_re = (nr * a_re + ni * a_im) / den
    coef_im = (ni * a_re - nr * a_im) / den
    b_re = bre_ref[...]
    b_im = bim_ref[...]
    c_re = jnp.concatenate([coef_re] * S5_H, axis=1)
    c_im = jnp.concatenate([coef_im] * S5_H, axis=1)
    bbre_ref[...] = c_re * b_re - c_im * b_im
    bbim_ref[...] = c_re * b_im + c_im * b_re


def s5_params(a_re, a_im, log_dt, b_re_t, b_im_t):
    g, p = a_re.shape
    hh = b_re_t.shape[1]
    pw_re, pw_im, bb_re, bb_im = pl.pallas_call(
        _s5_param_kernel,
        out_shape=[jax.ShapeDtypeStruct((SUBLANES, g, p), F32), jax.ShapeDtypeStruct((SUBLANES, g, p), F32),
                   jax.ShapeDtypeStruct((g, hh * p), F32), jax.ShapeDtypeStruct((g, hh * p), F32)],
        name="s5_params",
    )(a_re, a_im, log_dt, b_re_t.reshape(g, hh * p), b_im_t.reshape(g, hh * p))
    return pw_re, pw_im, bb_re.reshape(g, hh, p), bb_im.reshape(g, hh, p)


def _odd_mixer_kernel(h_ref, u_ref, poolw_ref, pscale_ref, wbre_ref, wbim_ref, wcre_ref, wcim_ref,
                      pwre_ref, pwim_ref, dskip_ref, wglu_ref, bglu_ref, wout_ref,
                      o_ref, tail_scr, car_re, car_im, xre_scr, xim_scr):
    tm = h_ref.shape[0]
    j = pl.program_id(1)
    halo = POOL_WINDOWS[-1]

    @pl.when(j == 0)
    def _():
        tail_scr[...] = jnp.zeros_like(tail_scr)
        car_re[...] = jnp.zeros_like(car_re)
        car_im[...] = jnp.zeros_like(car_im)

    uc = u_ref[:, 0:POOL_WIDTH]
    ud = u_ref[:, POOL_WIDTH:POOL_WIDTH + S5_WIDTH]

    ext = jnp.concatenate([tail_scr[...], uc], axis=0)
    tail_scr[...] = uc[tm - halo:tm, :]
    pos = (j * tm + lax.broadcasted_iota(jnp.int32, (tm, 1), 0)).astype(F32)
    mixed = []
    for gi, win in enumerate(POOL_WINDOWS):
        a = ext[:, POOL_GROUP_WIDTH * gi:POOL_GROUP_WIDTH * (gi + 1)]
        n = tm + halo
        step = 1
        end = 0
        while step < win:
            a = a[step:n, :] + a[0:n - step, :]
            n -= step
            end += step
            step *= 2
        wsum = a[halo - end:halo - end + tm, :]
        cnt = jnp.minimum(pos + 1.0, float(win))
        pooled = wsum / cnt - uc[:, POOL_GROUP_WIDTH * gi:POOL_GROUP_WIDTH * (gi + 1)]
        mixed.append(jnp.dot(pooled.astype(BF16), poolw_ref[gi], preferred_element_type=F32))
    y_c = jnp.concatenate(mixed, axis=1) * pscale_ref[...]

    udb = ud.astype(BF16)
    hw, hs = S5_WIDTH // 2, S5_STATE // 2
    for hf in range(2):
        cols = slice(hs * hf, hs * (hf + 1))
        uh = udb[:, hw * hf:hw * (hf + 1)]
        xre_scr[:, cols] = jnp.dot(uh, wbre_ref[hw * hf:hw * (hf + 1), cols], preferred_element_type=F32)
        xim_scr[:, cols] = jnp.dot(uh, wbim_ref[hw * hf:hw * (hf + 1), cols], preferred_element_type=F32)
    rowi = lax.broadcasted_iota(jnp.int32, (SUBLANES, S5_STATE), 0)
    pw_re = pwre_ref[...]
    pw_im = pwim_ref[...]
    step_pw = [(jnp.where(rowi >= d, pw_re[d - 1:d, :], 0.0), jnp.where(rowi >= d, pw_im[d - 1:d, :], 0.0))
               for d in (1, 2, 4)]

    def slab(s, carry):
        cr, ci = carry
        rows = pl.ds(pl.multiple_of(s * SUBLANES, SUBLANES), SUBLANES)
        xr = xre_scr[rows, :]
        xi = xim_scr[rows, :]
        for dsh, (pr, pi) in zip((1, 2, 4), step_pw):
            sr = pltpu.roll(xr, dsh, axis=0)
            si = pltpu.roll(xi, dsh, axis=0)
            xr, xi = xr + (pr * sr - pi * si), xi + (pr * si + pi * sr)
        xr, xi = xr + (pw_re * cr - pw_im * ci), xi + (pw_re * ci + pw_im * cr)
        xre_scr[rows, :] = xr
        xim_scr[rows, :] = xi
        return xr[SUBLANES - 1:SUBLANES, :], xi[SUBLANES - 1:SUBLANES, :]

    cr, ci = lax.fori_loop(0, tm // SUBLANES, slab, (car_re[...], car_im[...]))
    car_re[...] = cr
    car_im[...] = ci

    yh = []
    for hf in range(2):
        rws = slice(hs * hf, hs * (hf + 1))
        cls = slice(hw * hf, hw * (hf + 1))
        yh.append(jnp.dot(xre_scr[:, rws].astype(BF16), wcre_ref[rws, cls], preferred_element_type=F32)
                  - jnp.dot(xim_scr[:, rws].astype(BF16), wcim_ref[rws, cls], preferred_element_type=F32))
    y = jnp.concatenate(yh, axis=1) + dskip_ref[...] * ud
    z = _gelu(y)
    y_d = z * _sigmoid(jnp.dot(z.astype(BF16), wglu_ref[...], preferred_element_type=F32) + bglu_ref[...])

    ycat = jnp.concatenate([y_c, y_d], axis=1).astype(BF16)
    o_ref[...] = h_ref[...] + jnp.dot(ycat, wout_ref[...], preferred_element_type=F32)


def odd_mixer(h, u, consts, bsz, seq, tm):
    nj = seq // tm
    tok = lambda b, j: (b * nj + j, 0)
    full2 = lambda b, j: (0, 0)
    full3 = lambda b, j: (0, 0, 0)
    (poolw, pscale, wbre, wbim, wcre, wcim, pwre, pwim, dskip, wglu, bglu, wout) = consts
    return pl.pallas_call(
        _odd_mixer_kernel,
        grid=(bsz, nj),
        in_specs=[
            pl.BlockSpec((tm, D_MODEL), tok),
            pl.BlockSpec((tm, 1024), tok),
            pl.BlockSpec((4, 128, 128), full3),
            pl.BlockSpec((1, POOL_WIDTH), full2),
            pl.BlockSpec((S5_WIDTH, S5_STATE), full2),
            pl.BlockSpec((S5_WIDTH, S5_STATE), full2),
            pl.BlockSpec((S5_STATE, S5_WIDTH), full2),
            pl.BlockSpec((S5_STATE, S5_WIDTH), full2),
            pl.BlockSpec((SUBLANES, S5_STATE), full2),
            pl.BlockSpec((SUBLANES, S5_STATE), full2),
            pl.BlockSpec((1, S5_WIDTH), full2),
            pl.BlockSpec((S5_WIDTH, S5_WIDTH), full2),
            pl.BlockSpec((1, S5_WIDTH), full2),
            pl.BlockSpec((1024, D_MODEL), full2),
        ],
        out_specs=pl.BlockSpec((tm, D_MODEL), tok),
        out_shape=jax.ShapeDtypeStruct(h.shape, F32),
        scratch_shapes=[
            pltpu.VMEM((POOL_WINDOWS[-1], POOL_WIDTH), F32),
            pltpu.VMEM((1, S5_STATE), F32),
            pltpu.VMEM((1, S5_STATE), F32),
            pltpu.VMEM((tm, S5_STATE), F32),
            pltpu.VMEM((tm, S5_STATE), F32),
        ],
        compiler_params=pltpu.CompilerParams(dimension_semantics=("arbitrary", "arbitrary"),
                                             vmem_limit_bytes=VMEM_LIMIT),
        name="odd_mixer",
    )(h, u, poolw, pscale, wbre, wbim, wcre, wcim, pwre, pwim, dskip, wglu, bglu, wout)


def _top16_rows(s, ids, id_bound):
    vals, idxs = [], []
    for _ in range(PEER_TOPK):
        m = jnp.max(s, axis=0, keepdims=True)
        am = jnp.min(jnp.where(s == m, ids, float(id_bound)), axis=0, keepdims=True)
        vals.append(m)
        idxs.append(am)
        s = jnp.where(ids == am, -jnp.inf, s)
    return jnp.concatenate(vals, axis=0), jnp.concatenate(idxs, axis=0)


def _batcher_pairs(lo, hi):
    def merge(lo, hi, r):
        step = 2 * r
        if step < hi - lo:
            yield from merge(lo, hi, step)
            yield from merge(lo + r, hi, step)
            yield from ((i, i + r) for i in range(lo + r, hi - r, step))
        else:
            yield (lo, lo + r)

    if hi > lo:
        mid = lo + (hi - lo) // 2
        yield from _batcher_pairs(lo, mid)
        yield from _batcher_pairs(mid + 1, hi)
        yield from merge(lo, hi, 1)


_SORT16 = tuple(_batcher_pairs(0, PEER_TOPK - 1))
_BITONIC16 = tuple((i, i + d) for d in (8, 4, 2, 1) for i in range(PEER_TOPK) if not i & d)


def _top16_network(s):
    n = PEER_TOPK
    v = [s[SUBLANES * i:SUBLANES * (i + 1), :] for i in range(n)]
    sub = lax.broadcasted_iota(jnp.int32, (SUBLANES, s.shape[1]), 0).astype(F32)
    k = [sub + float(SUBLANES * i) for i in range(n)]

    def exchange(i, j):
        swap = v[j] > v[i]
        v[i], v[j] = jnp.where(swap, v[j], v[i]), jnp.where(swap, v[i], v[j])
        k[i], k[j] = jnp.where(swap, k[j], k[i]), jnp.where(swap, k[i], k[j])

    for i, j in _SORT16:
        exchange(i, j)
    for shift in (4, 2, 1):
        bv = [pltpu.roll(x, shift, axis=0) for x in v]
        bk = [pltpu.roll(x, shift, axis=0) for x in k]
        for i in range(n):
            take = bv[n - 1 - i] > v[i]
            v[i] = jnp.where(take, bv[n - 1 - i], v[i])
            k[i] = jnp.where(take, bk[n - 1 - i], k[i])
        for i, j in _BITONIC16:
            exchange(i, j)
    tie = jnp.zeros_like(v[0])
    for i in range(n - 1):
        tie = jnp.where(v[i] == v[i + 1], 1.0, tie)
    cnt = jnp.zeros_like(v[0])
    for i in range(n):
        cnt = cnt + jnp.where(s[SUBLANES * i:SUBLANES * (i + 1), :] >= v[n - 1], 1.0, 0.0)
    for shift in (4, 2, 1):
        cnt = cnt + pltpu.roll(cnt, shift, axis=0)
    tie = jnp.where(cnt > float(n), 1.0, tie)
    return (jnp.concatenate([x[0:1, :] for x in v], axis=0), jnp.concatenate([x[0:1, :] for x in k], axis=0), tie)


_PAIR_BLOCKS = (("b", 0, 0), ("b", 1, 0), ("b", 2, 0), ("b", 3, 0), ("b", 4, 0),
                ("a", 0, 8), ("a", 0, 0), ("a", 1, 0), ("b", 0, 8))
_PAIR_ID_BOUND = 4 * PEER_TOPK * PEER_TOPK


def _pair_block_ids(tm):
    r = lax.broadcasted_iota(jnp.int32, (SUBLANES, tm), 0).astype(F32)
    seen = set()
    out = []
    for side, fixed, start in _PAIR_BLOCKS:
        ids = jnp.zeros((SUBLANES, tm), F32)
        for q in range(SUBLANES):
            i, j = (fixed, start + q) if side == "a" else (start + q, fixed)
            ok = (i + 1) * (j + 1) <= PEER_TOPK and (i, j) not in seen
            seen.add((i, j))
            ids = jnp.where(r == q, float(i * PEER_TOPK + j if ok else _PAIR_ID_BOUND + len(seen)), ids)
        out.append(ids)
    assert len({p for p in seen if (p[0] + 1) * (p[1] + 1) <= PEER_TOPK}) == 50
    return jnp.concatenate(out, axis=0)


def _pair_block_sums(av, bv):
    out = []
    for side, fixed, start in _PAIR_BLOCKS:
        if side == "a":
            out.append(av[fixed:fixed + 1, :] + bv[start:start + SUBLANES, :])
        else:
            out.append(av[start:start + SUBLANES, :] + bv[fixed:fixed + 1, :])
    return jnp.concatenate(out, axis=0)


def _take16(table, sel):
    out = jnp.zeros(sel.shape, table.dtype)
    for i in range(PEER_TOPK):
        out = jnp.where(sel == i, table[i:i + 1, :], out)
    return out


def _dot3(ah, al, bh, bl, dims):
    return (lax.dot_general(ah, bh, dims, preferred_element_type=F32)
            + lax.dot_general(al, bh, dims, preferred_element_type=F32)
            + lax.dot_general(ah, bl, dims, preferred_element_type=F32))


def _score_weight_kernel(wq_ref, key_ref, o_ref):
    kh, kl = _split_bf16(key_ref[0])
    wh, wl = _split_bf16(wq_ref[...])
    o_ref[...] = _dot3(kh, kl, wh, wl, NT_DIMS)


def score_weights(w_q, keys):
    d, nq = w_q.shape
    n_hp, nk, dk = keys.shape
    return pl.pallas_call(
        _score_weight_kernel,
        grid=(n_hp,),
        in_specs=[pl.BlockSpec((d, dk), lambda i: (0, i)), pl.BlockSpec((1, nk, dk), lambda i: (i, 0, 0))],
        out_specs=pl.BlockSpec((nk, d), lambda i: (i, 0)),
        out_shape=jax.ShapeDtypeStruct((n_hp * nk, d), F32),
        compiler_params=pltpu.CompilerParams(dimension_semantics=("parallel",)),
        name="score_weights",
    )(w_q, keys)


def _peer_route_kernel(h_ref, nw_ref, wsh_ref, wsl_ref, xn_ref, exp_ref, gate_ref, qt_scr, et_scr, gt_scr):
    xn = _rms(h_ref[...], nw_ref[...])
    for r in range(SUBLANES):
        xn_ref[:, r, :] = xn[:, 128 * r:128 * (r + 1)]
    xh, xl = _split_bf16(xn)
    qt_scr[...] = _dot3(wsh_ref[...], wsl_ref[...], xh, xl, NT_DIMS)
    tm = h_ref.shape[0]
    key_ids = lax.broadcasted_iota(jnp.int32, (PEER_NKEYS, tm), 0).astype(F32)
    pair_ids = _pair_block_ids(tm)
    pair_ok = pair_ids < float(_PAIR_ID_BOUND)

    def head(hd, tie, exact):
        sa = qt_scr[pl.ds(pl.multiple_of(hd * 256, 256), 128), :]
        sb = qt_scr[pl.ds(pl.multiple_of(hd * 256 + 128, 128), 128), :]
        if exact:
            av, ai = _top16_rows(sa, key_ids, PEER_NKEYS)
            bv, bi = _top16_rows(sb, key_ids, PEER_NKEYS)
        else:
            av, ai, ta = _top16_network(sa)
            bv, bi, tb_ = _top16_network(sb)
            tie = jnp.maximum(tie, jnp.maximum(ta, tb_))
        cand = jnp.where(pair_ok, _pair_block_sums(av, bv), -jnp.inf)
        cv, flat = _top16_rows(cand, pair_ids, _PAIR_ID_BOUND)
        flat = flat.astype(jnp.int32)
        e_a = _take16(ai, flat >> 4)
        e_b = _take16(bi, flat & (PEER_TOPK - 1))
        ex = jnp.exp(cv - cv[0:1, :])
        rows = pl.ds(pl.multiple_of(hd * PEER_TOPK, PEER_TOPK), PEER_TOPK)
        et_scr[rows, :] = (e_a * float(PEER_NKEYS) + e_b) * float(PEER_ROW_WORDS)
        gt_scr[rows, :] = ex / jnp.sum(ex, axis=0, keepdims=True)
        return tie

    def head_pair(i, tie):
        return head(2 * i + 1, head(2 * i, tie, exact=False), exact=False)

    tie = lax.fori_loop(0, PEER_HEADS // 2, head_pair, jnp.zeros((SUBLANES, tm), F32))

    @pl.when(jnp.max(tie) > 0.0)
    def _():
        lax.fori_loop(0, PEER_HEADS, functools.partial(head, exact=True), jnp.zeros((SUBLANES, tm), F32))

    exp_ref[...] = et_scr[...].T.astype(jnp.int32)
    gate_ref[...] = gt_scr[...].T


def peer_route(h, nw, w_q, keys, tm):
    t, d = h.shape
    wsh, wsl = _split_bf16(score_weights(w_q, keys))
    nq = wsh.shape[0]
    return pl.pallas_call(
        _peer_route_kernel,
        grid=(t // tm,),
        in_specs=[
            pl.BlockSpec((tm, d), lambda i: (i, 0)),
            pl.BlockSpec((1, d), lambda i: (0, 0)),
            pl.BlockSpec((nq, d), lambda i: (0, 0)),
            pl.BlockSpec((nq, d), lambda i: (0, 0)),
        ],
        out_specs=[
            pl.BlockSpec((tm, SUBLANES, d // SUBLANES), lambda i: (i, 0, 0)),
            pl.BlockSpec((tm, PEER_HK), lambda i: (i, 0)),
            pl.BlockSpec((tm, PEER_HK), lambda i: (i, 0)),
        ],
        out_shape=[jax.ShapeDtypeStruct((t, SUBLANES, d // SUBLANES), F32),
                   jax.ShapeDtypeStruct((t, PEER_HK), jnp.int32), jax.ShapeDtypeStruct((t, PEER_HK), F32)],
        scratch_shapes=[pltpu.VMEM((nq, tm), F32), pltpu.VMEM((PEER_HK, tm), F32), pltpu.VMEM((PEER_HK, tm), F32)],
        compiler_params=pltpu.CompilerParams(dimension_semantics=("parallel",), vmem_limit_bytes=VMEM_LIMIT),
        name="peer_route",
    )(h, nw, wsh, wsl)


PEER_GROUP = 16
PEER_ROW_WORDS = 4


def _gather_group(idx_ref, tab_ref, g, stage_ref):
    rows = [idx_ref.at[g * PEER_GROUP + j] for j in range(PEER_GROUP)]
    for k in range(PEER_HK):
        for j in range(PEER_GROUP):
            off = pl.multiple_of(rows[j][k], PEER_ROW_WORDS)
            stage_ref[j, pl.ds(PEER_ROW_WORDS * k, PEER_ROW_WORDS), :] = tab_ref[pl.ds(off, PEER_ROW_WORDS), :]


def _gather_compute_pipeline(n_groups, idx_ref, idx_next_ref, tab_ref, consume, stage_a, stage_b):
    def compute(g, stage_ref):
        for j in range(PEER_GROUP):
            consume(g * PEER_GROUP + j, stage_ref.at[j])

    @pl.when(pl.program_id(0) == 0)
    def _():
        _gather_group(idx_ref, tab_ref, 0, stage_a)

    def body(i, carry):
        compute(2 * i, stage_a)
        _gather_group(idx_ref, tab_ref, 2 * i + 1, stage_b)
        compute(2 * i + 1, stage_b)
        _gather_group(idx_ref, tab_ref, 2 * i + 2, stage_a)
        return carry

    lax.fori_loop(0, n_groups // 2 - 1, body, 0)
    compute(n_groups - 2, stage_a)
    _gather_group(idx_ref, tab_ref, n_groups - 1, stage_b)
    compute(n_groups - 1, stage_b)
    _gather_group(idx_next_ref, tab_ref, 0, stage_a)


def _diag_mask():
    row = lax.broadcasted_iota(jnp.int32, (SUBLANES, SUBLANES * PEER_HK), 0)
    lane = lax.broadcasted_iota(jnp.int32, (SUBLANES, SUBLANES * PEER_HK), 1)
    return (lane & (SUBLANES - 1)) == row


def _peer_hidden_kernel(idx_ref, idx_next_ref, x_ref, g_ref, sel_ref, tab_ref, o_ref, stage_a, stage_b, part_scr):
    tb = x_ref.shape[0]
    diag = _diag_mask()

    def consume(t, rows_ref):
        u = pltpu.bitcast(rows_ref[...], BF16)
        xh, xl = _split_bf16(x_ref[t])
        x16 = jnp.concatenate([xh, xl], axis=0)
        out = lax.dot_general(x16, u, NT_DIMS, preferred_element_type=F32)
        o8 = out[0:SUBLANES] + out[SUBLANES:2 * SUBLANES]
        part_scr[pl.ds(t, 1), :] = jnp.sum(jnp.where(diag, o8, 0.0), axis=0, keepdims=True)

    _gather_compute_pipeline(tb // PEER_GROUP, idx_ref, idx_next_ref, tab_ref, consume, stage_a, stage_b)
    ph, plo = _split_bf16(part_scr[...])
    sel = sel_ref[...]
    hid = jnp.dot(ph, sel, preferred_element_type=F32) + jnp.dot(plo, sel, preferred_element_type=F32)
    o_ref[...] = g_ref[...] * _gelu(hid)


def peer_hidden(idx, x8, gates, sel, tab, tb):
    t = idx.shape[0]
    return pl.pallas_call(
        _peer_hidden_kernel,
        grid=(t // tb,),
        in_specs=[
            pl.BlockSpec((tb, PEER_HK), lambda i: (i, 0), memory_space=pltpu.SMEM),
            pl.BlockSpec((tb, PEER_HK), lambda i: (jnp.minimum(i + 1, t // tb - 1), 0), memory_space=pltpu.SMEM),
            pl.BlockSpec((tb, SUBLANES, 128), lambda i: (i, 0, 0)),
            pl.BlockSpec((tb, PEER_HK), lambda i: (i, 0)),
            pl.BlockSpec((SUBLANES * PEER_HK, PEER_HK), lambda i: (0, 0)),
            pl.BlockSpec((PEER_ROW_WORDS * PEER_EXPERTS, 128), lambda i: (0, 0), pipeline_mode=pl.Buffered(1)),
        ],
        out_specs=pl.BlockSpec((tb, PEER_HK), lambda i: (i, 0)),
        out_shape=jax.ShapeDtypeStruct((t, PEER_HK), F32),
        scratch_shapes=[pltpu.VMEM((PEER_GROUP, PEER_ROW_WORDS * PEER_HK, 128), jnp.int32),
                        pltpu.VMEM((PEER_GROUP, PEER_ROW_WORDS * PEER_HK, 128), jnp.int32),
                        pltpu.VMEM((tb, SUBLANES * PEER_HK), F32)],
        compiler_params=pltpu.CompilerParams(dimension_semantics=("arbitrary",), vmem_limit_bytes=VMEM_LIMIT_BIG),
        name="peer_hidden",
    )(idx, idx, x8, gates, sel, tab)


def _peer_out_kernel(idx_ref, idx_next_ref, w_ref, exp_ref, tab_ref, o_ref, stage_a, stage_b, wexp_scr):
    tb = w_ref.shape[0]
    diag = _diag_mask()
    wh, wl = _split_bf16(w_ref[...])
    ex = exp_ref[...]
    wexp_scr[...] = jnp.dot(wh, ex, preferred_element_type=F32) + jnp.dot(wl, ex, preferred_element_type=F32)

    def consume(t, rows_ref):
        v = pltpu.bitcast(rows_ref[...], BF16)
        w8 = jnp.where(diag, jnp.broadcast_to(wexp_scr[pl.ds(t, 1), :], (SUBLANES, SUBLANES * PEER_HK)), 0.0)
        w8h, w8l = _split_bf16(w8)
        w16 = jnp.concatenate([w8h, w8l], axis=0)
        out = jnp.dot(w16, v, preferred_element_type=F32)
        o_ref[t] = out[0:SUBLANES] + out[SUBLANES:2 * SUBLANES]

    _gather_compute_pipeline(tb // PEER_GROUP, idx_ref, idx_next_ref, tab_ref, consume, stage_a, stage_b)


def peer_out(idx, w, expand, tab, tb):
    t = idx.shape[0]
    return pl.pallas_call(
        _peer_out_kernel,
        grid=(t // tb,),
        in_specs=[
            pl.BlockSpec((tb, PEER_HK), lambda i: (i, 0), memory_space=pltpu.SMEM),
            pl.BlockSpec((tb, PEER_HK), lambda i: (jnp.minimum(i + 1, t // tb - 1), 0), memory_space=pltpu.SMEM),
            pl.BlockSpec((tb, PEER_HK), lambda i: (i, 0)),
            pl.BlockSpec((PEER_HK, SUBLANES * PEER_HK), lambda i: (0, 0)),
            pl.BlockSpec((PEER_ROW_WORDS * PEER_EXPERTS, 128), lambda i: (0, 0), pipeline_mode=pl.Buffered(1)),
        ],
        out_specs=pl.BlockSpec((tb, SUBLANES, 128), lambda i: (i, 0, 0)),
        out_shape=jax.ShapeDtypeStruct((t, SUBLANES, 128), F32),
        scratch_shapes=[pltpu.VMEM((PEER_GROUP, PEER_ROW_WORDS * PEER_HK, 128), jnp.int32),
                        pltpu.VMEM((PEER_GROUP, PEER_ROW_WORDS * PEER_HK, 128), jnp.int32),
                        pltpu.VMEM((tb, SUBLANES * PEER_HK), F32)],
        compiler_params=pltpu.CompilerParams(dimension_semantics=("arbitrary",), vmem_limit_bytes=VMEM_LIMIT_BIG),
        name="peer_out",
    )(idx, idx, w, expand, tab)


def _ple_kernel(h_ref, e8_ref, p_ref, nw_ref, wg_ref, wp_ref, fw_ref, o_ref, *, final_norm):
    h = h_ref[...] + jnp.concatenate([e8_ref[:, r, :] for r in range(SUBLANES)], axis=1)
    gate = _sigmoid(jnp.dot(_rms(h, nw_ref[...]).astype(BF16), wg_ref[...], preferred_element_type=F32))
    out = h + jnp.dot(p_ref[...].astype(BF16), wp_ref[...], preferred_element_type=F32) * gate
    if final_norm:
        out = _rms(out, fw_ref[...])
    o_ref[...] = out


def ple(h, e8, p, layer, nw, wg, wp, fw, tm, final_norm):
    t, d = h.shape
    pd = p.shape[1]
    first = layer * (t // tm)
    return pl.pallas_call(
        functools.partial(_ple_kernel, final_norm=final_norm),
        grid=(t // tm,),
        in_specs=[
            pl.BlockSpec((tm, d), lambda i: (i, 0)),
            pl.BlockSpec((tm, SUBLANES, d // SUBLANES), lambda i: (i, 0, 0)),
            pl.BlockSpec((tm, pd), lambda i: (first + i, 0)),
            pl.BlockSpec((1, d), lambda i: (0, 0)),
            pl.BlockSpec((d, d), lambda i: (0, 0)),
            pl.BlockSpec((pd, d), lambda i: (0, 0)),
            pl.BlockSpec((1, d), lambda i: (0, 0)),
        ],
        out_specs=pl.BlockSpec((tm, d), lambda i: (i, 0)),
        out_shape=jax.ShapeDtypeStruct((t, d), F32),
        compiler_params=pltpu.CompilerParams(dimension_semantics=("parallel",), vmem_limit_bytes=VMEM_LIMIT),
        name="ple",
    )(h, e8, p, nw, wg, wp, fw)


def _pack_table_kernel(t_ref, o_ref):
    x = t_ref[...]
    te = x.shape[0]
    for s in range(PEER_ROW_WORDS):
        lo = pltpu.bitcast(x[:, 256 * s:256 * s + 128].astype(BF16).astype(F32), jnp.int32)
        hi = pltpu.bitcast(x[:, 256 * s + 128:256 * s + 256].astype(BF16).astype(F32), jnp.int32)
        o_ref[pl.ds(s, te, stride=PEER_ROW_WORDS), :] = (hi & jnp.int32(-65536)) | lax.shift_right_logical(lo, 16)


def _pack_table(tab):
    e, d = tab.shape
    te = min(512, e)
    return pl.pallas_call(
        _pack_table_kernel,
        grid=(e // te,),
        in_specs=[pl.BlockSpec((te, d), lambda i: (i, 0))],
        out_specs=pl.BlockSpec((PEER_ROW_WORDS * te, 128), lambda i: (i, 0)),
        out_shape=jax.ShapeDtypeStruct((PEER_ROW_WORDS * e, 128), jnp.int32),
        compiler_params=pltpu.CompilerParams(dimension_semantics=("parallel",), vmem_limit_bytes=VMEM_LIMIT),
        name="pack_table",
    )(tab)


def _even_in_perm():
    off = np.cumsum([0, 256, 256, 512, 512, 16, 256, 256, 512, 512])
    qa, ka, va, ga, ra, qb, kb, vb, gb = [np.arange(off[i], off[i + 1]) for i in range(9)]
    half = RET_DK // 2
    rot = np.concatenate([np.concatenate([np.arange(h * RET_DK, h * RET_DK + half) for h in range(RET_HEADS)]),
                          np.concatenate([np.arange(h * RET_DK + half, (h + 1) * RET_DK) for h in range(RET_HEADS)])])
    return np.concatenate([qa, ka, qb[rot], kb[rot], va, ga, vb, gb, ra])


def _retention_tables():
    lg = np.log(1.0 - 2.0 ** (-5.0 - np.arange(RET_HEADS, dtype=np.float64)))
    idx = np.arange(CHUNK, dtype=np.float64)
    diff = idx[:, None] - idx[None, :]
    dmask = np.where(diff >= 0, np.exp(lg[:, None, None] * np.maximum(diff, 0.0)), 0.0)
    qdec = np.exp(lg[:, None] * (idx + 1.0))
    kdec = np.exp(lg[:, None] * (CHUNK - 1.0 - idx))
    cdec = np.exp(lg * CHUNK)
    per_col = lambda t: np.repeat(t.T, RET_DV, axis=1)
    lanes = np.arange(256)
    rows_head = np.repeat(np.arange(RET_HEADS), RET_DV)[:, None]
    half = RET_DK // 2
    hmask_g = (lanes[None, :] // GLA_DK) == rows_head
    hmask_r = ((lanes[None, :] % 128) // half == rows_head) & ((lanes[None, :] % 128) < RET_HEADS * half)
    return (jnp.asarray(dmask, F32), jnp.asarray(per_col(kdec), F32), jnp.asarray(per_col(qdec), F32),
            jnp.asarray(np.broadcast_to(np.repeat(cdec, RET_DV)[:, None], (RET_HEADS * RET_DV, 256)), F32),
            jnp.asarray(hmask_g, F32), jnp.asarray(hmask_r, F32))


def _peer_layer(h, nw, w_q, sub_keys, u_tab, v_tab, tm_route, tb):
    keys = sub_keys.reshape(2 * PEER_HEADS, PEER_NKEYS, -1)
    xn8, experts, gates = peer_route(h, nw, w_q, keys, tm_route)
    kk = np.arange(SUBLANES * PEER_HK) // SUBLANES
    sel = jnp.asarray(kk[:, None] == np.arange(PEER_HK)[None, :], BF16)
    w = peer_hidden(experts, xn8, gates, sel, _pack_table(u_tab), tb)
    return peer_out(experts, w, sel.T, _pack_table(v_tab), tb)


def _row(v):
    return v.reshape(1, -1).astype(F32)


def _layer0_mixer(h, w, bsz, seq):
    t = h.shape[0]
    w_in = w["ev_w_in"][0]
    w_in0 = jnp.pad(w_in[:, _even_in_perm()], ((0, 0), (0, EVEN_COLS - w_in.shape[1]))).astype(BF16)
    proj = norm_matmul(h, _row(w["norm_mix_w"][0]), w_in0, min(256, t))
    half = RET_DK // 2
    freqs = ROPE_BASE ** (-np.arange(half, dtype=np.float32) / half)
    dmask, kdec, qdec, cdec, hmask_g, hmask_r = _retention_tables()
    consts = (
        jnp.asarray(np.tile(freqs, RET_HEADS)[None, :], F32),
        jnp.pad(w["ev_gla_w_up"][0], ((0, 128 - GLA_GATE_RANK), (0, 0))).astype(BF16),
        _row(w["ev_gla_b_up"][0]), _row(w["ev_gla_norm_w"][0]), _row(w["ev_ret_norm_w"][0]),
        w["ev_w_out"][0].astype(BF16),
        jnp.asarray(np.tril(np.ones((CHUNK, CHUNK))), BF16),
        dmask, kdec, qdec, cdec, hmask_g, hmask_r,
    )
    pos = w["positions"].astype(F32).reshape(seq, 1)
    return even_mixer(h, proj, pos, consts, bsz, seq, min(256, seq))


def _layer1_mixer(h, w, bsz, seq):
    t = h.shape[0]
    u = norm_matmul(h, _row(w["norm_mix_w"][1]), w["od_w_in"][0].astype(BF16), min(512, t))
    pw_re, pw_im, bb_re, bb_im = s5_params(w["od_s5_a_re"][0], w["od_s5_a_im"][0], w["od_s5_log_dt"][0].reshape(-1, 1),
                                           w["od_s5_b_re"][0].transpose(0, 2, 1), w["od_s5_b_im"][0].transpose(0, 2, 1))
    eye = jnp.eye(S5_GROUPS, dtype=F32)
    blockdiag = lambda m: (m[:, :, None, :] * eye[:, None, :, None]).reshape(m.shape[0] * m.shape[1], -1)
    consts = (
        w["od_pool_w"][0].astype(BF16), _row(w["od_pool_scale"][0]),
        blockdiag(bb_re).astype(BF16), blockdiag(bb_im).astype(BF16),
        blockdiag(w["od_s5_c_re"][0]).T.astype(BF16), blockdiag(w["od_s5_c_im"][0]).T.astype(BF16),
        pw_re.reshape(SUBLANES, S5_STATE), pw_im.reshape(SUBLANES, S5_STATE),
        _row(w["od_s5_d"][0]), w["od_s5_w_glu"][0].astype(BF16), _row(w["od_s5_b_glu"][0]), w["od_w_out"][0].astype(BF16),
    )
    return odd_mixer(h, u, consts, bsz, seq, min(256, seq))


def kernel(x, p, positions, norm_mix_w, norm_ffn_w, norm_ple_w, final_norm_w, ev_w_in, ev_gla_w_up, ev_gla_b_up, ev_gla_norm_w, ev_ret_norm_w, ev_w_out, od_w_in, od_pool_w, od_pool_scale, od_s5_a_re, od_s5_a_im, od_s5_log_dt, od_s5_b_re, od_s5_b_im, od_s5_c_re, od_s5_c_im, od_s5_d, od_s5_w_glu, od_s5_b_glu, od_w_out, peer_w_q, peer_sub_keys, peer_u, peer_v, ple_w_proj, ple_w_gate):
    w = dict(locals())
    bsz, seq, d = x.shape
    t = bsz * seq
    tm_tok = min(512, t)
    tm_route = min(256, t)
    tb = min(256, t)
    h = x.reshape(t, d)
    for i, mixer in enumerate((_layer0_mixer, _layer1_mixer)):
        h = mixer(h, w, bsz, seq)
        e8 = _peer_layer(h, _row(norm_ffn_w[i]), peer_w_q[i], peer_sub_keys[i], peer_u[i], peer_v[i], tm_route, tb)
        h = ple(h, e8, p.reshape(-1, p.shape[-1]), i, _row(norm_ple_w[i]), ple_w_gate[i].astype(BF16),
                ple_w_proj[i].astype(BF16), _row(final_norm_w), tm_tok, i == 1)
    return h.reshape(bsz, seq, d)
```

```python
import functools
import math

import numpy as np
import jax
import jax.numpy as jnp
from jax import lax
from jax.experimental import pallas as pl
from jax.experimental.pallas import tpu as pltpu

F32 = jnp.float32
BF16 = jnp.bfloat16

D_MODEL = 1024
NORM_EPS = 1e-6
CHUNK = 64
GLA_HEADS = 4
GLA_DK = 64
GLA_DV = 128
GLA_GATE_RANK = 16
GLA_GATE_NORM = 16.0
RET_HEADS = 4
RET_DK = 64
RET_DV = 128
ROPE_BASE = 10000.0
POOL_WINDOWS = (2, 4, 8, 16)
POOL_WIDTH = 512
POOL_GROUP_WIDTH = 128
S5_H = 16
S5_P = 64
S5_GROUPS = 32
S5_WIDTH = 512
S5_STATE = S5_GROUPS * S5_P
PEER_HEADS = 8
PEER_NKEYS = 128
PEER_TOPK = 16
PEER_HK = PEER_HEADS * PEER_TOPK
PEER_EXPERTS = PEER_NKEYS * PEER_NKEYS
EVEN_COLS = 3200

VMEM_LIMIT_BIG = 56 * 1024 * 1024
VMEM_LIMIT = 48 * 1024 * 1024
SUBLANES = 8

NT_DIMS = (((1,), (1,)), ((), ()))


def _split_bf16(x):
    hi = x.astype(BF16)
    lo = (x - hi.astype(F32)).astype(BF16)
    return hi, lo


def _rms(x, w):
    return x * lax.rsqrt(jnp.mean(x * x, axis=-1, keepdims=True) + NORM_EPS) * w


def _sigmoid(x):
    return 1.0 / (1.0 + jnp.exp(-x))


def _gelu(x):
    return 0.5 * x * (1.0 + lax.erf(x * (2.0 ** -0.5)))


def _log_sigmoid(z):
    return jnp.minimum(z, 0.0) - jnp.log1p(jnp.exp(-jnp.abs(z)))


def _norm_mm_kernel(h_ref, nw_ref, w_ref, o_ref):
    xn = _rms(h_ref[...], nw_ref[...])
    o_ref[...] = jnp.dot(xn.astype(BF16), w_ref[...], preferred_element_type=F32)


def norm_matmul(h, nw, w, tm):
    t, d = h.shape
    n = w.shape[1]
    return pl.pallas_call(
        _norm_mm_kernel,
        grid=(t // tm,),
        in_specs=[
            pl.BlockSpec((tm, d), lambda i: (i, 0)),
            pl.BlockSpec((1, d), lambda i: (0, 0)),
            pl.BlockSpec((d, n), lambda i: (0, 0)),
        ],
        out_specs=pl.BlockSpec((tm, n), lambda i: (i, 0)),
        out_shape=jax.ShapeDtypeStruct((t, n), F32),
        compiler_params=pltpu.CompilerParams(dimension_semantics=("parallel",), vmem_limit_bytes=VMEM_LIMIT),
        name="norm_matmul",
    )(h, nw, w)


def _rotary_table_kernel(pos_ref, freq_ref, cos_ref, sin_ref):
    ang = pos_ref[...] * freq_ref[...]
    cos_ref[...] = jnp.cos(ang)
    sin_ref[...] = jnp.sin(ang)


def rotary_table(pos, freqs, ts):
    seq = pos.shape[0]
    n = freqs.shape[1]
    return pl.pallas_call(
        _rotary_table_kernel,
        grid=(seq // ts,),
        in_specs=[pl.BlockSpec((ts, 1), lambda i: (i, 0)), pl.BlockSpec((1, n), lambda i: (0, 0))],
        out_specs=[pl.BlockSpec((ts, n), lambda i: (i, 0)), pl.BlockSpec((ts, n), lambda i: (i, 0))],
        out_shape=[jax.ShapeDtypeStruct((seq, n), F32), jax.ShapeDtypeStruct((seq, n), F32)],
        compiler_params=pltpu.CompilerParams(dimension_semantics=("parallel",)),
        name="rotary_table",
    )(pos, freqs)


def _even_mixer_kernel(h_ref, qk_ref, va_ref, ga_ref, vb_ref, gb_ref, ra_ref, cos_ref, sin_ref,
                       wup_ref, bup_ref, gnw_ref, rnw_ref, wout_ref, tril_ref,
                       dmask_ref, kdec_ref, qdec_ref, cdec_ref, hmaskg_ref, hmaskr_ref,
                       o_ref, gstate, rstate, y_scr, qin_scr, dec_scr, kvg_scr, kvr_scr):
    tm = h_ref.shape[0]

    @pl.when(pl.program_id(1) == 0)
    def _():
        gstate[...] = jnp.zeros_like(gstate)
        rstate[...] = jnp.zeros_like(rstate)

    lane = lax.broadcasted_iota(jnp.int32, (1, 256), 1)
    gla_masks = [((lane >= GLA_DK * h) & (lane < GLA_DK * (h + 1))).astype(F32) for h in range(GLA_HEADS)]
    half = RET_DK // 2
    ret_masks = [(((lane >= half * h) & (lane < half * (h + 1)))
                  | ((lane >= 128 + half * h) & (lane < 128 + half * (h + 1)))).astype(F32)
                 for h in range(RET_HEADS)]
    ri = lax.broadcasted_iota(jnp.int32, (CHUNK, CHUNK), 0)
    ci = lax.broadcasted_iota(jnp.int32, (CHUNK, CHUNK), 1)
    causal = ri >= ci
    tril = tril_ref[...]
    wup = wup_ref[...]
    bup = bup_ref[...]

    n_chunks = tm // CHUNK
    causal4 = jnp.concatenate([causal] * GLA_HEADS, axis=0)
    dmask4 = jnp.concatenate([dmask_ref[h] for h in range(RET_HEADS)], axis=0)
    hmask_g = hmaskg_ref[...]
    hmask_r = hmaskr_ref[...]

    for c in range(n_chunks):
        rows = slice(c * CHUNK, (c + 1) * CHUNK)
        qk = qk_ref[rows, :]
        qa, ka, qb, kb = qk[:, 0:256], qk[:, 256:512], qk[:, 512:768], qk[:, 768:1024]

        z = jnp.dot(ra_ref[rows, :].astype(BF16), wup, preferred_element_type=F32) + bup
        la = _log_sigmoid(z) * (1.0 / GLA_GATE_NORM)
        lah, lal = _split_bf16(la)
        g = (jnp.dot(tril, lah, preferred_element_type=F32)
             + jnp.dot(tril, lal, preferred_element_type=F32))
        g_last = g[CHUNK - 1:CHUNK, :]
        ref = 0.5 * g_last
        qs = qa * (GLA_DK ** -0.5)
        qt = qs * jnp.exp(g - ref)
        kt = (ka * jnp.exp(ref - g)).astype(BF16)
        kd = ka * jnp.exp(g_last - g)
        qin_scr[rows, 0:256] = qs * jnp.exp(g)
        dec_scr[c] = jnp.exp(g_last)
        q4 = jnp.concatenate([(qt * gla_masks[h]) for h in range(GLA_HEADS)], axis=0).astype(BF16)
        s4 = jnp.where(causal4, lax.dot_general(q4, kt, NT_DIMS, preferred_element_type=F32), 0.0).astype(BF16)
        va = va_ref[rows, :]
        vab = va.astype(BF16)
        for h in range(GLA_HEADS):
            y_scr[rows, GLA_DV * h:GLA_DV * (h + 1)] = jnp.dot(
                s4[CHUNK * h:CHUNK * (h + 1), :], vab[:, GLA_DV * h:GLA_DV * (h + 1)], preferred_element_type=F32)
        kvg_scr[c] = jnp.dot(va.T.astype(BF16), kd.astype(BF16), preferred_element_type=F32) * hmask_g

        cs, sn = cos_ref[rows, :], sin_ref[rows, :]
        q1, q2 = qb[:, 0:128], qb[:, 128:256]
        k1, k2 = kb[:, 0:128], kb[:, 128:256]
        qr = jnp.concatenate([q1 * cs - q2 * sn, q2 * cs + q1 * sn], axis=1)
        kr = jnp.concatenate([k1 * cs - k2 * sn, k2 * cs + k1 * sn], axis=1) * (RET_DK ** -0.5)
        krb = kr.astype(BF16)
        qin_scr[rows, 256:512] = qr
        q4 = jnp.concatenate([(qr * ret_masks[h]) for h in range(RET_HEADS)], axis=0).astype(BF16)
        s4 = (lax.dot_general(q4, krb, NT_DIMS, preferred_element_type=F32) * dmask4).astype(BF16)
        vb = vb_ref[rows, :]
        vbb = vb.astype(BF16)
        for h in range(RET_HEADS):
            y_scr[rows, 512 + RET_DV * h:512 + RET_DV * (h + 1)] = jnp.dot(
                s4[CHUNK * h:CHUNK * (h + 1), :], vbb[:, RET_DV * h:RET_DV * (h + 1)], preferred_element_type=F32)
        kvr_scr[c] = jnp.dot((vb * kdec_ref[...]).T.astype(BF16), krb, preferred_element_type=F32) * hmask_r

    for c in range(n_chunks):
        rows = slice(c * CHUNK, (c + 1) * CHUNK)
        sg = gstate[...]
        y_scr[rows, 0:512] += lax.dot_general(qin_scr[rows, 0:256].astype(BF16), sg.astype(BF16), NT_DIMS,
                                              preferred_element_type=F32)
        gstate[...] = sg * dec_scr[c] + kvg_scr[c]
        sr = rstate[...]
        y_scr[rows, 512:1024] += lax.dot_general(qin_scr[rows, 256:512].astype(BF16), sr.astype(BF16), NT_DIMS,
                                                 preferred_element_type=F32) * qdec_ref[...]
        rstate[...] = sr * cdec_ref[...] + kvr_scr[c]

    pieces = []
    gnw = gnw_ref[...]
    for h in range(GLA_HEADS):
        o = y_scr[:, GLA_DV * h:GLA_DV * (h + 1)]
        gt = ga_ref[:, GLA_DV * h:GLA_DV * (h + 1)]
        pieces.append(_rms(o, gnw) * (gt * _sigmoid(gt)))
    for h in range(RET_HEADS):
        o = y_scr[:, 512 + RET_DV * h:512 + RET_DV * (h + 1)]
        gt = gb_ref[:, RET_DV * h:RET_DV * (h + 1)]
        mu = jnp.mean(o, axis=-1, keepdims=True)
        oc = o - mu
        var = jnp.mean(oc * oc, axis=-1, keepdims=True)
        nrm = oc * lax.rsqrt(var + NORM_EPS) * rnw_ref[:, RET_DV * h:RET_DV * (h + 1)]
        pieces.append(nrm * (gt * _sigmoid(gt)))
    y = jnp.concatenate(pieces, axis=1).astype(BF16)
    o_ref[...] = h_ref[...] + jnp.dot(y, wout_ref[...], preferred_element_type=F32)


def even_mixer(h, proj, pos, consts, bsz, seq, tm):
    nj = seq // tm
    tok = lambda b, j: (b * nj + j, 0)
    col = lambda cb: (lambda b, j: (b * nj + j, cb))
    full2 = lambda b, j: (0, 0)
    full3 = lambda b, j: (0, 0, 0)
    (freqs, wup, bup, gnw, rnw, wout, tril, dmask, kdec, qdec, cdec, hmask_g, hmask_r) = consts
    cos_t, sin_t = rotary_table(pos, freqs, min(512, seq))
    n_state = GLA_HEADS * GLA_DV
    return pl.pallas_call(
        _even_mixer_kernel,
        grid=(bsz, nj),
        in_specs=[
            pl.BlockSpec((tm, D_MODEL), tok),
            pl.BlockSpec((tm, 1024), col(0)),
            pl.BlockSpec((tm, 512), col(2)),
            pl.BlockSpec((tm, 512), col(3)),
            pl.BlockSpec((tm, 512), col(4)),
            pl.BlockSpec((tm, 512), col(5)),
            pl.BlockSpec((tm, 128), col(24)),
            pl.BlockSpec((tm, 128), lambda b, j: (j, 0)),
            pl.BlockSpec((tm, 128), lambda b, j: (j, 0)),
            pl.BlockSpec((128, 256), full2),
            pl.BlockSpec((1, 256), full2),
            pl.BlockSpec((1, 128), full2),
            pl.BlockSpec((1, 512), full2),
            pl.BlockSpec((1024, D_MODEL), full2),
            pl.BlockSpec((CHUNK, CHUNK), full2),
            pl.BlockSpec((RET_HEADS, CHUNK, CHUNK), full3),
            pl.BlockSpec((CHUNK, n_state), full2),
            pl.BlockSpec((CHUNK, n_state), full2),
            pl.BlockSpec((n_state, 256), full2),
            pl.BlockSpec((n_state, 256), full2),
            pl.BlockSpec((n_state, 256), full2),
        ],
        out_specs=pl.BlockSpec((tm, D_MODEL), tok),
        out_shape=jax.ShapeDtypeStruct(h.shape, F32),
        scratch_shapes=[
            pltpu.VMEM((n_state, 256), F32),
            pltpu.VMEM((n_state, 256), F32),
            pltpu.VMEM((tm, 1024), F32),
            pltpu.VMEM((tm, 512), F32),
            pltpu.VMEM((tm // CHUNK, 1, 256), F32),
            pltpu.VMEM((tm // CHUNK, n_state, 256), F32),
            pltpu.VMEM((tm // CHUNK, n_state, 256), F32),
        ],
        compiler_params=pltpu.CompilerParams(dimension_semantics=("arbitrary", "arbitrary"),
                                             vmem_limit_bytes=VMEM_LIMIT),
        name="even_mixer",
    )(h, proj, proj, proj, proj, proj, proj, cos_t, sin_t, wup, bup, gnw, rnw, wout, tril, dmask, kdec, qdec, cdec,
      hmask_g, hmask_r)


def _s5_param_kernel(are_ref, aim_ref, ldt_ref, bre_ref, bim_ref,
                     pw_re_ref, pw_im_ref, bbre_ref, bbim_ref):
    a_re = are_ref[...]
    a_im = aim_ref[...]
    dt = jnp.exp(ldt_ref[...])
    for r in range(SUBLANES):
        mag = jnp.exp(a_re * dt * (r + 1.0))
        pw_re_ref[r] = mag * jnp.cos(a_im * dt * (r + 1.0))
        pw_im_ref[r] = mag * jnp.sin(a_im * dt * (r + 1.0))
    mag = jnp.exp(a_re * dt)
    abar_re, abar_im = mag * jnp.cos(a_im * dt), mag * jnp.sin(a_im * dt)
    den = a_re * a_re + a_im * a_im
    nr, ni = abar_re - 1.0, abar_im
    coef_re = (nr * a_re + ni * a_im) / den
    coef_im = (ni * a_re - nr * a_im) / den
    b_re = bre_ref[...]
    b_im = bim_ref[...]
    c_re = jnp.concatenate([coef_re] * S5_H, axis=1)
    c_im = jnp.concatenate([coef_im] * S5_H, axis=1)
    bbre_ref[...] = c_re * b_re - c_im * b_im
    bbim_ref[...] = c_re * b_im + c_im * b_re


def s5_params(a_re, a_im, log_dt, b_re_t, b_im_t):
    g, p = a_re.shape
    hh = b_re_t.shape[1]
    pw_re, pw_im, bb_re, bb_im = pl.pallas_call(
        _s5_param_kernel,
        out_shape=[jax.ShapeDtypeStruct((SUBLANES, g, p), F32), jax.ShapeDtypeStruct((SUBLANES, g, p), F32),
                   jax.ShapeDtypeStruct((g, hh * p), F32), jax.ShapeDtypeStruct((g, hh * p), F32)],
        name="s5_params",
    )(a_re, a_im, log_dt, b_re_t.reshape(g, hh * p), b_im_t.reshape(g, hh * p))
    return pw_re, pw_im, bb_re.reshape(g, hh, p), bb_im.reshape(g, hh, p)


def _odd_mixer_kernel(h_ref, u_ref, poolw_ref, pscale_ref, wbre_ref, wbim_ref, wcre_ref, wcim_ref,
                      pwre_ref, pwim_ref, dskip_ref, wglu_ref, bglu_ref, wout_ref,
                      o_ref, tail_scr, car_re, car_im, xre_scr, xim_scr):
    tm = h_ref.shape[0]
    j = pl.program_id(1)
    halo = POOL_WINDOWS[-1]

    @pl.when(j == 0)
    def _():
        tail_scr[...] = jnp.zeros_like(tail_scr)
        car_re[...] = jnp.zeros_like(car_re)
        car_im[...] = jnp.zeros_like(car_im)

    uc = u_ref[:, 0:POOL_WIDTH]
    ud = u_ref[:, POOL_WIDTH:POOL_WIDTH + S5_WIDTH]

    ext = jnp.concatenate([tail_scr[...], uc], axis=0)
    tail_scr[...] = uc[tm - halo:tm, :]
    pos = (j * tm + lax.broadcasted_iota(jnp.int32, (tm, 1), 0)).astype(F32)
    mixed = []
    for gi, win in enumerate(POOL_WINDOWS):
        a = ext[:, POOL_GROUP_WIDTH * gi:POOL_GROUP_WIDTH * (gi + 1)]
        n = tm + halo
        step = 1
        end = 0
        while step < win:
            a = a[step:n, :] + a[0:n - step, :]
            n -= step
            end += step
            step *= 2
        wsum = a[halo - end:halo - end + tm, :]
        cnt = jnp.minimum(pos + 1.0, float(win))
        pooled = wsum / cnt - uc[:, POOL_GROUP_WIDTH * gi:POOL_GROUP_WIDTH * (gi + 1)]
        mixed.append(jnp.dot(pooled.astype(BF16), poolw_ref[gi], preferred_element_type=F32))
    y_c = jnp.concatenate(mixed, axis=1) * pscale_ref[...]

    udb = ud.astype(BF16)
    hw, hs = S5_WIDTH // 2, S5_STATE // 2
    for hf in range(2):
        cols = slice(hs * hf, hs * (hf + 1))
        uh = udb[:, hw * hf:hw * (hf + 1)]
        xre_scr[:, cols] = jnp.dot(uh, wbre_ref[hw * hf:hw * (hf + 1), cols], preferred_element_type=F32)
        xim_scr[:, cols] = jnp.dot(uh, wbim_ref[hw * hf:hw * (hf + 1), cols], preferred_element_type=F32)
    rowi = lax.broadcasted_iota(jnp.int32, (SUBLANES, S5_STATE), 0)
    pw_re = pwre_ref[...]
    pw_im = pwim_ref[...]
    step_pw = [(jnp.where(rowi >= d, pw_re[d - 1:d, :], 0.0), jnp.where(rowi >= d, pw_im[d - 1:d, :], 0.0))
               for d in (1, 2, 4)]

    def slab(s, carry):
        cr, ci = carry
        rows = pl.ds(pl.multiple_of(s * SUBLANES, SUBLANES), SUBLANES)
        xr = xre_scr[rows, :]
        xi = xim_scr[rows, :]
        for dsh, (pr, pi) in zip((1, 2, 4), step_pw):
            sr = pltpu.roll(xr, dsh, axis=0)
            si = pltpu.roll(xi, dsh, axis=0)
            xr, xi = xr + (pr * sr - pi * si), xi + (pr * si + pi * sr)
        xr, xi = xr + (pw_re * cr - pw_im * ci), xi + (pw_re * ci + pw_im * cr)
        xre_scr[rows, :] = xr
        xim_scr[rows, :] = xi
        return xr[SUBLANES - 1:SUBLANES, :], xi[SUBLANES - 1:SUBLANES, :]

    cr, ci = lax.fori_loop(0, tm // SUBLANES, slab, (car_re[...], car_im[...]))
    car_re[...] = cr
    car_im[...] = ci

    yh = []
    for hf in range(2):
        rws = slice(hs * hf, hs * (hf + 1))
        cls = slice(hw * hf, hw * (hf + 1))
        yh.append(jnp.dot(xre_scr[:, rws].astype(BF16), wcre_ref[rws, cls], preferred_element_type=F32)
                  - jnp.dot(xim_scr[:, rws].astype(BF16), wcim_ref[rws, cls], preferred_element_type=F32))
    y = jnp.concatenate(yh, axis=1) + dskip_ref[...] * ud
    z = _gelu(y)
    y_d = z * _sigmoid(jnp.dot(z.astype(BF16), wglu_ref[...], preferred_element_type=F32) + bglu_ref[...])

    ycat = jnp.concatenate([y_c, y_d], axis=1).astype(BF16)
    o_ref[...] = h_ref[...] + jnp.dot(ycat, wout_ref[...], preferred_element_type=F32)


def odd_mixer(h, u, consts, bsz, seq, tm):
    nj = seq // tm
    tok = lambda b, j: (b * nj + j, 0)
    full2 = lambda b, j: (0, 0)
    full3 = lambda b, j: (0, 0, 0)
    (poolw, pscale, wbre, wbim, wcre, wcim, pwre, pwim, dskip, wglu, bglu, wout) = consts
    return pl.pallas_call(
        _odd_mixer_kernel,
        grid=(bsz, nj),
        in_specs=[
            pl.BlockSpec((tm, D_MODEL), tok),
            pl.BlockSpec((tm, 1024), tok),
            pl.BlockSpec((4, 128, 128), full3),
            pl.BlockSpec((1, POOL_WIDTH), full2),
            pl.BlockSpec((S5_WIDTH, S5_STATE), full2),
            pl.BlockSpec((S5_WIDTH, S5_STATE), full2),
            pl.BlockSpec((S5_STATE, S5_WIDTH), full2),
            pl.BlockSpec((S5_STATE, S5_WIDTH), full2),
            pl.BlockSpec((SUBLANES, S5_STATE), full2),
            pl.BlockSpec((SUBLANES, S5_STATE), full2),
            pl.BlockSpec((1, S5_WIDTH), full2),
            pl.BlockSpec((S5_WIDTH, S5_WIDTH), full2),
            pl.BlockSpec((1, S5_WIDTH), full2),
            pl.BlockSpec((1024, D_MODEL), full2),
        ],
        out_specs=pl.BlockSpec((tm, D_MODEL), tok),
        out_shape=jax.ShapeDtypeStruct(h.shape, F32),
        scratch_shapes=[
            pltpu.VMEM((POOL_WINDOWS[-1], POOL_WIDTH), F32),
            pltpu.VMEM((1, S5_STATE), F32),
            pltpu.VMEM((1, S5_STATE), F32),
            pltpu.VMEM((tm, S5_STATE), F32),
            pltpu.VMEM((tm, S5_STATE), F32),
        ],
        compiler_params=pltpu.CompilerParams(dimension_semantics=("arbitrary", "arbitrary"),
                                             vmem_limit_bytes=VMEM_LIMIT),
        name="odd_mixer",
    )(h, u, poolw, pscale, wbre, wbim, wcre, wcim, pwre, pwim, dskip, wglu, bglu, wout)


def _top16_rows(s, ids, id_bound):
    vals, idxs = [], []
    for _ in range(PEER_TOPK):
        m = jnp.max(s, axis=0, keepdims=True)
        am = jnp.min(jnp.where(s == m, ids, float(id_bound)), axis=0, keepdims=True)
        vals.append(m)
        idxs.append(am)
        s = jnp.where(ids == am, -jnp.inf, s)
    return jnp.concatenate(vals, axis=0), jnp.concatenate(idxs, axis=0)


def _batcher_pairs(lo, hi):
    def merge(lo, hi, r):
        step = 2 * r
        if step < hi - lo:
            yield from merge(lo, hi, step)
            yield from merge(lo + r, hi, step)
            yield from ((i, i + r) for i in range(lo + r, hi - r, step))
        else:
            yield (lo, lo + r)

    if hi > lo:
        mid = lo + (hi - lo) // 2
        yield from _batcher_pairs(lo, mid)
        yield from _batcher_pairs(mid + 1, hi)
        yield from merge(lo, hi, 1)


_SORT16 = tuple(_batcher_pairs(0, PEER_TOPK - 1))
_BITONIC16 = tuple((i, i + d) for d in (8, 4, 2, 1) for i in range(PEER_TOPK) if not i & d)


def _top16_network(s):
    n = PEER_TOPK
    v = [s[SUBLANES * i:SUBLANES * (i + 1), :] for i in range(n)]
    sub = lax.broadcasted_iota(jnp.int32, (SUBLANES, s.shape[1]), 0).astype(F32)
    k = [sub + float(SUBLANES * i) for i in range(n)]

    def exchange(i, j):
        swap = v[j] > v[i]
        v[i], v[j] = jnp.where(swap, v[j], v[i]), jnp.where(swap, v[i], v[j])
        k[i], k[j] = jnp.where(swap, k[j], k[i]), jnp.where(swap, k[i], k[j])

    for i, j in _SORT16:
        exchange(i, j)
    for shift in (4, 2, 1):
        bv = [pltpu.roll(x, shift, axis=0) for x in v]
        bk = [pltpu.roll(x, shift, axis=0) for x in k]
        for i in range(n):
            take = bv[n - 1 - i] > v[i]
            v[i] = jnp.where(take, bv[n - 1 - i], v[i])
            k[i] = jnp.where(take, bk[n - 1 - i], k[i])
        for i, j in _BITONIC16:
            exchange(i, j)
    tie = jnp.zeros_like(v[0])
    for i in range(n - 1):
        tie = jnp.where(v[i] == v[i + 1], 1.0, tie)
    cnt = jnp.zeros_like(v[0])
    for i in range(n):
        cnt = cnt + jnp.where(s[SUBLANES * i:SUBLANES * (i + 1), :] >= v[n - 1], 1.0, 0.0)
    for shift in (4, 2, 1):
        cnt = cnt + pltpu.roll(cnt, shift, axis=0)
    tie = jnp.where(cnt > float(n), 1.0, tie)
    return (jnp.concatenate([x[0:1, :] for x in v], axis=0), jnp.concatenate([x[0:1, :] for x in k], axis=0), tie)


_PAIR_BLOCKS = (("b", 0, 0), ("b", 1, 0), ("b", 2, 0), ("b", 3, 0), ("b", 4, 0),
                ("a", 0, 8), ("a", 0, 0), ("a", 1, 0), ("b", 0, 8))
_PAIR_ID_BOUND = 4 * PEER_TOPK * PEER_TOPK


def _pair_block_ids(tm):
    r = lax.broadcasted_iota(jnp.int32, (SUBLANES, tm), 0).astype(F32)
    seen = set()
    out = []
    for side, fixed, start in _PAIR_BLOCKS:
        ids = jnp.zeros((SUBLANES, tm), F32)
        for q in range(SUBLANES):
            i, j = (fixed, start + q) if side == "a" else (start + q, fixed)
            ok = (i + 1) * (j + 1) <= PEER_TOPK and (i, j) not in seen
            seen.add((i, j))
            ids = jnp.where(r == q, float(i * PEER_TOPK + j if ok else _PAIR_ID_BOUND + len(seen)), ids)
        out.append(ids)
    assert len({p for p in seen if (p[0] + 1) * (p[1] + 1) <= PEER_TOPK}) == 50
    return jnp.concatenate(out, axis=0)


def _pair_block_sums(av, bv):
    out = []
    for side, fixed, start in _PAIR_BLOCKS:
        if side == "a":
            out.append(av[fixed:fixed + 1, :] + bv[start:start + SUBLANES, :])
        else:
            out.append(av[start:start + SUBLANES, :] + bv[fixed:fixed + 1, :])
    return jnp.concatenate(out, axis=0)


def _take16(table, sel):
    out = jnp.zeros(sel.shape, table.dtype)
    for i in range(PEER_TOPK):
        out = jnp.where(sel == i, table[i:i + 1, :], out)
    return out


def _dot3(ah, al, bh, bl, dims):
    return (lax.dot_general(ah, bh, dims, preferred_element_type=F32)
            + lax.dot_general(al, bh, dims, preferred_element_type=F32)
            + lax.dot_general(ah, bl, dims, preferred_element_type=F32))


def _score_weight_kernel(wq_ref, key_ref, o_ref):
    kh, kl = _split_bf16(key_ref[0])
    wh, wl = _split_bf16(wq_ref[...])
    o_ref[...] = _dot3(kh, kl, wh, wl, NT_DIMS)


def score_weights(w_q, keys):
    d, nq = w_q.shape
    n_hp, nk, dk = keys.shape
    return pl.pallas_call(
        _score_weight_kernel,
        grid=(n_hp,),
        in_specs=[pl.BlockSpec((d, dk), lambda i: (0, i)), pl.BlockSpec((1, nk, dk), lambda i: (i, 0, 0))],
        out_specs=pl.BlockSpec((nk, d), lambda i: (i, 0)),
        out_shape=jax.ShapeDtypeStruct((n_hp * nk, d), F32),
        compiler_params=pltpu.CompilerParams(dimension_semantics=("parallel",)),
        name="score_weights",
    )(w_q, keys)


def _peer_route_kernel(h_ref, nw_ref, wsh_ref, wsl_ref, xn_ref, exp_ref, gate_ref, qt_scr, et_scr, gt_scr):
    xn = _rms(h_ref[...], nw_ref[...])
    for r in range(SUBLANES):
        xn_ref[:, r, :] = xn[:, 128 * r:128 * (r + 1)]
    xh, xl = _split_bf16(xn)
    qt_scr[...] = _dot3(wsh_ref[...], wsl_ref[...], xh, xl, NT_DIMS)
    tm = h_ref.shape[0]
    key_ids = lax.broadcasted_iota(jnp.int32, (PEER_NKEYS, tm), 0).astype(F32)
    pair_ids = _pair_block_ids(tm)
    pair_ok = pair_ids < float(_PAIR_ID_BOUND)

    def head(hd, tie, exact):
        sa = qt_scr[pl.ds(pl.multiple_of(hd * 256, 256), 128), :]
        sb = qt_scr[pl.ds(pl.multiple_of(hd * 256 + 128, 128), 128), :]
        if exact:
            av, ai = _top16_rows(sa, key_ids, PEER_NKEYS)
            bv, bi = _top16_rows(sb, key_ids, PEER_NKEYS)
        else:
            av, ai, ta = _top16_network(sa)
            bv, bi, tb_ = _top16_network(sb)
            tie = jnp.maximum(tie, jnp.maximum(ta, tb_))
        cand = jnp.where(pair_ok, _pair_block_sums(av, bv), -jnp.inf)
        cv, flat = _top16_rows(cand, pair_ids, _PAIR_ID_BOUND)
        flat = flat.astype(jnp.int32)
        e_a = _take16(ai, flat >> 4)
        e_b = _take16(bi, flat & (PEER_TOPK - 1))
        ex = jnp.exp(cv - cv[0:1, :])
        rows = pl.ds(pl.multiple_of(hd * PEER_TOPK, PEER_TOPK), PEER_TOPK)
        et_scr[rows, :] = (e_a * float(PEER_NKEYS) + e_b) * float(PEER_ROW_WORDS)
        gt_scr[rows, :] = ex / jnp.sum(ex, axis=0, keepdims=True)
        return tie

    def head_pair(i, tie):
        return head(2 * i + 1, head(2 * i, tie, exact=False), exact=False)

    tie = lax.fori_loop(0, PEER_HEADS // 2, head_pair, jnp.zeros((SUBLANES, tm), F32))

    @pl.when(jnp.max(tie) > 0.0)
    def _():
        lax.fori_loop(0, PEER_HEADS, functools.partial(head, exact=True), jnp.zeros((SUBLANES, tm), F32))

    exp_ref[...] = et_scr[...].T.astype(jnp.int32)
    gate_ref[...] = gt_scr[...].T


def peer_route(h, nw, w_q, keys, tm):
    t, d = h.shape
    wsh, wsl = _split_bf16(score_weights(w_q, keys))
    nq = wsh.shape[0]
    return pl.pallas_call(
        _peer_route_kernel,
        grid=(t // tm,),
        in_specs=[
            pl.BlockSpec((tm, d), lambda i: (i, 0)),
            pl.BlockSpec((1, d), lambda i: (0, 0)),
            pl.BlockSpec((nq, d), lambda i: (0, 0)),
            pl.BlockSpec((nq, d), lambda i: (0, 0)),
        ],
        out_specs=[
            pl.BlockSpec((tm, SUBLANES, d // SUBLANES), lambda i: (i, 0, 0)),
            pl.BlockSpec((tm, PEER_HK), lambda i: (i, 0)),
            pl.BlockSpec((tm, PEER_HK), lambda i: (i, 0)),
        ],
        out_shape=[jax.ShapeDtypeStruct((t, SUBLANES, d // SUBLANES), F32),
                   jax.ShapeDtypeStruct((t, PEER_HK), jnp.int32), jax.ShapeDtypeStruct((t, PEER_HK), F32)],
        scratch_shapes=[pltpu.VMEM((nq, tm), F32), pltpu.VMEM((PEER_HK, tm), F32), pltpu.VMEM((PEER_HK, tm), F32)],
        compiler_params=pltpu.CompilerParams(dimension_semantics=("parallel",), vmem_limit_bytes=VMEM_LIMIT),
        name="peer_route",
    )(h, nw, wsh, wsl)


PEER_GROUP = 16
PEER_ROW_WORDS = 4


def _gather_group(idx_ref, tab_ref, g, stage_ref):
    rows = [idx_ref.at[g * PEER_GROUP + j] for j in range(PEER_GROUP)]
    for k in range(PEER_HK):
        for j in range(PEER_GROUP):
            off = pl.multiple_of(rows[j][k], PEER_ROW_WORDS)
            stage_ref[j, pl.ds(PEER_ROW_WORDS * k, PEER_ROW_WORDS), :] = tab_ref[pl.ds(off, PEER_ROW_WORDS), :]


def _gather_compute_pipeline(n_groups, idx_ref, tab_ref, consume, stage_a, stage_b):
    def compute(g, stage_ref):
        for j in range(PEER_GROUP):
            consume(g * PEER_GROUP + j, stage_ref.at[j])

    _gather_group(idx_ref, tab_ref, 0, stage_a)

    def body(i, carry):
        compute(2 * i, stage_a)
        _gather_group(idx_ref, tab_ref, 2 * i + 1, stage_b)
        compute(2 * i + 1, stage_b)
        _gather_group(idx_ref, tab_ref, 2 * i + 2, stage_a)
        return carry

    lax.fori_loop(0, n_groups // 2 - 1, body, 0)
    _gather_group(idx_ref, tab_ref, n_groups - 1, stage_b)
    compute(n_groups - 2, stage_a)
    compute(n_groups - 1, stage_b)


def _diag_mask():
    row = lax.broadcasted_iota(jnp.int32, (SUBLANES, SUBLANES * PEER_HK), 0)
    lane = lax.broadcasted_iota(jnp.int32, (SUBLANES, SUBLANES * PEER_HK), 1)
    return (lane & (SUBLANES - 1)) == row


def _peer_hidden_kernel(idx_ref, x_ref, g_ref, sel_ref, tab_ref, o_ref, stage_a, stage_b, part_scr):
    tb = x_ref.shape[0]
    diag = _diag_mask()

    def consume(t, rows_ref):
        u = pltpu.bitcast(rows_ref[...], BF16)
        xh, xl = _split_bf16(x_ref[t])
        x16 = jnp.concatenate([xh, xl], axis=0)
        out = lax.dot_general(x16, u, NT_DIMS, preferred_element_type=F32)
        o8 = out[0:SUBLANES] + out[SUBLANES:2 * SUBLANES]
        part_scr[pl.ds(t, 1), :] = jnp.sum(jnp.where(diag, o8, 0.0), axis=0, keepdims=True)

    _gather_compute_pipeline(tb // PEER_GROUP, idx_ref, tab_ref, consume, stage_a, stage_b)
    ph, plo = _split_bf16(part_scr[...])
    sel = sel_ref[...]
    hid = jnp.dot(ph, sel, preferred_element_type=F32) + jnp.dot(plo, sel, preferred_element_type=F32)
    o_ref[...] = g_ref[...] * _gelu(hid)


def peer_hidden(idx, x8, gates, sel, tab, tb):
    t = idx.shape[0]
    return pl.pallas_call(
        _peer_hidden_kernel,
        grid=(t // tb,),
        in_specs=[
            pl.BlockSpec((tb, PEER_HK), lambda i: (i, 0), memory_space=pltpu.SMEM),
            pl.BlockSpec((tb, SUBLANES, 128), lambda i: (i, 0, 0)),
            pl.BlockSpec((tb, PEER_HK), lambda i: (i, 0)),
            pl.BlockSpec((SUBLANES * PEER_HK, PEER_HK), lambda i: (0, 0)),
            pl.BlockSpec((PEER_ROW_WORDS * PEER_EXPERTS, 128), lambda i: (0, 0), pipeline_mode=pl.Buffered(1)),
        ],
        out_specs=pl.BlockSpec((tb, PEER_HK), lambda i: (i, 0)),
        out_shape=jax.ShapeDtypeStruct((t, PEER_HK), F32),
        scratch_shapes=[pltpu.VMEM((PEER_GROUP, PEER_ROW_WORDS * PEER_HK, 128), jnp.int32),
                        pltpu.VMEM((PEER_GROUP, PEER_ROW_WORDS * PEER_HK, 128), jnp.int32),
                        pltpu.VMEM((tb, SUBLANES * PEER_HK), F32)],
        compiler_params=pltpu.CompilerParams(dimension_semantics=("arbitrary",), vmem_limit_bytes=VMEM_LIMIT_BIG),
        name="peer_hidden",
    )(idx, x8, gates, sel, tab)


def _peer_out_kernel(idx_ref, w_ref, exp_ref, tab_ref, o_ref, stage_a, stage_b, wexp_scr):
    tb = w_ref.shape[0]
    diag = _diag_mask()
    wh, wl = _split_bf16(w_ref[...])
    ex = exp_ref[...]
    wexp_scr[...] = jnp.dot(wh, ex, preferred_element_type=F32) + jnp.dot(wl, ex, preferred_element_type=F32)

    def consume(t, rows_ref):
        v = pltpu.bitcast(rows_ref[...], BF16)
        w8 = jnp.where(diag, jnp.broadcast_to(wexp_scr[pl.ds(t, 1), :], (SUBLANES, SUBLANES * PEER_HK)), 0.0)
        w8h, w8l = _split_bf16(w8)
        w16 = jnp.concatenate([w8h, w8l], axis=0)
        out = jnp.dot(w16, v, preferred_element_type=F32)
        o_ref[t] = out[0:SUBLANES] + out[SUBLANES:2 * SUBLANES]

    _gather_compute_pipeline(tb // PEER_GROUP, idx_ref, tab_ref, consume, stage_a, stage_b)


def peer_out(idx, w, expand, tab, tb):
    t = idx.shape[0]
    return pl.pallas_call(
        _peer_out_kernel,
        grid=(t // tb,),
        in_specs=[
            pl.BlockSpec((tb, PEER_HK), lambda i: (i, 0), memory_space=pltpu.SMEM),
            pl.BlockSpec((tb, PEER_HK), lambda i: (i, 0)),
            pl.BlockSpec((PEER_HK, SUBLANES * PEER_HK), lambda i: (0, 0)),
            pl.BlockSpec((PEER_ROW_WORDS * PEER_EXPERTS, 128), lambda i: (0, 0), pipeline_mode=pl.Buffered(1)),
        ],
        out_specs=pl.BlockSpec((tb, SUBLANES, 128), lambda i: (i, 0, 0)),
        out_shape=jax.ShapeDtypeStruct((t, SUBLANES, 128), F32),
        scratch_shapes=[pltpu.VMEM((PEER_GROUP, PEER_ROW_WORDS * PEER_HK, 128), jnp.int32),
                        pltpu.VMEM((PEER_GROUP, PEER_ROW_WORDS * PEER_HK, 128), jnp.int32),
                        pltpu.VMEM((tb, SUBLANES * PEER_HK), F32)],
        compiler_params=pltpu.CompilerParams(dimension_semantics=("arbitrary",), vmem_limit_bytes=VMEM_LIMIT_BIG),
        name="peer_out",
    )(idx, w, expand, tab)


def _ple_kernel(h_ref, e8_ref, p_ref, nw_ref, wg_ref, wp_ref, fw_ref, o_ref, *, final_norm):
    h = h_ref[...] + jnp.concatenate([e8_ref[:, r, :] for r in range(SUBLANES)], axis=1)
    gate = _sigmoid(jnp.dot(_rms(h, nw_ref[...]).astype(BF16), wg_ref[...], preferred_element_type=F32))
    out = h + jnp.dot(p_ref[...].astype(BF16), wp_ref[...], preferred_element_type=F32) * gate
    if final_norm:
        out = _rms(out, fw_ref[...])
    o_ref[...] = out


def ple(h, e8, p, layer, nw, wg, wp, fw, tm, final_norm):
    t, d = h.shape
    pd = p.shape[1]
    first = layer * (t // tm)
    return pl.pallas_call(
        functools.partial(_ple_kernel, final_norm=final_norm),
        grid=(t // tm,),
        in_specs=[
            pl.BlockSpec((tm, d), lambda i: (i, 0)),
            pl.BlockSpec((tm, SUBLANES, d // SUBLANES), lambda i: (i, 0, 0)),
            pl.BlockSpec((tm, pd), lambda i: (first + i, 0)),
            pl.BlockSpec((1, d), lambda i: (0, 0)),
            pl.BlockSpec((d, d), lambda i: (0, 0)),
            pl.BlockSpec((pd, d), lambda i: (0, 0)),
            pl.BlockSpec((1, d), lambda i: (0, 0)),
        ],
        out_specs=pl.BlockSpec((tm, d), lambda i: (i, 0)),
        out_shape=jax.ShapeDtypeStruct((t, d), F32),
        compiler_params=pltpu.CompilerParams(dimension_semantics=("parallel",), vmem_limit_bytes=VMEM_LIMIT),
        name="ple",
    )(h, e8, p, nw, wg, wp, fw)


def _pack_table_kernel(t_ref, o_ref):
    x = t_ref[...]
    te = x.shape[0]
    for s in range(PEER_ROW_WORDS):
        lo = pltpu.bitcast(x[:, 256 * s:256 * s + 128].astype(BF16).astype(F32), jnp.int32)
        hi = pltpu.bitcast(x[:, 256 * s + 128:256 * s + 256].astype(BF16).astype(F32), jnp.int32)
        o_ref[pl.ds(s, te, stride=PEER_ROW_WORDS), :] = (hi & jnp.int32(-65536)) | lax.shift_right_logical(lo, 16)


def _pack_table(tabs, layer=0):
    e, d = tabs.shape[-2:]
    tabs = tabs.reshape(-1, d)
    te = min(512, e)
    first = layer * (e // te)
    return pl.pallas_call(
        _pack_table_kernel,
        grid=(e // te,),
        in_specs=[pl.BlockSpec((te, d), lambda i: (first + i, 0))],
        out_specs=pl.BlockSpec((PEER_ROW_WORDS * te, 128), lambda i: (i, 0)),
        out_shape=jax.ShapeDtypeStruct((PEER_ROW_WORDS * e, 128), jnp.int32),
        compiler_params=pltpu.CompilerParams(dimension_semantics=("parallel",), vmem_limit_bytes=VMEM_LIMIT),
        name="pack_table",
    )(tabs)


def _even_in_perm():
    off = np.cumsum([0, 256, 256, 512, 512, 16, 256, 256, 512, 512])
    qa, ka, va, ga, ra, qb, kb, vb, gb = [np.arange(off[i], off[i + 1]) for i in range(9)]
    half = RET_DK // 2
    rot = np.concatenate([np.concatenate([np.arange(h * RET_DK, h * RET_DK + half) for h in range(RET_HEADS)]),
                          np.concatenate([np.arange(h * RET_DK + half, (h + 1) * RET_DK) for h in range(RET_HEADS)])])
    return np.concatenate([qa, ka, qb[rot], kb[rot], va, ga, vb, gb, ra])


def _retention_tables():
    lg = np.log(1.0 - 2.0 ** (-5.0 - np.arange(RET_HEADS, dtype=np.float64)))
    idx = np.arange(CHUNK, dtype=np.float64)
    diff = idx[:, None] - idx[None, :]
    dmask = np.where(diff >= 0, np.exp(lg[:, None, None] * np.maximum(diff, 0.0)), 0.0)
    qdec = np.exp(lg[:, None] * (idx + 1.0))
    kdec = np.exp(lg[:, None] * (CHUNK - 1.0 - idx))
    cdec = np.exp(lg * CHUNK)
    per_col = lambda t: np.repeat(t.T, RET_DV, axis=1)
    lanes = np.arange(256)
    rows_head = np.repeat(np.arange(RET_HEADS), RET_DV)[:, None]
    half = RET_DK // 2
    hmask_g = (lanes[None, :] // GLA_DK) == rows_head
    hmask_r = ((lanes[None, :] % 128) // half == rows_head) & ((lanes[None, :] % 128) < RET_HEADS * half)
    return (jnp.asarray(dmask, F32), jnp.asarray(per_col(kdec), F32), jnp.asarray(per_col(qdec), F32),
            jnp.asarray(np.broadcast_to(np.repeat(cdec, RET_DV)[:, None], (RET_HEADS * RET_DV, 256)), F32),
            jnp.asarray(hmask_g, F32), jnp.asarray(hmask_r, F32))


def _peer_layer(h, nw, w_q, sub_keys, u_tabs, v_tabs, layer, tm_route, tb):
    keys = sub_keys.reshape(2 * PEER_HEADS, PEER_NKEYS, -1)
    xn8, experts, gates = peer_route(h, nw, w_q, keys, tm_route)
    kk = np.arange(SUBLANES * PEER_HK) // SUBLANES
    sel = jnp.asarray(kk[:, None] == np.arange(PEER_HK)[None, :], BF16)
    w = peer_hidden(experts, xn8, gates, sel, _pack_table(u_tabs, layer), tb)
    return peer_out(experts, w, sel.T, _pack_table(v_tabs, layer), tb)


def _row(v):
    return v.reshape(1, -1).astype(F32)


def _layer0_mixer(h, w, bsz, seq):
    t = h.shape[0]
    w_in = w["ev_w_in"][0]
    w_in0 = jnp.pad(w_in[:, _even_in_perm()], ((0, 0), (0, EVEN_COLS - w_in.shape[1]))).astype(BF16)
    proj = norm_matmul(h, _row(w["norm_mix_w"][0]), w_in0, min(256, t))
    half = RET_DK // 2
    freqs = ROPE_BASE ** (-np.arange(half, dtype=np.float32) / half)
    dmask, kdec, qdec, cdec, hmask_g, hmask_r = _retention_tables()
    consts = (
        jnp.asarray(np.tile(freqs, RET_HEADS)[None, :], F32),
        jnp.pad(w["ev_gla_w_up"][0], ((0, 128 - GLA_GATE_RANK), (0, 0))).astype(BF16),
        _row(w["ev_gla_b_up"][0]), _row(w["ev_gla_norm_w"][0]), _row(w["ev_ret_norm_w"][0]),
        w["ev_w_out"][0].astype(BF16),
        jnp.asarray(np.tril(np.ones((CHUNK, CHUNK))), BF16),
        dmask, kdec, qdec, cdec, hmask_g, hmask_r,
    )
    pos = w["positions"].astype(F32).reshape(seq, 1)
    return even_mixer(h, proj, pos, consts, bsz, seq, min(256, seq))


def _layer1_mixer(h, w, bsz, seq):
    t = h.shape[0]
    u = norm_matmul(h, _row(w["norm_mix_w"][1]), w["od_w_in"][0].astype(BF16), min(512, t))
    pw_re, pw_im, bb_re, bb_im = s5_params(w["od_s5_a_re"][0], w["od_s5_a_im"][0], w["od_s5_log_dt"][0].reshape(-1, 1),
                                           w["od_s5_b_re"][0].transpose(0, 2, 1), w["od_s5_b_im"][0].transpose(0, 2, 1))
    eye = jnp.eye(S5_GROUPS, dtype=F32)
    blockdiag = lambda m: (m[:, :, None, :] * eye[:, None, :, None]).reshape(m.shape[0] * m.shape[1], -1)
    consts = (
        w["od_pool_w"][0].astype(BF16), _row(w["od_pool_scale"][0]),
        blockdiag(bb_re).astype(BF16), blockdiag(bb_im).astype(BF16),
        blockdiag(w["od_s5_c_re"][0]).T.astype(BF16), blockdiag(w["od_s5_c_im"][0]).T.astype(BF16),
        pw_re.reshape(SUBLANES, S5_STATE), pw_im.reshape(SUBLANES, S5_STATE),
        _row(w["od_s5_d"][0]), w["od_s5_w_glu"][0].astype(BF16), _row(w["od_s5_b_glu"][0]), w["od_w_out"][0].astype(BF16),
    )
    return odd_mixer(h, u, consts, bsz, seq, min(256, seq))


def kernel(x, p, positions, norm_mix_w, norm_ffn_w, norm_ple_w, final_norm_w, ev_w_in, ev_gla_w_up, ev_gla_b_up, ev_gla_norm_w, ev_ret_norm_w, ev_w_out, od_w_in, od_pool_w, od_pool_scale, od_s5_a_re, od_s5_a_im, od_s5_log_dt, od_s5_b_re, od_s5_b_im, od_s5_c_re, od_s5_c_im, od_s5_d, od_s5_w_glu, od_s5_b_glu, od_w_out, peer_w_q, peer_sub_keys, peer_u, peer_v, ple_w_proj, ple_w_gate):
    w = dict(locals())
    bsz, seq, d = x.shape
    t = bsz * seq
    tm_tok = min(512, t)
    tm_route = min(256, t)
    tb = min(512, t)
    h = x.reshape(t, d)
    for i, mixer in enumerate((_layer0_mixer, _layer1_mixer)):
        h = mixer(h, w, bsz, seq)
        e8 = _peer_layer(h, _row(norm_ffn_w[i]), peer_w_q[i], peer_sub_keys[i], peer_u, peer_v, i, tm_route, tb)
        h = ple(h, e8, p.reshape(-1, p.shape[-1]), i, _row(norm_ple_w[i]), ple_w_gate[i].astype(BF16),
                ple_w_proj[i].astype(BF16), _row(final_norm_w), tm_tok, i == 1)
    return h.reshape(bsz, seq, d)
```

```python
import functools
import math

import numpy as np
import jax
import jax.numpy as jnp
from jax import lax
from jax.experimental import pallas as pl
from jax.experimental.pallas import tpu as pltpu

F32 = jnp.float32
BF16 = jnp.bfloat16

D_MODEL = 1024
NORM_EPS = 1e-6
CHUNK = 64
GLA_HEADS = 4
GLA_DK = 64
GLA_DV = 128
GLA_GATE_RANK = 16
GLA_GATE_NORM = 16.0
RET_HEADS = 4
RET_DK = 64
RET_DV = 128
ROPE_BASE = 10000.0
POOL_WINDOWS = (2, 4, 8, 16)
POOL_WIDTH = 512
POOL_GROUP_WIDTH = 128
S5_H = 16
S5_P = 64
S5_GROUPS = 32
S5_WIDTH = 512
S5_STATE = S5_GROUPS * S5_P
PEER_HEADS = 8
PEER_NKEYS = 128
PEER_TOPK = 16
PEER_HK = PEER_HEADS * PEER_TOPK
PEER_EXPERTS = PEER_NKEYS * PEER_NKEYS
EVEN_COLS = 3200

VMEM_LIMIT_BIG = 56 * 1024 * 1024
VMEM_LIMIT = 48 * 1024 * 1024
SUBLANES = 8

NT_DIMS = (((1,), (1,)), ((), ()))


def _split_bf16(x):
    hi = x.astype(BF16)
    lo = (x - hi.astype(F32)).astype(BF16)
    return hi, lo


def _rms(x, w):
    return x * lax.rsqrt(jnp.mean(x * x, axis=-1, keepdims=True) + NORM_EPS) * w


def _sigmoid(x):
    return 1.0 / (1.0 + jnp.exp(-x))


def _gelu(x):
    return 0.5 * x * (1.0 + lax.erf(x * (2.0 ** -0.5)))


def _log_sigmoid(z):
    return jnp.minimum(z, 0.0) - jnp.log1p(jnp.exp(-jnp.abs(z)))


def _norm_mm_kernel(h_ref, nw_ref, w_ref, o_ref):
    xn = _rms(h_ref[...], nw_ref[...])
    o_ref[...] = jnp.dot(xn.astype(BF16), w_ref[...], preferred_element_type=F32)


def norm_matmul(h, nw, w, tm):
    t, d = h.shape
    n = w.shape[1]
    return pl.pallas_call(
        _norm_mm_kernel,
        grid=(t // tm,),
        in_specs=[
            pl.BlockSpec((tm, d), lambda i: (i, 0)),
            pl.BlockSpec((1, d), lambda i: (0, 0)),
            pl.BlockSpec((d, n), lambda i: (0, 0)),
        ],
        out_specs=pl.BlockSpec((tm, n), lambda i: (i, 0)),
        out_shape=jax.ShapeDtypeStruct((t, n), F32),
        compiler_params=pltpu.CompilerParams(dimension_semantics=("parallel",), vmem_limit_bytes=VMEM_LIMIT),
        name="norm_matmul",
    )(h, nw, w)


def _rotary_table_kernel(pos_ref, freq_ref, cos_ref, sin_ref):
    ang = pos_ref[...] * freq_ref[...]
    cos_ref[...] = jnp.cos(ang)
    sin_ref[...] = jnp.sin(ang)


def rotary_table(pos, freqs, ts):
    seq = pos.shape[0]
    n = freqs.shape[1]
    return pl.pallas_call(
        _rotary_table_kernel,
        grid=(seq // ts,),
        in_specs=[pl.BlockSpec((ts, 1), lambda i: (i, 0)), pl.BlockSpec((1, n), lambda i: (0, 0))],
        out_specs=[pl.BlockSpec((ts, n), lambda i: (i, 0)), pl.BlockSpec((ts, n), lambda i: (i, 0))],
        out_shape=[jax.ShapeDtypeStruct((seq, n), F32), jax.ShapeDtypeStruct((seq, n), F32)],
        compiler_params=pltpu.CompilerParams(dimension_semantics=("parallel",)),
        name="rotary_table",
    )(pos, freqs)


def _even_mixer_kernel(h_ref, qk_ref, va_ref, ga_ref, vb_ref, gb_ref, ra_ref, cos_ref, sin_ref,
                       wup_ref, bup_ref, gnw_ref, rnw_ref, wout_ref, tril_ref,
                       dmask_ref, kdec_ref, qdec_ref, cdec_ref, hmaskg_ref, hmaskr_ref,
                       o_ref, gstate, rstate, y_scr, qin_scr, dec_scr, kvg_scr, kvr_scr):
    tm = h_ref.shape[0]

    @pl.when(pl.program_id(1) == 0)
    def _():
        gstate[...] = jnp.zeros_like(gstate)
        rstate[...] = jnp.zeros_like(rstate)

    lane = lax.broadcasted_iota(jnp.int32, (1, 256), 1)
    gla_masks = [((lane >= GLA_DK * h) & (lane < GLA_DK * (h + 1))).astype(F32) for h in range(GLA_HEADS)]
    half = RET_DK // 2
    ret_masks = [(((lane >= half * h) & (lane < half * (h + 1)))
                  | ((lane >= 128 + half * h) & (lane < 128 + half * (h + 1)))).astype(F32)
                 for h in range(RET_HEADS)]
    ri = lax.broadcasted_iota(jnp.int32, (CHUNK, CHUNK), 0)
    ci = lax.broadcasted_iota(jnp.int32, (CHUNK, CHUNK), 1)
    causal = ri >= ci
    tril = tril_ref[...]
    wup = wup_ref[...]
    bup = bup_ref[...]

    n_chunks = tm // CHUNK
    causal4 = jnp.concatenate([causal] * GLA_HEADS, axis=0)
    dmask4 = jnp.concatenate([dmask_ref[h] for h in range(RET_HEADS)], axis=0)
    hmask_g = hmaskg_ref[...]
    hmask_r = hmaskr_ref[...]

    for c in range(n_chunks):
        rows = slice(c * CHUNK, (c + 1) * CHUNK)
        qk = qk_ref[rows, :]
        qa, ka, qb, kb = qk[:, 0:256], qk[:, 256:512], qk[:, 512:768], qk[:, 768:1024]

        z = jnp.dot(ra_ref[rows, :].astype(BF16), wup, preferred_element_type=F32) + bup
        la = _log_sigmoid(z) * (1.0 / GLA_GATE_NORM)
        lah, lal = _split_bf16(la)
        g = (jnp.dot(tril, lah, preferred_element_type=F32)
             + jnp.dot(tril, lal, preferred_element_type=F32))
        g_last = g[CHUNK - 1:CHUNK, :]
        ref = 0.5 * g_last
        qs = qa * (GLA_DK ** -0.5)
        qt = qs * jnp.exp(g - ref)
        kt = (ka * jnp.exp(ref - g)).astype(BF16)
        kd = ka * jnp.exp(g_last - g)
        qin_scr[rows, 0:256] = qs * jnp.exp(g)
        dec_scr[c] = jnp.exp(g_last)
        q4 = jnp.concatenate([(qt * gla_masks[h]) for h in range(GLA_HEADS)], axis=0).astype(BF16)
        s4 = jnp.where(causal4, lax.dot_general(q4, kt, NT_DIMS, preferred_element_type=F32), 0.0).astype(BF16)
        va = va_ref[rows, :]
        vab = va.astype(BF16)
        for h in range(GLA_HEADS):
            y_scr[rows, GLA_DV * h:GLA_DV * (h + 1)] = jnp.dot(
                s4[CHUNK * h:CHUNK * (h + 1), :], vab[:, GLA_DV * h:GLA_DV * (h + 1)], preferred_element_type=F32)
        kvg_scr[c] = jnp.dot(va.T.astype(BF16), kd.astype(BF16), preferred_element_type=F32) * hmask_g

        cs, sn = cos_ref[rows, :], sin_ref[rows, :]
        q1, q2 = qb[:, 0:128], qb[:, 128:256]
        k1, k2 = kb[:, 0:128], kb[:, 128:256]
        qr = jnp.concatenate([q1 * cs - q2 * sn, q2 * cs + q1 * sn], axis=1)
        kr = jnp.concatenate([k1 * cs - k2 * sn, k2 * cs + k1 * sn], axis=1) * (RET_DK ** -0.5)
        krb = kr.astype(BF16)
        qin_scr[rows, 256:512] = qr
        q4 = jnp.concatenate([(qr * ret_masks[h]) for h in range(RET_HEADS)], axis=0).astype(BF16)
        s4 = (lax.dot_general(q4, krb, NT_DIMS, preferred_element_type=F32) * dmask4).astype(BF16)
        vb = vb_ref[rows, :]
        vbb = vb.astype(BF16)
        for h in range(RET_HEADS):
            y_scr[rows, 512 + RET_DV * h:512 + RET_DV * (h + 1)] = jnp.dot(
                s4[CHUNK * h:CHUNK * (h + 1), :], vbb[:, RET_DV * h:RET_DV * (h + 1)], preferred_element_type=F32)
        kvr_scr[c] = jnp.dot((vb * kdec_ref[...]).T.astype(BF16), krb, preferred_element_type=F32) * hmask_r

    for c in range(n_chunks):
        rows = slice(c * CHUNK, (c + 1) * CHUNK)
        sg = gstate[...]
        y_scr[rows, 0:512] += lax.dot_general(qin_scr[rows, 0:256].astype(BF16), sg.astype(BF16), NT_DIMS,
                                              preferred_element_type=F32)
        gstate[...] = sg * dec_scr[c] + kvg_scr[c]
        sr = rstate[...]
        y_scr[rows, 512:1024] += lax.dot_general(qin_scr[rows, 256:512].astype(BF16), sr.astype(BF16), NT_DIMS,
                                                 preferred_element_type=F32) * qdec_ref[...]
        rstate[...] = sr * cdec_ref[...] + kvr_scr[c]

    pieces = []
    gnw = gnw_ref[...]
    for h in range(GLA_HEADS):
        o = y_scr[:, GLA_DV * h:GLA_DV * (h + 1)]
        gt = ga_ref[:, GLA_DV * h:GLA_DV * (h + 1)]
        pieces.append(_rms(o, gnw) * (gt * _sigmoid(gt)))
    for h in range(RET_HEADS):
        o = y_scr[:, 512 + RET_DV * h:512 + RET_DV * (h + 1)]
        gt = gb_ref[:, RET_DV * h:RET_DV * (h + 1)]
        mu = jnp.mean(o, axis=-1, keepdims=True)
        oc = o - mu
        var = jnp.mean(oc * oc, axis=-1, keepdims=True)
        nrm = oc * lax.rsqrt(var + NORM_EPS) * rnw_ref[:, RET_DV * h:RET_DV * (h + 1)]
        pieces.append(nrm * (gt * _sigmoid(gt)))
    y = jnp.concatenate(pieces, axis=1).astype(BF16)
    o_ref[...] = h_ref[...] + jnp.dot(y, wout_ref[...], preferred_element_type=F32)


def even_mixer(h, proj, pos, consts, bsz, seq, tm):
    nj = seq // tm
    tok = lambda b, j: (b * nj + j, 0)
    col = lambda cb: (lambda b, j: (b * nj + j, cb))
    full2 = lambda b, j: (0, 0)
    full3 = lambda b, j: (0, 0, 0)
    (freqs, wup, bup, gnw, rnw, wout, tril, dmask, kdec, qdec, cdec, hmask_g, hmask_r) = consts
    cos_t, sin_t = rotary_table(pos, freqs, min(512, seq))
    n_state = GLA_HEADS * GLA_DV
    return pl.pallas_call(
        _even_mixer_kernel,
        grid=(bsz, nj),
        in_specs=[
            pl.BlockSpec((tm, D_MODEL), tok),
            pl.BlockSpec((tm, 1024), col(0)),
            pl.BlockSpec((tm, 512), col(2)),
            pl.BlockSpec((tm, 512), col(3)),
            pl.BlockSpec((tm, 512), col(4)),
            pl.BlockSpec((tm, 512), col(5)),
            pl.BlockSpec((tm, 128), col(24)),
            pl.BlockSpec((tm, 128), lambda b, j: (j, 0)),
            pl.BlockSpec((tm, 128), lambda b, j: (j, 0)),
            pl.BlockSpec((128, 256), full2),
            pl.BlockSpec((1, 256), full2),
            pl.BlockSpec((1, 128), full2),
            pl.BlockSpec((1, 512), full2),
            pl.BlockSpec((1024, D_MODEL), full2),
            pl.BlockSpec((CHUNK, CHUNK), full2),
            pl.BlockSpec((RET_HEADS, CHUNK, CHUNK), full3),
            pl.BlockSpec((CHUNK, n_state), full2),
            pl.BlockSpec((CHUNK, n_state), full2),
            pl.BlockSpec((n_state, 256), full2),
            pl.BlockSpec((n_state, 256), full2),
            pl.BlockSpec((n_state, 256), full2),
        ],
        out_specs=pl.BlockSpec((tm, D_MODEL), tok),
        out_shape=jax.ShapeDtypeStruct(h.shape, F32),
        scratch_shapes=[
            pltpu.VMEM((n_state, 256), F32),
            pltpu.VMEM((n_state, 256), F32),
            pltpu.VMEM((tm, 1024), F32),
            pltpu.VMEM((tm, 512), F32),
            pltpu.VMEM((tm // CHUNK, 1, 256), F32),
            pltpu.VMEM((tm // CHUNK, n_state, 256), F32),
            pltpu.VMEM((tm // CHUNK, n_state, 256), F32),
        ],
        compiler_params=pltpu.CompilerParams(dimension_semantics=("arbitrary", "arbitrary"),
                                             vmem_limit_bytes=VMEM_LIMIT),
        name="even_mixer",
    )(h, proj, proj, proj, proj, proj, proj, cos_t, sin_t, wup, bup, gnw, rnw, wout, tril, dmask, kdec, qdec, cdec,
      hmask_g, hmask_r)


def _s5_param_kernel(are_ref, aim_ref, ldt_ref, bre_ref, bim_ref,
                     pw_re_ref, pw_im_ref, bbre_ref, bbim_ref):
    a_re = are_ref[...]
    a_im = aim_ref[...]
    dt = jnp.exp(ldt_ref[...])
    for r in range(SUBLANES):
        mag = jnp.exp(a_re * dt * (r + 1.0))
        pw_re_ref[r] = mag * jnp.cos(a_im * dt * (r + 1.0))
        pw_im_ref[r] = mag * jnp.sin(a_im * dt * (r + 1.0))
    mag = jnp.exp(a_re * dt)
    abar_re, abar_im = mag * jnp.cos(a_im * dt), mag * jnp.sin(a_im * dt)
    den = a_re * a_re + a_im * a_im
    nr, ni = abar_re - 1.0, abar_im
    coef_re = (nr * a_re + ni * a_im) / den
    coef_im = (ni * a_re - nr * a_im) / den
    b_re = bre_ref[...]
    b_im = bim_ref[...]
    c_re = jnp.concatenate([coef_re] * S5_H, axis=1)
    c_im = jnp.concatenate([coef_im] * S5_H, axis=1)
    bbre_ref[...] = c_re * b_re - c_im * b_im
    bbim_ref[...] = c_re * b_im + c_im * b_re


def s5_params(a_re, a_im, log_dt, b_re_t, b_im_t):
    g, p = a_re.shape
    hh = b_re_t.shape[1]
    pw_re, pw_im, bb_re, bb_im = pl.pallas_call(
        _s5_param_kernel,
        out_shape=[jax.ShapeDtypeStruct((SUBLANES, g, p), F32), jax.ShapeDtypeStruct((SUBLANES, g, p), F32),
                   jax.ShapeDtypeStruct((g, hh * p), F32), jax.ShapeDtypeStruct((g, hh * p), F32)],
        name="s5_params",
    )(a_re, a_im, log_dt, b_re_t.reshape(g, hh * p), b_im_t.reshape(g, hh * p))
    return pw_re, pw_im, bb_re.reshape(g, hh, p), bb_im.reshape(g, hh, p)


def _odd_mixer_kernel(h_ref, u_ref, poolw_ref, pscale_ref, wbre_ref, wbim_ref, wcre_ref, wcim_ref,
                      pwre_ref, pwim_ref, dskip_ref, wglu_ref, bglu_ref, wout_ref,
                      o_ref, tail_scr, car_re, car_im, xre_scr, xim_scr):
    tm = h_ref.shape[0]
    j = pl.program_id(1)
    halo = POOL_WINDOWS[-1]

    @pl.when(j == 0)
    def _():
        tail_scr[...] = jnp.zeros_like(tail_scr)
        car_re[...] = jnp.zeros_like(car_re)
        car_im[...] = jnp.zeros_like(car_im)

    uc = u_ref[:, 0:POOL_WIDTH]
    ud = u_ref[:, POOL_WIDTH:POOL_WIDTH + S5_WIDTH]

    ext = jnp.concatenate([tail_scr[...], uc], axis=0)
    tail_scr[...] = uc[tm - halo:tm, :]
    pos = (j * tm + lax.broadcasted_iota(jnp.int32, (tm, 1), 0)).astype(F32)
    mixed = []
    for gi, win in enumerate(POOL_WINDOWS):
        a = ext[:, POOL_GROUP_WIDTH * gi:POOL_GROUP_WIDTH * (gi + 1)]
        n = tm + halo
        step = 1
        end = 0
        while step < win:
            a = a[step:n, :] + a[0:n - step, :]
            n -= step
            end += step
            step *= 2
        wsum = a[halo - end:halo - end + tm, :]
        cnt = jnp.minimum(pos + 1.0, float(win))
        pooled = wsum / cnt - uc[:, POOL_GROUP_WIDTH * gi:POOL_GROUP_WIDTH * (gi + 1)]
        mixed.append(jnp.dot(pooled.astype(BF16), poolw_ref[gi], preferred_element_type=F32))
    y_c = jnp.concatenate(mixed, axis=1) * pscale_ref[...]

    udb = ud.astype(BF16)
    hw, hs = S5_WIDTH // 2, S5_STATE // 2
    for hf in range(2):
        cols = slice(hs * hf, hs * (hf + 1))
        uh = udb[:, hw * hf:hw * (hf + 1)]
        xre_scr[:, cols] = jnp.dot(uh, wbre_ref[hw * hf:hw * (hf + 1), cols], preferred_element_type=F32)
        xim_scr[:, cols] = jnp.dot(uh, wbim_ref[hw * hf:hw * (hf + 1), cols], preferred_element_type=F32)
    rowi = lax.broadcasted_iota(jnp.int32, (SUBLANES, S5_STATE), 0)
    pw_re = pwre_ref[...]
    pw_im = pwim_ref[...]
    step_pw = [(jnp.where(rowi >= d, pw_re[d - 1:d, :], 0.0), jnp.where(rowi >= d, pw_im[d - 1:d, :], 0.0))
               for d in (1, 2, 4)]

    def slab(s, carry):
        cr, ci = carry
        rows = pl.ds(pl.multiple_of(s * SUBLANES, SUBLANES), SUBLANES)
        xr = xre_scr[rows, :]
        xi = xim_scr[rows, :]
        for dsh, (pr, pi) in zip((1, 2, 4), step_pw):
            sr = pltpu.roll(xr, dsh, axis=0)
            si = pltpu.roll(xi, dsh, axis=0)
            xr, xi = xr + (pr * sr - pi * si), xi + (pr * si + pi * sr)
        xr, xi = xr + (pw_re * cr - pw_im * ci), xi + (pw_re * ci + pw_im * cr)
        xre_scr[rows, :] = xr
        xim_scr[rows, :] = xi
        return xr[SUBLANES - 1:SUBLANES, :], xi[SUBLANES - 1:SUBLANES, :]

    cr, ci = lax.fori_loop(0, tm // SUBLANES, slab, (car_re[...], car_im[...]))
    car_re[...] = cr
    car_im[...] = ci

    yh = []
    for hf in range(2):
        rws = slice(hs * hf, hs * (hf + 1))
        cls = slice(hw * hf, hw * (hf + 1))
        yh.append(jnp.dot(xre_scr[:, rws].astype(BF16), wcre_ref[rws, cls], preferred_element_type=F32)
                  - jnp.dot(xim_scr[:, rws].astype(BF16), wcim_ref[rws, cls], preferred_element_type=F32))
    y = jnp.concatenate(yh, axis=1) + dskip_ref[...] * ud
    z = _gelu(y)
    y_d = z * _sigmoid(jnp.dot(z.astype(BF16), wglu_ref[...], preferred_element_type=F32) + bglu_ref[...])

    ycat = jnp.concatenate([y_c, y_d], axis=1).astype(BF16)
    o_ref[...] = h_ref[...] + jnp.dot(ycat, wout_ref[...], preferred_element_type=F32)


def odd_mixer(h, u, consts, bsz, seq, tm):
    nj = seq // tm
    tok = lambda b, j: (b * nj + j, 0)
    full2 = lambda b, j: (0, 0)
    full3 = lambda b, j: (0, 0, 0)
    (poolw, pscale, wbre, wbim, wcre, wcim, pwre, pwim, dskip, wglu, bglu, wout) = consts
    return pl.pallas_call(
        _odd_mixer_kernel,
        grid=(bsz, nj),
        in_specs=[
            pl.BlockSpec((tm, D_MODEL), tok),
            pl.BlockSpec((tm, 1024), tok),
            pl.BlockSpec((4, 128, 128), full3),
            pl.BlockSpec((1, POOL_WIDTH), full2),
            pl.BlockSpec((S5_WIDTH, S5_STATE), full2),
            pl.BlockSpec((S5_WIDTH, S5_STATE), full2),
            pl.BlockSpec((S5_STATE, S5_WIDTH), full2),
            pl.BlockSpec((S5_STATE, S5_WIDTH), full2),
            pl.BlockSpec((SUBLANES, S5_STATE), full2),
            pl.BlockSpec((SUBLANES, S5_STATE), full2),
            pl.BlockSpec((1, S5_WIDTH), full2),
            pl.BlockSpec((S5_WIDTH, S5_WIDTH), full2),
            pl.BlockSpec((1, S5_WIDTH), full2),
            pl.BlockSpec((1024, D_MODEL), full2),
        ],
        out_specs=pl.BlockSpec((tm, D_MODEL), tok),
        out_shape=jax.ShapeDtypeStruct(h.shape, F32),
        scratch_shapes=[
            pltpu.VMEM((POOL_WINDOWS[-1], POOL_WIDTH), F32),
            pltpu.VMEM((1, S5_STATE), F32),
            pltpu.VMEM((1, S5_STATE), F32),
            pltpu.VMEM((tm, S5_STATE), F32),
            pltpu.VMEM((tm, S5_STATE), F32),
        ],
        compiler_params=pltpu.CompilerParams(dimension_semantics=("arbitrary", "arbitrary"),
                                             vmem_limit_bytes=VMEM_LIMIT),
        name="odd_mixer",
    )(h, u, poolw, pscale, wbre, wbim, wcre, wcim, pwre, pwim, dskip, wglu, bglu, wout)


def _top16_rows(s, ids, id_bound):
    vals, idxs = [], []
    for _ in range(PEER_TOPK):
        m = jnp.max(s, axis=0, keepdims=True)
        am = jnp.min(jnp.where(s == m, ids, float(id_bound)), axis=0, keepdims=True)
        vals.append(m)
        idxs.append(am)
        s = jnp.where(ids == am, -jnp.inf, s)
    return jnp.concatenate(vals, axis=0), jnp.concatenate(idxs, axis=0)


def _batcher_pairs(lo, hi):
    def merge(lo, hi, r):
        step = 2 * r
        if step < hi - lo:
            yield from merge(lo, hi, step)
            yield from merge(lo + r, hi, step)
            yield from ((i, i + r) for i in range(lo + r, hi - r, step))
        else:
            yield (lo, lo + r)

    if hi > lo:
        mid = lo + (hi - lo) // 2
        yield from _batcher_pairs(lo, mid)
        yield from _batcher_pairs(mid + 1, hi)
        yield from merge(lo, hi, 1)


_SORT16 = tuple(_batcher_pairs(0, PEER_TOPK - 1))
_BITONIC16 = tuple((i, i + d) for d in (8, 4, 2, 1) for i in range(PEER_TOPK) if not i & d)


def _top16_network(s):
    n = PEER_TOPK
    v = [s[SUBLANES * i:SUBLANES * (i + 1), :] for i in range(n)]
    sub = lax.broadcasted_iota(jnp.int32, (SUBLANES, s.shape[1]), 0).astype(F32)
    k = [sub + float(SUBLANES * i) for i in range(n)]

    def exchange(i, j):
        swap = v[j] > v[i]
        v[i], v[j] = jnp.where(swap, v[j], v[i]), jnp.where(swap, v[i], v[j])
        k[i], k[j] = jnp.where(swap, k[j], k[i]), jnp.where(swap, k[i], k[j])

    for i, j in _SORT16:
        exchange(i, j)
    for shift in (4, 2, 1):
        bv = [pltpu.roll(x, shift, axis=0) for x in v]
        bk = [pltpu.roll(x, shift, axis=0) for x in k]
        for i in range(n):
            take = bv[n - 1 - i] > v[i]
            v[i] = jnp.where(take, bv[n - 1 - i], v[i])
            k[i] = jnp.where(take, bk[n - 1 - i], k[i])
        for i, j in _BITONIC16:
            exchange(i, j)
    tie = jnp.zeros_like(v[0])
    for i in range(n - 1):
        tie = jnp.where(v[i] == v[i + 1], 1.0, tie)
    cnt = jnp.zeros_like(v[0])
    for i in range(n):
        cnt = cnt + jnp.where(s[SUBLANES * i:SUBLANES * (i + 1), :] >= v[n - 1], 1.0, 0.0)
    for shift in (4, 2, 1):
        cnt = cnt + pltpu.roll(cnt, shift, axis=0)
    tie = jnp.where(cnt > float(n), 1.0, tie)
    return (jnp.concatenate([x[0:1, :] for x in v], axis=0), jnp.concatenate([x[0:1, :] for x in k], axis=0), tie)


_PAIR_BLOCKS = (("b", 0, 0), ("b", 1, 0), ("b", 2, 0), ("b", 3, 0), ("b", 4, 0),
                ("a", 0, 8), ("a", 0, 0), ("a", 1, 0), ("b", 0, 8))
_PAIR_ID_BOUND = 4 * PEER_TOPK * PEER_TOPK


def _pair_block_ids(tm):
    r = lax.broadcasted_iota(jnp.int32, (SUBLANES, tm), 0).astype(F32)
    seen = set()
    out = []
    for side, fixed, start in _PAIR_BLOCKS:
        ids = jnp.zeros((SUBLANES, tm), F32)
        for q in range(SUBLANES):
            i, j = (fixed, start + q) if side == "a" else (start + q, fixed)
            ok = (i + 1) * (j + 1) <= PEER_TOPK and (i, j) not in seen
            seen.add((i, j))
            ids = jnp.where(r == q, float(i * PEER_TOPK + j if ok else _PAIR_ID_BOUND + len(seen)), ids)
        out.append(ids)
    assert len({p for p in seen if (p[0] + 1) * (p[1] + 1) <= PEER_TOPK}) == 50
    return jnp.concatenate(out, axis=0)


def _pair_block_sums(av, bv):
    out = []
    for side, fixed, start in _PAIR_BLOCKS:
        if side == "a":
            out.append(av[fixed:fixed + 1, :] + bv[start:start + SUBLANES, :])
        else:
            out.append(av[start:start + SUBLANES, :] + bv[fixed:fixed + 1, :])
    return jnp.concatenate(out, axis=0)


def _take16(table, sel):
    out = jnp.zeros(sel.shape, table.dtype)
    for i in range(PEER_TOPK):
        out = jnp.where(sel == i, table[i:i + 1, :], out)
    return out


def _dot3(ah, al, bh, bl, dims):
    return (lax.dot_general(ah, bh, dims, preferred_element_type=F32)
            + lax.dot_general(al, bh, dims, preferred_element_type=F32)
            + lax.dot_general(ah, bl, dims, preferred_element_type=F32))


def _score_weight_kernel(wq_ref, key_ref, o_ref):
    kh, kl = _split_bf16(key_ref[0])
    wh, wl = _split_bf16(wq_ref[...])
    o_ref[...] = _dot3(kh, kl, wh, wl, NT_DIMS)


def score_weights(w_q, keys):
    d, nq = w_q.shape
    n_hp, nk, dk = keys.shape
    return pl.pallas_call(
        _score_weight_kernel,
        grid=(n_hp,),
        in_specs=[pl.BlockSpec((d, dk), lambda i: (0, i)), pl.BlockSpec((1, nk, dk), lambda i: (i, 0, 0))],
        out_specs=pl.BlockSpec((nk, d), lambda i: (i, 0)),
        out_shape=jax.ShapeDtypeStruct((n_hp * nk, d), F32),
        compiler_params=pltpu.CompilerParams(dimension_semantics=("parallel",)),
        name="score_weights",
    )(w_q, keys)


def _peer_route_kernel(h_ref, nw_ref, wsh_ref, wsl_ref, xn_ref, exp_ref, gate_ref, qt_scr, et_scr, gt_scr):
    xn = _rms(h_ref[...], nw_ref[...])
    for r in range(SUBLANES):
        xn_ref[:, r, :] = xn[:, 128 * r:128 * (r + 1)]
    xh, xl = _split_bf16(xn)
    qt_scr[...] = _dot3(wsh_ref[...], wsl_ref[...], xh, xl, NT_DIMS)
    tm = h_ref.shape[0]
    key_ids = lax.broadcasted_iota(jnp.int32, (PEER_NKEYS, tm), 0).astype(F32)
    pair_ids = _pair_block_ids(tm)
    pair_ok = pair_ids < float(_PAIR_ID_BOUND)

    def head(hd, tie, exact):
        sa = qt_scr[pl.ds(pl.multiple_of(hd * 256, 256), 128), :]
        sb = qt_scr[pl.ds(pl.multiple_of(hd * 256 + 128, 128), 128), :]
        if exact:
            av, ai = _top16_rows(sa, key_ids, PEER_NKEYS)
            bv, bi = _top16_rows(sb, key_ids, PEER_NKEYS)
        else:
            av, ai, ta = _top16_network(sa)
            bv, bi, tb_ = _top16_network(sb)
            tie = jnp.maximum(tie, jnp.maximum(ta, tb_))
        cand = jnp.where(pair_ok, _pair_block_sums(av, bv), -jnp.inf)
        cv, flat = _top16_rows(cand, pair_ids, _PAIR_ID_BOUND)
        flat = flat.astype(jnp.int32)
        e_a = _take16(ai, flat >> 4)
        e_b = _take16(bi, flat & (PEER_TOPK - 1))
        ex = jnp.exp(cv - cv[0:1, :])
        rows = pl.ds(pl.multiple_of(hd * PEER_TOPK, PEER_TOPK), PEER_TOPK)
        et_scr[rows, :] = (e_a * float(PEER_NKEYS) + e_b) * float(PEER_ROW_WORDS)
        gt_scr[rows, :] = ex / jnp.sum(ex, axis=0, keepdims=True)
        return tie

    def head_pair(i, tie):
        return head(2 * i + 1, head(2 * i, tie, exact=False), exact=False)

    tie = lax.fori_loop(0, PEER_HEADS // 2, head_pair, jnp.zeros((SUBLANES, tm), F32))

    @pl.when(jnp.max(tie) > 0.0)
    def _():
        lax.fori_loop(0, PEER_HEADS, functools.partial(head, exact=True), jnp.zeros((SUBLANES, tm), F32))

    exp_ref[...] = et_scr[...].T.astype(jnp.int32)
    gate_ref[...] = gt_scr[...].T


def peer_route(h, nw, w_q, keys, tm):
    t, d = h.shape
    wsh, wsl = _split_bf16(score_weights(w_q, keys))
    nq = wsh.shape[0]
    return pl.pallas_call(
        _peer_route_kernel,
        grid=(t // tm,),
        in_specs=[
            pl.BlockSpec((tm, d), lambda i: (i, 0)),
            pl.BlockSpec((1, d), lambda i: (0, 0)),
            pl.BlockSpec((nq, d), lambda i: (0, 0)),
            pl.BlockSpec((nq, d), lambda i: (0, 0)),
        ],
        out_specs=[
            pl.BlockSpec((tm, SUBLANES, d // SUBLANES), lambda i: (i, 0, 0)),
            pl.BlockSpec((tm, PEER_HK), lambda i: (i, 0)),
            pl.BlockSpec((tm, PEER_HK), lambda i: (i, 0)),
        ],
        out_shape=[jax.ShapeDtypeStruct((t, SUBLANES, d // SUBLANES), F32),
                   jax.ShapeDtypeStruct((t, PEER_HK), jnp.int32), jax.ShapeDtypeStruct((t, PEER_HK), F32)],
        scratch_shapes=[pltpu.VMEM((nq, tm), F32), pltpu.VMEM((PEER_HK, tm), F32), pltpu.VMEM((PEER_HK, tm), F32)],
        compiler_params=pltpu.CompilerParams(dimension_semantics=("parallel",), vmem_limit_bytes=VMEM_LIMIT),
        name="peer_route",
    )(h, nw, wsh, wsl)


PEER_GROUP = 16
PEER_ROW_WORDS = 4


def _gather_group(idx_ref, tab_ref, g, stage_ref):
    rows = [idx_ref.at[g * PEER_GROUP + j] for j in range(PEER_GROUP)]
    for k in range(PEER_HK):
        for j in range(PEER_GROUP):
            off = pl.multiple_of(rows[j][k], PEER_ROW_WORDS)
            stage_ref[j, pl.ds(PEER_ROW_WORDS * k, PEER_ROW_WORDS), :] = tab_ref[pl.ds(off, PEER_ROW_WORDS), :]


def _gather_compute_pipeline(n_groups, idx_ref, tab_ref, consume, stage_a, stage_b):
    def compute(g, stage_ref):
        for j in range(PEER_GROUP):
            consume(g * PEER_GROUP + j, stage_ref.at[j])

    _gather_group(idx_ref, tab_ref, 0, stage_a)

    def body(i, carry):
        compute(2 * i, stage_a)
        _gather_group(idx_ref, tab_ref, 2 * i + 1, stage_b)
        compute(2 * i + 1, stage_b)
        _gather_group(idx_ref, tab_ref, 2 * i + 2, stage_a)
        return carry

    lax.fori_loop(0, n_groups // 2 - 1, body, 0)
    _gather_group(idx_ref, tab_ref, n_groups - 1, stage_b)
    compute(n_groups - 2, stage_a)
    compute(n_groups - 1, stage_b)


def _diag_mask():
    row = lax.broadcasted_iota(jnp.int32, (SUBLANES, SUBLANES * PEER_HK), 0)
    lane = lax.broadcasted_iota(jnp.int32, (SUBLANES, SUBLANES * PEER_HK), 1)
    return (lane & (SUBLANES - 1)) == row


def _peer_hidden_kernel(idx_ref, x_ref, g_ref, sel_ref, tab_ref, o_ref, stage_a, stage_b, part_scr):
    tb = x_ref.shape[0]
    diag = _diag_mask()

    def consume(t, rows_ref):
        u = pltpu.bitcast(rows_ref[...], BF16)
        xh, xl = _split_bf16(x_ref[t])
        x16 = jnp.concatenate([xh, xl], axis=0)
        out = lax.dot_general(x16, u, NT_DIMS, preferred_element_type=F32)
        o8 = out[0:SUBLANES] + out[SUBLANES:2 * SUBLANES]
        part_scr[pl.ds(t, 1), :] = jnp.sum(jnp.where(diag, o8, 0.0), axis=0, keepdims=True)

    _gather_compute_pipeline(tb // PEER_GROUP, idx_ref, tab_ref, consume, stage_a, stage_b)
    ph, plo = _split_bf16(part_scr[...])
    sel = sel_ref[...]
    hid = jnp.dot(ph, sel, preferred_element_type=F32) + jnp.dot(plo, sel, preferred_element_type=F32)
    o_ref[...] = g_ref[...] * _gelu(hid)


def peer_hidden(idx, x8, gates, sel, tab, tb):
    t = idx.shape[0]
    return pl.pallas_call(
        _peer_hidden_kernel,
        grid=(t // tb,),
        in_specs=[
            pl.BlockSpec((tb, PEER_HK), lambda i: (i, 0), memory_space=pltpu.SMEM),
            pl.BlockSpec((tb, SUBLANES, 128), lambda i: (i, 0, 0)),
            pl.BlockSpec((tb, PEER_HK), lambda i: (i, 0)),
            pl.BlockSpec((SUBLANES * PEER_HK, PEER_HK), lambda i: (0, 0)),
            pl.BlockSpec((PEER_ROW_WORDS * PEER_EXPERTS, 128), lambda i: (0, 0), pipeline_mode=pl.Buffered(1)),
        ],
        out_specs=pl.BlockSpec((tb, PEER_HK), lambda i: (i, 0)),
        out_shape=jax.ShapeDtypeStruct((t, PEER_HK), F32),
        scratch_shapes=[pltpu.VMEM((PEER_GROUP, PEER_ROW_WORDS * PEER_HK, 128), jnp.int32),
                        pltpu.VMEM((PEER_GROUP, PEER_ROW_WORDS * PEER_HK, 128), jnp.int32),
                        pltpu.VMEM((tb, SUBLANES * PEER_HK), F32)],
        compiler_params=pltpu.CompilerParams(dimension_semantics=("arbitrary",), vmem_limit_bytes=VMEM_LIMIT_BIG),
        name="peer_hidden",
    )(idx, x8, gates, sel, tab)


def _peer_out_kernel(idx_ref, w_ref, exp_ref, tab_ref, o_ref, stage_a, stage_b, wexp_scr):
    tb = w_ref.shape[0]
    diag = _diag_mask()
    wh, wl = _split_bf16(w_ref[...])
    ex = exp_ref[...]
    wexp_scr[...] = jnp.dot(wh, ex, preferred_element_type=F32) + jnp.dot(wl, ex, preferred_element_type=F32)

    def consume(t, rows_ref):
        v = pltpu.bitcast(rows_ref[...], BF16)
        w8 = jnp.where(diag, jnp.broadcast_to(wexp_scr[pl.ds(t, 1), :], (SUBLANES, SUBLANES * PEER_HK)), 0.0)
        w8h, w8l = _split_bf16(w8)
        w16 = jnp.concatenate([w8h, w8l], axis=0)
        out = jnp.dot(w16, v, preferred_element_type=F32)
        o_ref[t] = out[0:SUBLANES] + out[SUBLANES:2 * SUBLANES]

    _gather_compute_pipeline(tb // PEER_GROUP, idx_ref, tab_ref, consume, stage_a, stage_b)


def peer_out(idx, w, expand, tab, tb):
    t = idx.shape[0]
    return pl.pallas_call(
        _peer_out_kernel,
        grid=(t // tb,),
        in_specs=[
            pl.BlockSpec((tb, PEER_HK), lambda i: (i, 0), memory_space=pltpu.SMEM),
            pl.BlockSpec((tb, PEER_HK), lambda i: (i, 0)),
            pl.BlockSpec((PEER_HK, SUBLANES * PEER_HK), lambda i: (0, 0)),
            pl.BlockSpec((PEER_ROW_WORDS * PEER_EXPERTS, 128), lambda i: (0, 0), pipeline_mode=pl.Buffered(1)),
        ],
        out_specs=pl.BlockSpec((tb, SUBLANES, 128), lambda i: (i, 0, 0)),
        out_shape=jax.ShapeDtypeStruct((t, SUBLANES, 128), F32),
        scratch_shapes=[pltpu.VMEM((PEER_GROUP, PEER_ROW_WORDS * PEER_HK, 128), jnp.int32),
                        pltpu.VMEM((PEER_GROUP, PEER_ROW_WORDS * PEER_HK, 128), jnp.int32),
                        pltpu.VMEM((tb, SUBLANES * PEER_HK), F32)],
        compiler_params=pltpu.CompilerParams(dimension_semantics=("arbitrary",), vmem_limit_bytes=VMEM_LIMIT_BIG),
        name="peer_out",
    )(idx, w, expand, tab)


def _ple_kernel(h_ref, e8_ref, p_ref, nw_ref, wg_ref, wp_ref, fw_ref, o_ref, *, final_norm):
    h = h_ref[...] + jnp.concatenate([e8_ref[:, r, :] for r in range(SUBLANES)], axis=1)
    gate = _sigmoid(jnp.dot(_rms(h, nw_ref[...]).astype(BF16), wg_ref[...], preferred_element_type=F32))
    out = h + jnp.dot(p_ref[...].astype(BF16), wp_ref[...], preferred_element_type=F32) * gate
    if final_norm:
        out = _rms(out, fw_ref[...])
    o_ref[...] = out


def _ple_proj_kernel(h_ref, e8_ref, p_ref, nw_ref, wg_ref, wp_ref, nnw_ref, wn_ref, o_ref, u_ref):
    h = h_ref[...] + jnp.concatenate([e8_ref[:, r, :] for r in range(SUBLANES)], axis=1)
    gate = _sigmoid(jnp.dot(_rms(h, nw_ref[...]).astype(BF16), wg_ref[...], preferred_element_type=F32))
    out = h + jnp.dot(p_ref[...].astype(BF16), wp_ref[...], preferred_element_type=F32) * gate
    o_ref[...] = out
    u_ref[...] = jnp.dot(_rms(out, nnw_ref[...]).astype(BF16), wn_ref[...], preferred_element_type=F32)


def ple_proj(h, e8, p, layer, nw, wg, wp, next_nw, next_w, tm):
    t, d = h.shape
    pd = p.shape[1]
    n = next_w.shape[1]
    first = layer * (t // tm)
    return pl.pallas_call(
        _ple_proj_kernel,
        grid=(t // tm,),
        in_specs=[
            pl.BlockSpec((tm, d), lambda i: (i, 0)),
            pl.BlockSpec((tm, SUBLANES, d // SUBLANES), lambda i: (i, 0, 0)),
            pl.BlockSpec((tm, pd), lambda i: (first + i, 0)),
            pl.BlockSpec((1, d), lambda i: (0, 0)),
            pl.BlockSpec((d, d), lambda i: (0, 0)),
            pl.BlockSpec((pd, d), lambda i: (0, 0)),
            pl.BlockSpec((1, d), lambda i: (0, 0)),
            pl.BlockSpec((d, n), lambda i: (0, 0)),
        ],
        out_specs=[pl.BlockSpec((tm, d), lambda i: (i, 0)), pl.BlockSpec((tm, n), lambda i: (i, 0))],
        out_shape=[jax.ShapeDtypeStruct((t, d), F32), jax.ShapeDtypeStruct((t, n), F32)],
        compiler_params=pltpu.CompilerParams(dimension_semantics=("parallel",), vmem_limit_bytes=VMEM_LIMIT),
        name="ple_proj",
    )(h, e8, p, nw, wg, wp, next_nw, next_w)


def ple(h, e8, p, layer, nw, wg, wp, fw, tm, final_norm):
    t, d = h.shape
    pd = p.shape[1]
    first = layer * (t // tm)
    return pl.pallas_call(
        functools.partial(_ple_kernel, final_norm=final_norm),
        grid=(t // tm,),
        in_specs=[
            pl.BlockSpec((tm, d), lambda i: (i, 0)),
            pl.BlockSpec((tm, SUBLANES, d // SUBLANES), lambda i: (i, 0, 0)),
            pl.BlockSpec((tm, pd), lambda i: (first + i, 0)),
            pl.BlockSpec((1, d), lambda i: (0, 0)),
            pl.BlockSpec((d, d), lambda i: (0, 0)),
            pl.BlockSpec((pd, d), lambda i: (0, 0)),
            pl.BlockSpec((1, d), lambda i: (0, 0)),
        ],
        out_specs=pl.BlockSpec((tm, d), lambda i: (i, 0)),
        out_shape=jax.ShapeDtypeStruct((t, d), F32),
        compiler_params=pltpu.CompilerParams(dimension_semantics=("parallel",), vmem_limit_bytes=VMEM_LIMIT),
        name="ple",
    )(h, e8, p, nw, wg, wp, fw)


def _pack_table_kernel(t_ref, o_ref):
    x = t_ref[...]
    te = x.shape[0]
    for s in range(PEER_ROW_WORDS):
        lo = pltpu.bitcast(x[:, 256 * s:256 * s + 128].astype(BF16).astype(F32), jnp.int32)
        hi = pltpu.bitcast(x[:, 256 * s + 128:256 * s + 256].astype(BF16).astype(F32), jnp.int32)
        o_ref[pl.ds(s, te, stride=PEER_ROW_WORDS), :] = (hi & jnp.int32(-65536)) | lax.shift_right_logical(lo, 16)


def _pack_table(tabs, layer=0):
    e, d = tabs.shape[-2:]
    tabs = tabs.reshape(-1, d)
    te = min(512, e)
    first = layer * (e // te)
    return pl.pallas_call(
        _pack_table_kernel,
        grid=(e // te,),
        in_specs=[pl.BlockSpec((te, d), lambda i: (first + i, 0))],
        out_specs=pl.BlockSpec((PEER_ROW_WORDS * te, 128), lambda i: (i, 0)),
        out_shape=jax.ShapeDtypeStruct((PEER_ROW_WORDS * e, 128), jnp.int32),
        compiler_params=pltpu.CompilerParams(dimension_semantics=("parallel",), vmem_limit_bytes=VMEM_LIMIT),
        name="pack_table",
    )(tabs)


def _even_in_perm():
    off = np.cumsum([0, 256, 256, 512, 512, 16, 256, 256, 512, 512])
    qa, ka, va, ga, ra, qb, kb, vb, gb = [np.arange(off[i], off[i + 1]) for i in range(9)]
    half = RET_DK // 2
    rot = np.concatenate([np.concatenate([np.arange(h * RET_DK, h * RET_DK + half) for h in range(RET_HEADS)]),
                          np.concatenate([np.arange(h * RET_DK + half, (h + 1) * RET_DK) for h in range(RET_HEADS)])])
    return np.concatenate([qa, ka, qb[rot], kb[rot], va, ga, vb, gb, ra])


def _retention_tables():
    lg = np.log(1.0 - 2.0 ** (-5.0 - np.arange(RET_HEADS, dtype=np.float64)))
    idx = np.arange(CHUNK, dtype=np.float64)
    diff = idx[:, None] - idx[None, :]
    dmask = np.where(diff >= 0, np.exp(lg[:, None, None] * np.maximum(diff, 0.0)), 0.0)
    qdec = np.exp(lg[:, None] * (idx + 1.0))
    kdec = np.exp(lg[:, None] * (CHUNK - 1.0 - idx))
    cdec = np.exp(lg * CHUNK)
    per_col = lambda t: np.repeat(t.T, RET_DV, axis=1)
    lanes = np.arange(256)
    rows_head = np.repeat(np.arange(RET_HEADS), RET_DV)[:, None]
    half = RET_DK // 2
    hmask_g = (lanes[None, :] // GLA_DK) == rows_head
    hmask_r = ((lanes[None, :] % 128) // half == rows_head) & ((lanes[None, :] % 128) < RET_HEADS * half)
    return (jnp.asarray(dmask, F32), jnp.asarray(per_col(kdec), F32), jnp.asarray(per_col(qdec), F32),
            jnp.asarray(np.broadcast_to(np.repeat(cdec, RET_DV)[:, None], (RET_HEADS * RET_DV, 256)), F32),
            jnp.asarray(hmask_g, F32), jnp.asarray(hmask_r, F32))


def _peer_layer(h, nw, w_q, sub_keys, u_tabs, v_tabs, layer, tm_route, tb):
    keys = sub_keys.reshape(2 * PEER_HEADS, PEER_NKEYS, -1)
    xn8, experts, gates = peer_route(h, nw, w_q, keys, tm_route)
    kk = np.arange(SUBLANES * PEER_HK) // SUBLANES
    sel = jnp.asarray(kk[:, None] == np.arange(PEER_HK)[None, :], BF16)
    w = peer_hidden(experts, xn8, gates, sel, _pack_table(u_tabs, layer), tb)
    return peer_out(experts, w, sel.T, _pack_table(v_tabs, layer), tb)


def _row(v):
    return v.reshape(1, -1).astype(F32)


def _layer0_mixer(h, w, bsz, seq):
    t = h.shape[0]
    w_in = w["ev_w_in"][0]
    w_in0 = jnp.pad(w_in[:, _even_in_perm()], ((0, 0), (0, EVEN_COLS - w_in.shape[1]))).astype(BF16)
    proj = norm_matmul(h, _row(w["norm_mix_w"][0]), w_in0, min(256, t))
    half = RET_DK // 2
    freqs = ROPE_BASE ** (-np.arange(half, dtype=np.float32) / half)
    dmask, kdec, qdec, cdec, hmask_g, hmask_r = _retention_tables()
    consts = (
        jnp.asarray(np.tile(freqs, RET_HEADS)[None, :], F32),
        jnp.pad(w["ev_gla_w_up"][0], ((0, 128 - GLA_GATE_RANK), (0, 0))).astype(BF16),
        _row(w["ev_gla_b_up"][0]), _row(w["ev_gla_norm_w"][0]), _row(w["ev_ret_norm_w"][0]),
        w["ev_w_out"][0].astype(BF16),
        jnp.asarray(np.tril(np.ones((CHUNK, CHUNK))), BF16),
        dmask, kdec, qdec, cdec, hmask_g, hmask_r,
    )
    pos = w["positions"].astype(F32).reshape(seq, 1)
    return even_mixer(h, proj, pos, consts, bsz, seq, min(256, seq))


def _layer1_mixer(h, u, w, bsz, seq):
    pw_re, pw_im, bb_re, bb_im = s5_params(w["od_s5_a_re"][0], w["od_s5_a_im"][0], w["od_s5_log_dt"][0].reshape(-1, 1),
                                           w["od_s5_b_re"][0].transpose(0, 2, 1), w["od_s5_b_im"][0].transpose(0, 2, 1))
    eye = jnp.eye(S5_GROUPS, dtype=F32)
    blockdiag = lambda m: (m[:, :, None, :] * eye[:, None, :, None]).reshape(m.shape[0] * m.shape[1], -1)
    consts = (
        w["od_pool_w"][0].astype(BF16), _row(w["od_pool_scale"][0]),
        blockdiag(bb_re).astype(BF16), blockdiag(bb_im).astype(BF16),
        blockdiag(w["od_s5_c_re"][0]).T.astype(BF16), blockdiag(w["od_s5_c_im"][0]).T.astype(BF16),
        pw_re.reshape(SUBLANES, S5_STATE), pw_im.reshape(SUBLANES, S5_STATE),
        _row(w["od_s5_d"][0]), w["od_s5_w_glu"][0].astype(BF16), _row(w["od_s5_b_glu"][0]), w["od_w_out"][0].astype(BF16),
    )
    return odd_mixer(h, u, consts, bsz, seq, min(256, seq))


def kernel(x, p, positions, norm_mix_w, norm_ffn_w, norm_ple_w, final_norm_w, ev_w_in, ev_gla_w_up, ev_gla_b_up, ev_gla_norm_w, ev_ret_norm_w, ev_w_out, od_w_in, od_pool_w, od_pool_scale, od_s5_a_re, od_s5_a_im, od_s5_log_dt, od_s5_b_re, od_s5_b_im, od_s5_c_re, od_s5_c_im, od_s5_d, od_s5_w_glu, od_s5_b_glu, od_w_out, peer_w_q, peer_sub_keys, peer_u, peer_v, ple_w_proj, ple_w_gate):
    w = dict(locals())
    bsz, seq, d = x.shape
    t = bsz * seq
    tm_tok = min(512, t)
    tm_route = min(256, t)
    tb = min(512, t)
    h = x.reshape(t, d)
    p2 = p.reshape(-1, p.shape[-1])
    h = _layer0_mixer(h, w, bsz, seq)
    e8 = _peer_layer(h, _row(norm_ffn_w[0]), peer_w_q[0], peer_sub_keys[0], peer_u, peer_v, 0, tm_route, tb)
    h, u = ple_proj(h, e8, p2, 0, _row(norm_ple_w[0]), ple_w_gate[0].astype(BF16), ple_w_proj[0].astype(BF16),
                    _row(norm_mix_w[1]), od_w_in[0].astype(BF16), tm_tok)
    h = _layer1_mixer(h, u, w, bsz, seq)
    e8 = _peer_layer(h, _row(norm_ffn_w[1]), peer_w_q[1], peer_sub_keys[1], peer_u, peer_v, 1, tm_route, tb)
    h = ple(h, e8, p2, 1, _row(norm_ple_w[1]), ple_w_gate[1].astype(BF16), ple_w_proj[1].astype(BF16),
            _row(final_norm_w), tm_tok, True)
    return h.reshape(bsz, seq, d)
```
